```python
import math
import jax, jax.numpy as jnp
from jax import lax
import numpy as np

D_MODEL = 1024
BATCH = 1
SEQ = 16384
DEPTH = 1
DEC_BATCH = 128
DEC_SEQ = 8
PAST_LEN = 16384
PAGE_SIZE = 128

MLA_HEADS = 8
MLA_NOPE_DIM = 64
MLA_ROPE_DIM = 32
MLA_V_DIM = 64
MLA_Q_RANK = 384
MLA_KV_RANK = 256
MLA_SCALE = (MLA_NOPE_DIM + MLA_ROPE_DIM) ** -0.5
Q_BLOCK = 128
ROPE_THETA = 10000.0
GLA_HEADS = 4
GLA_K_DIM = 64
GLA_V_DIM = 128
GLA_GATE_RANK = 16
GLA_GATE_NORMALIZER = 16.0
GLA_CHUNK = 64
GLA_QK_WIDTH = GLA_HEADS * GLA_K_DIM
GLA_V_WIDTH = GLA_HEADS * GLA_V_DIM
MLA_OUT_WIDTH = MLA_HEADS * MLA_V_DIM
MIX_WIDTH = MLA_OUT_WIDTH + GLA_V_WIDTH
IN_SIZES = (MLA_Q_RANK, MLA_KV_RANK, MLA_ROPE_DIM, GLA_QK_WIDTH, GLA_QK_WIDTH,
            GLA_V_WIDTH, GLA_GATE_RANK, GLA_V_WIDTH)
D_IN = MLA_Q_RANK + MLA_KV_RANK + MLA_ROPE_DIM + 2 * GLA_QK_WIDTH + 2 * GLA_V_WIDTH + GLA_GATE_RANK
MEM_TOKENS = 256
MEM_HEADS = 4
MEM_HEAD_DIM = 128
MEM_WIDTH = MEM_HEADS * MEM_HEAD_DIM
N_EXPERTS = 32
TOP_K = 4
D_FF = 1024
SWIGLU_LIMIT = 7.0
SWIGLU_ALPHA = 1.702
EPS = 1e-6

kernel_name = "hymba_mla_gla_moe_memxattn_step"


def rmsnorm(x, g):
    xf = x.astype(jnp.float32)
    var = jnp.mean(xf * xf, axis=-1, keepdims=True)
    return (xf * lax.rsqrt(var + EPS)).astype(x.dtype) * g


def rope(x, pos):
    half = x.shape[-1] // 2
    inv = ROPE_THETA ** (-jnp.arange(half, dtype=jnp.float32) / half)
    ang = pos.astype(jnp.float32)[:, None] * inv[None, :]
    cos = jnp.cos(ang)[None, :, None, :].astype(x.dtype)
    sin = jnp.sin(ang)[None, :, None, :].astype(x.dtype)
    x1, x2 = x[..., :half], x[..., half:]
    return jnp.concatenate([x1 * cos - x2 * sin, x1 * sin + x2 * cos], axis=-1)


def split_columns(z, sizes):
    out, start = [], 0
    for s in sizes:
        out.append(z[..., start:start + s])
        start += s
    return out


def mla_attention(q_lat, q_rope, q_pos, ckv_all, kr_all):
    B, T, H, C = q_lat.shape
    k_pos = jnp.arange(ckv_all.shape[1])
    qb = math.gcd(T, Q_BLOCK)
    nb = T // qb

    def blocks(a):
        return a.reshape(B, nb, qb, *a.shape[2:]).swapaxes(0, 1)

    def one_block(args):
        ql, qr, qp = args
        s = (jnp.einsum("bqhc,bkc->bhqk", ql, ckv_all)
             + jnp.einsum("bqhr,bkr->bhqk", qr, kr_all)).astype(jnp.float32) * MLA_SCALE
        s = jnp.where((k_pos[None, :] <= qp[:, None])[None, None], s, -jnp.inf)
        p = jax.nn.softmax(s, axis=-1).astype(ckv_all.dtype)
        return jnp.einsum("bhqk,bkc->bqhc", p, ckv_all)

    o = lax.map(one_block, (blocks(q_lat), blocks(q_rope), q_pos.reshape(nb, qb)))
    return o.swapaxes(0, 1).reshape(B, T, H, C)


def gla_chunked(q, k, v, glog, state0):
    B, T, H, dk = q.shape
    dv = v.shape[-1]
    C = math.gcd(T, GLA_CHUNK)
    N = T // C

    def to_chunks(a):
        return a.reshape(B, N, C, H, a.shape[-1]).transpose(1, 0, 3, 2, 4)

    causal = jnp.tril(jnp.ones((C, C), dtype=bool))

    def step(S, inp):
        qb, kb, vb, gb = inp
        bcum = jnp.cumsum(gb.astype(jnp.float32), axis=2)
        inter = jnp.einsum("bhcd,bhde->bhce", qb * jnp.exp(bcum), S)
        diff = bcum[:, :, :, None, :] - bcum[:, :, None, :, :]
        decay = jnp.exp(jnp.where(causal[None, None, :, :, None], diff, -jnp.inf))
        attn = jnp.einsum("bhtd,bhsd,bhtsd->bhts", qb.astype(jnp.float32), kb.astype(jnp.float32), decay)
        intra = jnp.einsum("bhts,bhse->bhte", attn, vb.astype(jnp.float32))
        blast = bcum[:, :, -1:, :]
        k_dec = kb * jnp.exp(blast - bcum)
        S_new = jnp.exp(blast[:, :, 0, :])[..., None] * S + jnp.einsum("bhsd,bhse->bhde", k_dec, vb.astype(jnp.float32))
        return S_new, inter + intra

    S, o = lax.scan(step, state0.astype(jnp.float32), (to_chunks(q), to_chunks(k), to_chunks(v), to_chunks(glog)))
    o = o.transpose(1, 0, 3, 2, 4).reshape(B, T, H, dv)
    return o.astype(v.dtype), S.astype(state0.dtype)


def mixing_sublayer(h, ckv_past, kr_past, gla_state0, norm_g, w_in, q_a_norm_g, w_uq,
                    kv_a_norm_g, w_uk, w_uv, gla_gate_w2, gla_gate_b, gla_norm_g, w_out):
    B, T, _ = h.shape
    P = ckv_past.shape[1]
    pos = P + jnp.arange(T)
    z = rmsnorm(h, norm_g) @ w_in
    cq, ckv, kr, gq, gk, gv, gg, go = split_columns(z, IN_SIZES)
    q = (rmsnorm(cq, q_a_norm_g) @ w_uq).reshape(B, T, MLA_HEADS, MLA_NOPE_DIM + MLA_ROPE_DIM)
    q_rope = rope(q[..., MLA_NOPE_DIM:], pos)
    q_lat = jnp.einsum("bthn,chn->bthc", q[..., :MLA_NOPE_DIM], w_uk)
    ckv = rmsnorm(ckv, kv_a_norm_g)
    kr = rope(kr[:, :, None, :], pos)[:, :, 0, :]
    ckv_all = jnp.concatenate([ckv_past, ckv], axis=1)
    kr_all = jnp.concatenate([kr_past, kr], axis=1)
    o_lat = mla_attention(q_lat, q_rope, pos, ckv_all, kr_all)
    mla_o = jnp.einsum("bthc,chv->bthv", o_lat, w_uv).reshape(B, T, MLA_OUT_WIDTH)
    gq = gq.reshape(B, T, GLA_HEADS, GLA_K_DIM) * (GLA_K_DIM ** -0.5)
    gk = gk.reshape(B, T, GLA_HEADS, GLA_K_DIM)
    gv = gv.reshape(B, T, GLA_HEADS, GLA_V_DIM)
    glog = jax.nn.log_sigmoid((gg @ gla_gate_w2 + gla_gate_b).astype(jnp.float32)) / GLA_GATE_NORMALIZER
    glog = glog.reshape(B, T, GLA_HEADS, GLA_K_DIM)
    o_gla, gla_state = gla_chunked(gq, gk, gv, glog, gla_state0)
    o_gla = rmsnorm(o_gla, gla_norm_g) * jax.nn.silu(go.reshape(B, T, GLA_HEADS, GLA_V_DIM))
    o = jnp.concatenate([mla_o, o_gla.reshape(B, T, GLA_V_WIDTH)], axis=-1)
    return h + o @ w_out, ckv, kr, gla_state


def memory_kv(mem, norm_g, w_k, w_v):
    B, M, _ = mem.shape
    mn = rmsnorm(mem, norm_g)
    k = (mn @ w_k).reshape(B, M, MEM_HEADS, MEM_HEAD_DIM)
    v = (mn @ w_v).reshape(B, M, MEM_HEADS, MEM_HEAD_DIM)
    return k, v


def memory_sublayer(h, mem_k, mem_v, norm_g, w_q, w_o):
    B, T, _ = h.shape
    q = (rmsnorm(h, norm_g) @ w_q).reshape(B, T, MEM_HEADS, MEM_HEAD_DIM)
    s = jnp.einsum("bthd,bmhd->bhtm", q, mem_k).astype(jnp.float32) * (MEM_HEAD_DIM ** -0.5)
    p = jax.nn.softmax(s, axis=-1).astype(h.dtype)
    o = jnp.einsum("bhtm,bmhd->bthd", p, mem_v).reshape(B, T, MEM_WIDTH)
    return h + o @ w_o


def moe_sublayer(h, norm_g, w_router, b_router, w_up, b_up, w_down, b_down):
    B, T, D = h.shape
    xt = rmsnorm(h, norm_g).reshape(B * T, D)
    logits = (xt @ w_router + b_router).astype(jnp.float32)
    top_v, top_i = lax.top_k(logits, TOP_K)
    gates = jax.nn.softmax(top_v, axis=-1)
    combine = jnp.sum(jax.nn.one_hot(top_i, N_EXPERTS, dtype=jnp.float32) * gates[..., None], axis=1).astype(xt.dtype)
    out = jnp.zeros_like(xt)
    for e in range(N_EXPERTS):
        hu = xt @ w_up[e] + b_up[e]
        gate = jnp.minimum(hu[:, :D_FF], SWIGLU_LIMIT)
        up = jnp.clip(hu[:, D_FF:], -SWIGLU_LIMIT, SWIGLU_LIMIT)
        act = (up + 1.0) * gate * jax.nn.sigmoid(SWIGLU_ALPHA * gate)
        out = out + combine[:, e:e + 1] * (act @ w_down[e] + b_down[e])
    return h + out.reshape(B, T, D)


def setup_inputs(seed: int = 0) -> dict:
    key = jax.random.key(seed)
    keys = jax.random.split(key, 48)
    counter = iter(range(48))
    f32 = jnp.float32

    def nrm(shape, scale=1.0):
        return jax.random.normal(keys[next(counter)], shape, f32) * scale

    def gain(shape):
        return 1.0 + nrm(shape, 0.05)

    L = DEPTH
    n_pages = PAST_LEN // PAGE_SIZE
    n_used = DEC_BATCH * n_pages
    n_phys = (n_used * 5) // 4
    perm = jax.random.permutation(keys[next(counter)], n_phys)
    page_table = perm[:n_used].reshape(DEC_BATCH, n_pages).astype(jnp.int32)
    return {
        "x_prompt": nrm((BATCH, SEQ, D_MODEL)),
        "x_sample": nrm((DEC_BATCH, DEC_SEQ, D_MODEL)),
        "mem_prompt": nrm((BATCH, MEM_TOKENS, D_MODEL)),
        "cache_ckv": nrm((L, n_phys, PAGE_SIZE, MLA_KV_RANK)),
        "cache_krope": nrm((L, n_phys, PAGE_SIZE, MLA_ROPE_DIM)),
        "state_gla": nrm((L, DEC_BATCH, GLA_HEADS, GLA_K_DIM, GLA_V_DIM)),
        "cache_mem_k": nrm((L, DEC_BATCH, MEM_TOKENS, MEM_HEADS, MEM_HEAD_DIM)),
        "cache_mem_v": nrm((L, DEC_BATCH, MEM_TOKENS, MEM_HEADS, MEM_HEAD_DIM)),
        "page_table": page_table,
        "norm_mix_g": gain((L, D_MODEL)),
        "w_in": nrm((L, D_MODEL, D_IN), D_MODEL ** -0.5),
        "q_a_norm_g": gain((L, MLA_Q_RANK)),
        "w_uq": nrm((L, MLA_Q_RANK, MLA_HEADS * (MLA_NOPE_DIM + MLA_ROPE_DIM)), MLA_Q_RANK ** -0.5),
        "kv_a_norm_g": gain((L, MLA_KV_RANK)),
        "w_uk": nrm((L, MLA_KV_RANK, MLA_HEADS, MLA_NOPE_DIM), MLA_KV_RANK ** -0.5),
        "w_uv": nrm((L, MLA_KV_RANK, MLA_HEADS, MLA_V_DIM), MLA_KV_RANK ** -0.5),
        "gla_gate_w2": nrm((L, GLA_GATE_RANK, GLA_QK_WIDTH), GLA_GATE_RANK ** -0.5),
        "gla_gate_b": nrm((L, GLA_QK_WIDTH), 0.1),
        "gla_norm_g": gain((L, GLA_V_DIM)),
        "w_out": nrm((L, MIX_WIDTH, D_MODEL), MIX_WIDTH ** -0.5),
        "norm_mem_g": gain((L, D_MODEL)),
        "mem_in_norm_g": gain((L, D_MODEL)),
        "w_mem_q": nrm((L, D_MODEL, MEM_WIDTH), D_MODEL ** -0.5),
        "w_mem_k": nrm((L, D_MODEL, MEM_WIDTH), D_MODEL ** -0.5),
        "w_mem_v": nrm((L, D_MODEL, MEM_WIDTH), D_MODEL ** -0.5),
        "w_mem_o": nrm((L, MEM_WIDTH, D_MODEL), MEM_WIDTH ** -0.5),
        "norm_moe_g": gain((L, D_MODEL)),
        "w_router": nrm((L, D_MODEL, N_EXPERTS), D_MODEL ** -0.5),
        "b_router": nrm((L, N_EXPERTS), 0.01),
        "w_moe_up": nrm((L, N_EXPERTS, D_MODEL, 2 * D_FF), D_MODEL ** -0.5),
        "b_moe_up": nrm((L, N_EXPERTS, 2 * D_FF), 0.01),
        "w_moe_down": nrm((L, N_EXPERTS, D_FF, D_MODEL), D_FF ** -0.5),
        "b_moe_down": nrm((L, N_EXPERTS, D_MODEL), 0.01),
        "norm_final_g": gain((D_MODEL,)),
    }


def reference(x_prompt, x_sample, mem_prompt, cache_ckv, cache_krope, state_gla, cache_mem_k,
              cache_mem_v, page_table, norm_mix_g, w_in, q_a_norm_g, w_uq, kv_a_norm_g, w_uk,
              w_uv, gla_gate_w2, gla_gate_b, gla_norm_g, w_out, norm_mem_g, mem_in_norm_g,
              w_mem_q, w_mem_k, w_mem_v, w_mem_o, norm_moe_g, w_router, b_router, w_moe_up,
              b_moe_up, w_moe_down, b_moe_down, norm_final_g):
    B = x_prompt.shape[0]
    DB = x_sample.shape[0]
    hp, hs = x_prompt, x_sample
    ckv_p_l, kr_p_l, gla_p_l, mk_p_l, mv_p_l, ckv_s_l, kr_s_l, gla_s_l = [], [], [], [], [], [], [], []
    for l in range(DEPTH):
        mix_w = (norm_mix_g[l], w_in[l], q_a_norm_g[l], w_uq[l], kv_a_norm_g[l], w_uk[l], w_uv[l],
                 gla_gate_w2[l], gla_gate_b[l], gla_norm_g[l], w_out[l])
        empty_ckv = jnp.zeros((B, 0, MLA_KV_RANK), hp.dtype)
        empty_kr = jnp.zeros((B, 0, MLA_ROPE_DIM), hp.dtype)
        gla0 = jnp.zeros((B, GLA_HEADS, GLA_K_DIM, GLA_V_DIM), state_gla.dtype)
        hp, ckv_p, kr_p, gla_p = mixing_sublayer(hp, empty_ckv, empty_kr, gla0, *mix_w)
        past_ckv = cache_ckv[l][page_table].reshape(DB, -1, MLA_KV_RANK)
        past_kr = cache_krope[l][page_table].reshape(DB, -1, MLA_ROPE_DIM)
        hs, ckv_s, kr_s, gla_s = mixing_sublayer(hs, past_ckv, past_kr, state_gla[l], *mix_w)
        mk_p, mv_p = memory_kv(mem_prompt, mem_in_norm_g[l], w_mem_k[l], w_mem_v[l])
        hp = memory_sublayer(hp, mk_p, mv_p, norm_mem_g[l], w_mem_q[l], w_mem_o[l])
        hs = memory_sublayer(hs, cache_mem_k[l], cache_mem_v[l], norm_mem_g[l], w_mem_q[l], w_mem_o[l])
        moe_w = (norm_moe_g[l], w_router[l], b_router[l], w_moe_up[l], b_moe_up[l], w_moe_down[l], b_moe_down[l])
        hp = moe_sublayer(hp, *moe_w)
        hs = moe_sublayer(hs, *moe_w)
        ckv_p_l.append(ckv_p); kr_p_l.append(kr_p); gla_p_l.append(gla_p)
        mk_p_l.append(mk_p); mv_p_l.append(mv_p)
        ckv_s_l.append(ckv_s); kr_s_l.append(kr_s); gla_s_l.append(gla_s)
    y_prompt = rmsnorm(hp, norm_final_g)
    y_sample = rmsnorm(hs, norm_final_g)
    ckv_prompt = jnp.stack(ckv_p_l)
    krope_prompt = jnp.stack(kr_p_l)
    gla_prompt = jnp.stack(gla_p_l)
    mem_k_prompt = jnp.stack(mk_p_l)
    mem_v_prompt = jnp.stack(mv_p_l)
    ckv_sample = jnp.stack(ckv_s_l)
    krope_sample = jnp.stack(kr_s_l)
    gla_sample = jnp.stack(gla_s_l)
    return (y_prompt, y_sample, ckv_prompt, krope_prompt, gla_prompt, mem_k_prompt, mem_v_prompt,
            ckv_sample, krope_sample, gla_sample)
```

```python
import functools
import math

import jax
import jax.numpy as jnp
from jax import lax
from jax.experimental import pallas as pl
from jax.experimental.pallas import tpu as pltpu

F32 = jnp.float32
BF16 = jnp.bfloat16

D_MODEL = 1024
MLA_HEADS = 8
MLA_NOPE = 64
MLA_ROPE = 32
MLA_V = 64
MLA_Q_RANK = 384
MLA_KV_RANK = 256
MLA_SCALE = (MLA_NOPE + MLA_ROPE) ** -0.5
ROPE_THETA = 10000.0
QK_WIDTH = 384
GLA_HEADS = 4
GLA_DK = 64
GLA_DV = 128
GLA_GATE_RANK = 16
GLA_GATE_NORMALIZER = 16.0
GLA_CHUNK = 64
GLA_SUB = 16
MEM_TOKENS = 256
MEM_HEADS = 4
MEM_HEAD_DIM = 128
MEM_WIDTH = MEM_HEADS * MEM_HEAD_DIM
N_EXPERTS = 32
TOP_K = 4
D_FF = 1024
SWIGLU_LIMIT = 7.0
SWIGLU_ALPHA = 1.702
EPS = 1e-6
PAGE = 128
NEG = -1e30

VMEM_LIMIT = 56 * 1024 * 1024


def _cparams(sem):
    return pltpu.CompilerParams(dimension_semantics=sem, vmem_limit_bytes=VMEM_LIMIT)


def _rms(x, g):
    var = jnp.mean(x * x, axis=-1, keepdims=True)
    return x * lax.rsqrt(var + EPS) * g


def _bdot(a, b):
    return jnp.dot(a.astype(BF16), b.astype(BF16), preferred_element_type=F32)


def _bdot_nt(a, b):
    return lax.dot_general(a.astype(BF16), b.astype(BF16), (((1,), (1,)), ((), ())),
                           preferred_element_type=F32)


def _bdot_tn(a, b):
    return lax.dot_general(a.astype(BF16), b.astype(BF16), (((0,), (0,)), ((), ())),
                           preferred_element_type=F32)


def _split_dot(a, b_exact):
    hi = a.astype(BF16)
    r1 = a - hi.astype(F32)
    mid = r1.astype(BF16)
    lo = (r1 - mid.astype(F32)).astype(BF16)
    return (jnp.dot(hi, b_exact, preferred_element_type=F32)
            + jnp.dot(mid, b_exact, preferred_element_type=F32)
            + jnp.dot(lo, b_exact, preferred_element_type=F32))


def _full(shape):
    n = len(shape)
    return pl.BlockSpec(shape, lambda *_: (0,) * n)


def _mem_kv_kernel(mem_ref, g_ref, wk_ref, wv_ref, k_ref, v_ref):
    mn = _rms(mem_ref[...], g_ref[...]).astype(BF16)
    k_ref[...] = jnp.dot(mn, wk_ref[...], preferred_element_type=F32)
    v_ref[...] = jnp.dot(mn, wv_ref[...], preferred_element_type=F32)


def _mem_kv(mem, g, wk, wv):
    m = mem.shape[0]
    return pl.pallas_call(
        _mem_kv_kernel,
        out_shape=(jax.ShapeDtypeStruct((m, MEM_WIDTH), F32), jax.ShapeDtypeStruct((m, MEM_WIDTH), F32)),
        name="mem_kv",
    )(mem, g, wk, wv)


_C_CQ, _C_CKV, _C_GQ, _C_GK, _C_GV, _C_GO, _C_END = 0, 384, 640, 896, 1152, 1664, 2176


def _log_sigmoid(x):
    return jnp.minimum(x, 0.0) - jnp.log1p(jnp.exp(-jnp.abs(x)))


def _mix_in_kernel(h_ref, cs_ref, sn_ref, g_ref, w1_ref, w2_ref, qag_ref, wuqn_ref, wuqr_ref, wuqrr_ref,
                   kvag_ref, wukt_ref, wg2_ref, bg_ref,
                   q_ref, ckv_ref, kr_ref, kcat_ref, gq_ref, gk_ref, gv_ref, glog_ref, go_ref):
    xb = _rms(h_ref[...], g_ref[...]).astype(BF16)
    z1 = jnp.dot(xb, w1_ref[...], preferred_element_type=F32)
    z2 = jnp.dot(xb, w2_ref[...], preferred_element_type=F32)
    cs = cs_ref[...]
    sn = sn_ref[...]
    ckv = _rms(z1[:, _C_CKV:_C_GQ], kvag_ref[...])
    ckv_ref[...] = ckv
    krp = z2[:, 0:128] * cs + z2[:, 128:256] * sn
    kr_ref[...] = krp[:, 0:MLA_ROPE]
    kcat_ref[:, 0:MLA_KV_RANK] = ckv.astype(BF16)
    kcat_ref[:, MLA_KV_RANK:QK_WIDTH] = krp.astype(BF16)
    cqn = _rms(z1[:, _C_CQ:_C_CKV], qag_ref[...]).astype(BF16)
    qn = jnp.dot(cqn, wuqn_ref[...], preferred_element_type=F32)
    csw = jnp.concatenate([cs] * MLA_HEADS, axis=1)
    snw = jnp.concatenate([sn] * MLA_HEADS, axis=1)
    qr = (jnp.dot(cqn, wuqr_ref[...], preferred_element_type=F32) * csw
          + jnp.dot(cqn, wuqrr_ref[...], preferred_element_type=F32) * snw)
    for h in range(MLA_HEADS):
        ql = _bdot(qn[:, h * MLA_NOPE:(h + 1) * MLA_NOPE], wukt_ref[h])
        q_ref[h, :, 0:MLA_KV_RANK] = (ql * MLA_SCALE).astype(BF16)
        q_ref[h, :, MLA_KV_RANK:QK_WIDTH] = (qr[:, h * 128:(h + 1) * 128] * MLA_SCALE).astype(BF16)
    gq_ref[...] = z1[:, _C_GQ:_C_GK]
    gk_ref[...] = z1[:, _C_GK:_C_GV]
    gv_ref[...] = z1[:, _C_GV:_C_GO]
    go_ref[...] = z1[:, _C_GO:_C_END]
    gate = _bdot(z2[:, 256:384], wg2_ref[...]) + bg_ref[...]
    glog_ref[...] = _log_sigmoid(gate) * (1.0 / GLA_GATE_NORMALIZER)


def _mix_in(h, cs, sn, wts, tm):
    t = h.shape[0]
    grid = (t // tm,)
    row = lambda w: pl.BlockSpec((tm, w), lambda i: (i, 0))
    in_specs = [row(D_MODEL), row(128), row(128)] + [_full(w.shape) for w in wts]
    out_shape = (
        jax.ShapeDtypeStruct((MLA_HEADS, t, QK_WIDTH), BF16),
        jax.ShapeDtypeStruct((t, MLA_KV_RANK), F32),
        jax.ShapeDtypeStruct((t, MLA_ROPE), F32),
        jax.ShapeDtypeStruct((t, QK_WIDTH), BF16),
        jax.ShapeDtypeStruct((t, 256), F32),
        jax.ShapeDtypeStruct((t, 256), F32),
        jax.ShapeDtypeStruct((t, 512), F32),
        jax.ShapeDtypeStruct((t, 256), F32),
        jax.ShapeDtypeStruct((t, 512), F32),
    )
    out_specs = (
        pl.BlockSpec((MLA_HEADS, tm, QK_WIDTH), lambda i: (0, i, 0)),
        row(MLA_KV_RANK), row(MLA_ROPE), row(QK_WIDTH), row(256), row(256), row(512), row(256), row(512),
    )
    return pl.pallas_call(
        _mix_in_kernel, grid=grid, in_specs=in_specs, out_specs=out_specs, out_shape=out_shape,
        compiler_params=_cparams(("parallel",)), name="mix_in",
    )(h, cs, sn, *wts)


def _softmax_step(s, m_ref, l_ref, acc_ref, v):
    m_old = m_ref[...]
    m_new = jnp.maximum(m_old, jnp.max(s, axis=-1, keepdims=True))
    alpha = jnp.exp(m_old - m_new)
    p = jnp.exp(s - m_new)
    l_ref[...] = alpha * l_ref[...] + jnp.sum(p, axis=-1, keepdims=True)
    acc_ref[...] = alpha * acc_ref[...] + jnp.dot(p.astype(BF16), v, preferred_element_type=F32)
    m_ref[...] = m_new


def _mla_out(acc_ref, l_ref, wuv_ref, o_ref, rows):
    o = (acc_ref[...] / l_ref[...]).astype(BF16)
    for p in range(MLA_HEADS // 2):
        a = o[(2 * p) * rows:(2 * p + 1) * rows]
        b = o[(2 * p + 1) * rows:(2 * p + 2) * rows]
        y = (jnp.dot(a, wuv_ref[2 * p], preferred_element_type=F32)
             + jnp.dot(b, wuv_ref[2 * p + 1], preferred_element_type=F32))
        o_ref[:, p * 128:(p + 1) * 128] = y.astype(o_ref.dtype)


def _mla_prompt_kernel(q_ref, k_ref, wuv_ref, o_ref, m_ref, l_ref, acc_ref, *, tq, tk):
    i = pl.program_id(0)
    rows = MLA_HEADS * tq
    q = q_ref[...].reshape(rows, QK_WIDTH)
    m_ref[...] = jnp.full(m_ref.shape, NEG, F32)
    l_ref[...] = jnp.zeros(l_ref.shape, F32)
    acc_ref[...] = jnp.zeros(acc_ref.shape, F32)
    q0 = i * tq
    n_full = q0 // tk
    n_all = (q0 + tq - 1) // tk + 1

    def block(j, masked):
        k = k_ref[pl.ds(pl.multiple_of(j * tk, tk), tk), :]
        s = _bdot_nt(q, k)
        if masked:
            qpos = q0 + (lax.broadcasted_iota(jnp.int32, (rows, tk), 0) & (tq - 1))
            kpos = j * tk + lax.broadcasted_iota(jnp.int32, (rows, tk), 1)
            s = jnp.where(kpos <= qpos, s, NEG)
        _softmax_step(s, m_ref, l_ref, acc_ref, k[:, 0:MLA_KV_RANK])

    def full_body(j, c):
        block(j, False)
        return c

    def diag_body(j, c):
        block(j, True)
        return c

    lax.fori_loop(0, n_full, full_body, 0)
    lax.fori_loop(n_full, n_all, diag_body, 0)
    _mla_out(acc_ref, l_ref, wuv_ref, o_ref, tq)


def _mla_prompt(q, kcat, wuv_pair, tq, tk):
    t = kcat.shape[0]
    rows = MLA_HEADS * tq
    return pl.pallas_call(
        functools.partial(_mla_prompt_kernel, tq=tq, tk=tk),
        grid=(t // tq,),
        in_specs=[pl.BlockSpec((MLA_HEADS, tq, QK_WIDTH), lambda i: (0, i, 0)),
                  _full(kcat.shape), _full(wuv_pair.shape)],
        out_specs=pl.BlockSpec((tq, MLA_HEADS * MLA_V), lambda i: (i, 0)),
        out_shape=jax.ShapeDtypeStruct((t, MLA_HEADS * MLA_V), BF16),
        scratch_shapes=[pltpu.VMEM((rows, 1), F32), pltpu.VMEM((rows, 1), F32),
                        pltpu.VMEM((rows, MLA_KV_RANK), F32)],
        compiler_params=_cparams(("arbitrary",)), name="mla_prompt",
    )(q, kcat, wuv_pair)


def _mla_sample_kernel(pt_ref, q_ref, knew_ref, wuv_ref, *rest, n_pg, t_new):
    ckv_refs = rest[:n_pg]
    kr_refs = rest[n_pg:2 * n_pg]
    o_ref, m_ref, l_ref, acc_ref = rest[2 * n_pg:]
    g = pl.program_id(1)
    rows = MLA_HEADS * t_new
    q = q_ref[...].reshape(rows, QK_WIDTH)

    @pl.when(g == 0)
    def _():
        m_ref[...] = jnp.full(m_ref.shape, NEG, F32)
        l_ref[...] = jnp.zeros(l_ref.shape, F32)
        acc_ref[...] = jnp.zeros(acc_ref.shape, F32)

    ql = q[:, 0:MLA_KV_RANK]
    qr = q[:, MLA_KV_RANK:MLA_KV_RANK + MLA_ROPE]
    for i in range(n_pg):
        kv = ckv_refs[i][0].astype(BF16)
        kr = kr_refs[i][0].astype(BF16)
        s = _bdot_nt(ql, kv) + _bdot_nt(qr, kr)
        _softmax_step(s, m_ref, l_ref, acc_ref, kv)

    @pl.when(g == pl.num_programs(1) - 1)
    def _():
        kn = knew_ref[0]
        s = _bdot_nt(q, kn)
        qpos = lax.broadcasted_iota(jnp.int32, (rows, t_new), 0) & (t_new - 1)
        kpos = lax.broadcasted_iota(jnp.int32, (rows, t_new), 1)
        s = jnp.where(kpos <= qpos, s, NEG)
        _softmax_step(s, m_ref, l_ref, acc_ref, kn[:, 0:MLA_KV_RANK])
        _mla_out(acc_ref, l_ref, wuv_ref, o_ref.at[0], t_new)


def _mla_sample(page_table, q, knew, wuv_pair, cache_ckv, cache_kr, n_pg):
    db, n_pages = page_table.shape
    t_new = knew.shape[1]
    rows = MLA_HEADS * t_new

    def page_map(i):
        return lambda b, g, pt: (pt[b, g * n_pg + i], 0, 0)

    in_specs = [pl.BlockSpec((MLA_HEADS, 1, t_new, QK_WIDTH), lambda b, g, pt: (0, b, 0, 0)),
                pl.BlockSpec((1, t_new, QK_WIDTH), lambda b, g, pt: (b, 0, 0)),
                pl.BlockSpec(wuv_pair.shape, lambda b, g, pt: (0, 0, 0))]
    in_specs += [pl.BlockSpec((1, PAGE, MLA_KV_RANK), page_map(i)) for i in range(n_pg)]
    in_specs += [pl.BlockSpec((1, PAGE, MLA_ROPE), page_map(i)) for i in range(n_pg)]
    grid_spec = pltpu.PrefetchScalarGridSpec(
        num_scalar_prefetch=1, grid=(db, n_pages // n_pg), in_specs=in_specs,
        out_specs=pl.BlockSpec((1, t_new, MLA_HEADS * MLA_V), lambda b, g, pt: (b, 0, 0)),
        scratch_shapes=[pltpu.VMEM((rows, 1), F32), pltpu.VMEM((rows, 1), F32),
                        pltpu.VMEM((rows, MLA_KV_RANK), F32)])
    return pl.pallas_call(
        functools.partial(_mla_sample_kernel, n_pg=n_pg, t_new=t_new),
        grid_spec=grid_spec,
        out_shape=jax.ShapeDtypeStruct((db, t_new, MLA_HEADS * MLA_V), BF16),
        compiler_params=_cparams(("parallel", "arbitrary")), name="mla_sample",
    )(page_table, q, knew, wuv_pair, *([cache_ckv] * n_pg), *([cache_kr] * n_pg))


def _gla_kernel(gq_ref, gk_ref, gv_ref, gl_ref, go_ref, s0_ref, gn_ref, tri_ref, o_ref, st_ref, *, c, sub, nc):
    g_idx = pl.program_id(2)

    @pl.when(g_idx == 0)
    def _():
        st_ref[...] = s0_ref[...]

    tri = tri_ref[...]
    nsub = c // sub
    dk, dv = GLA_DK, GLA_DV
    lane = lax.broadcasted_iota(jnp.int32, (sub, 2 * dk), 1)
    col16 = lax.broadcasted_iota(jnp.int32, (sub, sub), 1)
    row16 = lax.broadcasted_iota(jnp.int32, (sub, sub), 0)
    eye = (lax.broadcasted_iota(jnp.int32, (dk, dk), 0) == lax.broadcasted_iota(jnp.int32, (dk, dk), 1))

    def chunk(ci, carry):
        r0 = pl.multiple_of(ci * c, c)
        q2 = gq_ref[0, pl.ds(r0, c), :]
        k2 = gk_ref[0, pl.ds(r0, c), :]
        g2 = gl_ref[0, pl.ds(r0, c), :]
        b2 = _split_dot_left(tri, g2)
        blast2 = b2[c - 1:c, :]
        qe2 = q2 * jnp.exp(b2)
        kd2 = k2 * jnp.exp(blast2 - b2)
        diag = [[None] * nsub for _ in range(2)]
        for i in range(nsub):
            qi = q2[i * sub:(i + 1) * sub]
            ki = k2[i * sub:(i + 1) * sub]
            bi = b2[i * sub:(i + 1) * sub]
            a0 = jnp.zeros((sub, sub), F32)
            a1 = jnp.zeros((sub, sub), F32)
            for s in range(sub):
                x = qi * ki[s:s + 1] * jnp.exp(jnp.minimum(bi - bi[s:s + 1], 0.0))
                c0 = jnp.sum(jnp.where(lane < dk, x, 0.0), axis=1, keepdims=True)
                c1 = jnp.sum(jnp.where(lane >= dk, x, 0.0), axis=1, keepdims=True)
                a0 = jnp.where(col16 == s, c0, a0)
                a1 = jnp.where(col16 == s, c1, a1)
            diag[0][i] = jnp.where(row16 >= col16, a0, 0.0)
            diag[1][i] = jnp.where(row16 >= col16, a1, 0.0)
        for hh in range(2):
            ls = slice(hh * dk, (hh + 1) * dk)
            v = gv_ref[0, pl.ds(r0, c), hh * dv:(hh + 1) * dv]
            vb = v.astype(BF16)
            st = st_ref[0, hh]
            inter = _bdot(qe2[:, ls], st)
            b = b2[:, ls]
            outs = []
            for i in range(nsub):
                oi = _bdot(diag[hh][i], vb[i * sub:(i + 1) * sub])
                if i > 0:
                    ref = b[i * sub - 1:i * sub]
                    qi = q2[i * sub:(i + 1) * sub, ls] * jnp.exp(b[i * sub:(i + 1) * sub] - ref)
                    kj = k2[0:i * sub, ls] * jnp.exp(ref - b[0:i * sub])
                    oi = oi + _bdot(_bdot_nt(qi, kj), vb[0:i * sub])
                outs.append(oi)
            o = inter + (jnp.concatenate(outs, axis=0) if nsub > 1 else outs[0])
            a_row = jnp.exp(blast2[:, ls])
            a_col = jnp.sum(jnp.where(eye, a_row, 0.0), axis=1, keepdims=True)
            st_ref[0, hh] = a_col * st + _bdot_tn(kd2[:, ls], vb)
            on = _rms(o, gn_ref[...])
            gate = go_ref[0, pl.ds(r0, c), hh * dv:(hh + 1) * dv]
            o_ref[0, pl.ds(r0, c), hh * dv:(hh + 1) * dv] = (on * gate * jax.nn.sigmoid(gate)).astype(o_ref.dtype)
        return carry

    lax.fori_loop(0, nc, chunk, 0)


def _split_dot_left(tri, x):
    hi = x.astype(BF16)
    r1 = x - hi.astype(F32)
    mid = r1.astype(BF16)
    lo = (r1 - mid.astype(F32)).astype(BF16)
    return (jnp.dot(tri, hi, preferred_element_type=F32)
            + jnp.dot(tri, mid, preferred_element_type=F32)
            + jnp.dot(tri, lo, preferred_element_type=F32))


def _gla(gq, gk, gv, glog, go, state0, gn, nc):
    bsz, t, _ = gq.shape
    c = math.gcd(t, GLA_CHUNK)
    sub = min(GLA_SUB, c)
    n_groups = t // (c * nc)
    tri = jnp.tril(jnp.ones((c, c), F32)).astype(BF16)
    qk_spec = pl.BlockSpec((1, c * nc, 2 * GLA_DK), lambda b, p, g: (b, g, p))
    v_spec = pl.BlockSpec((1, c * nc, 2 * GLA_DV), lambda b, p, g: (b, g, p))
    st_spec = pl.BlockSpec((1, 2, GLA_DK, GLA_DV), lambda b, p, g: (b, p, 0, 0))
    return pl.pallas_call(
        functools.partial(_gla_kernel, c=c, sub=sub, nc=nc),
        grid=(bsz, GLA_HEADS // 2, n_groups),
        in_specs=[qk_spec, qk_spec, v_spec, qk_spec, v_spec, st_spec,
                  pl.BlockSpec((1, GLA_DV), lambda b, p, g: (0, 0)),
                  pl.BlockSpec((c, c), lambda b, p, g: (0, 0))],
        out_specs=(v_spec, st_spec),
        out_shape=(jax.ShapeDtypeStruct((bsz, t, GLA_HEADS * GLA_DV), BF16),
                   jax.ShapeDtypeStruct(state0.shape, F32)),
        compiler_params=_cparams(("parallel", "parallel", "arbitrary")), name="gla",
    )(gq, gk, gv, glog, go, state0, gn, tri)


def _mix_out_mem_kernel(mla_ref, gla_ref, h_ref, woa_ref, wob_ref, gm_ref, wq_ref, mk_ref, mv_ref, wo_ref,
                        o_ref, *, groups, r):
    h1 = (h_ref[...] + jnp.dot(mla_ref[...], woa_ref[...], preferred_element_type=F32)
          + jnp.dot(gla_ref[...], wob_ref[...], preferred_element_type=F32))
    xn = _rms(h1, gm_ref[...]).astype(BF16)
    q = jnp.dot(xn, wq_ref[...], preferred_element_type=F32).astype(BF16)
    scale = MEM_HEAD_DIM ** -0.5
    outs = []
    for gi in range(groups):
        heads = []
        for hh in range(MEM_HEADS):
            ls = slice(hh * MEM_HEAD_DIM, (hh + 1) * MEM_HEAD_DIM)
            k = mk_ref[gi, :, ls].astype(BF16)
            v = mv_ref[gi, :, ls].astype(BF16)
            s = _bdot_nt(q[gi * r:(gi + 1) * r, ls], k) * scale
            s = s - jnp.max(s, axis=-1, keepdims=True)
            p = jnp.exp(s)
            p = p / jnp.sum(p, axis=-1, keepdims=True)
            heads.append(jnp.dot(p.astype(BF16), v, preferred_element_type=F32))
        outs.append(jnp.concatenate(heads, axis=1))
    o = jnp.concatenate(outs, axis=0) if groups > 1 else outs[0]
    o_ref[...] = h1 + jnp.dot(o.astype(BF16), wo_ref[...], preferred_element_type=F32)


def _mix_out_mem(mla_o, gla_o, h, woa, wob, gm, wq, mk, mv, wo, groups, r):
    t = h.shape[0]
    tm = groups * r
    row = lambda w: pl.BlockSpec((tm, w), lambda i: (i, 0))
    if mk.shape[0] == 1:
        kv_spec = pl.BlockSpec((1, MEM_TOKENS, MEM_WIDTH), lambda i: (0, 0, 0))
    else:
        kv_spec = pl.BlockSpec((groups, MEM_TOKENS, MEM_WIDTH), lambda i: (i, 0, 0))
    return pl.pallas_call(
        functools.partial(_mix_out_mem_kernel, groups=groups, r=r),
        grid=(t // tm,),
        in_specs=[row(512), row(512), row(D_MODEL), _full(woa.shape), _full(wob.shape), _full(gm.shape),
                  _full(wq.shape), kv_spec, kv_spec, _full(wo.shape)],
        out_specs=row(D_MODEL),
        out_shape=jax.ShapeDtypeStruct((t, D_MODEL), F32),
        compiler_params=_cparams(("parallel",)), name="mix_out_mem",
    )(mla_o, gla_o, h, woa, wob, gm, wq, mk, mv, wo)


def _router_topk(xn, wr_ref, br_ref):
    w = wr_ref[...]
    w_hi = w.astype(BF16)
    w_lo = (w - w_hi.astype(F32)).astype(BF16)
    x_hi = xn.astype(BF16)
    x_lo = (xn - x_hi.astype(F32)).astype(BF16)
    logits = (jnp.dot(x_hi, w_hi, preferred_element_type=F32) + jnp.dot(x_hi, w_lo, preferred_element_type=F32)
              + jnp.dot(x_lo, w_hi, preferred_element_type=F32)) + br_ref[...]
    tm, e = logits.shape
    lane = lax.broadcasted_iota(jnp.int32, (tm, e), 1)
    work = logits
    sel = jnp.zeros((tm, e), jnp.bool_)
    top = None
    for _ in range(TOP_K):
        m = jnp.max(work, axis=-1, keepdims=True)
        if top is None:
            top = m
        idx = jnp.min(jnp.where(work == m, lane, e), axis=-1, keepdims=True)
        pick = lane == idx
        sel = jnp.logical_or(sel, pick)
        work = jnp.where(pick, -jnp.inf, work)
    ex = jnp.where(sel, jnp.exp(logits - top), 0.0)
    return ex / jnp.sum(ex, axis=-1, keepdims=True)


def _moe_kernel(h_ref, g_ref, wr_ref, br_ref, wup_ref, bup_ref, wdn_ref, bdn_ref, gf_ref, y_ref,
                xb_ref, comb_ref, acc_ref):
    e = pl.program_id(1)

    @pl.when(e == 0)
    def _():
        xn = _rms(h_ref[...], g_ref[...])
        xb_ref[...] = xn.astype(BF16)
        comb_ref[...] = _router_topk(xn, wr_ref, br_ref)
        acc_ref[...] = jnp.zeros(acc_ref.shape, F32)

    hu = jnp.dot(xb_ref[...], wup_ref[0], preferred_element_type=F32) + bup_ref[0]
    gate = jnp.minimum(hu[:, 0:D_FF], SWIGLU_LIMIT)
    up = jnp.clip(hu[:, D_FF:2 * D_FF], -SWIGLU_LIMIT, SWIGLU_LIMIT)
    act = (up + 1.0) * gate * jax.nn.sigmoid(SWIGLU_ALPHA * gate)
    y = jnp.dot(act.astype(BF16), wdn_ref[0], preferred_element_type=F32) + bdn_ref[0]
    comb = comb_ref[...]
    lane = lax.broadcasted_iota(jnp.int32, comb.shape, 1)
    ce = jnp.sum(jnp.where(lane == e, comb, 0.0), axis=1, keepdims=True)
    acc_ref[...] += ce * y

    @pl.when(e == pl.num_programs(1) - 1)
    def _():
        y_ref[...] = _rms(h_ref[...] + acc_ref[...], gf_ref[...])


def _moe(h, g, wr, br, wup, bup, wdn, bdn, gf, tm):
    t = h.shape[0]
    row = pl.BlockSpec((tm, D_MODEL), lambda i, e: (i, 0))
    c2 = lambda shape: pl.BlockSpec(shape, lambda i, e: (0, 0))
    return pl.pallas_call(
        _moe_kernel,
        grid=(t // tm, N_EXPERTS),
        in_specs=[row, c2(g.shape), c2(wr.shape), c2(br.shape),
                  pl.BlockSpec((1, D_MODEL, 2 * D_FF), lambda i, e: (e, 0, 0)),
                  pl.BlockSpec((1, 1, 2 * D_FF), lambda i, e: (e, 0, 0)),
                  pl.BlockSpec((1, D_FF, D_MODEL), lambda i, e: (e, 0, 0)),
                  pl.BlockSpec((1, 1, D_MODEL), lambda i, e: (e, 0, 0)),
                  c2(gf.shape)],
        out_specs=row,
        out_shape=jax.ShapeDtypeStruct((t, D_MODEL), F32),
        scratch_shapes=[pltpu.VMEM((tm, D_MODEL), BF16), pltpu.VMEM((tm, N_EXPERTS), F32),
                        pltpu.VMEM((tm, D_MODEL), F32)],
        compiler_params=_cparams(("parallel", "arbitrary")), name="moe",
    )(h, g, wr, br, wup, bup, wdn, bdn, gf)


def _rope_tables(pos):
    half = MLA_ROPE // 2
    inv = ROPE_THETA ** (-jnp.arange(half, dtype=F32) / half)
    ang = pos.astype(F32)[:, None] * inv[None, :]
    cos, sin = jnp.cos(ang), jnp.sin(ang)
    pad = jnp.zeros((pos.shape[0], 128 - MLA_ROPE), F32)
    return jnp.concatenate([cos, cos, pad], axis=1), jnp.concatenate([sin, sin, pad], axis=1)


def _rot_cols(w):
    half = w.shape[-1] // 2
    return jnp.concatenate([-w[..., half:], w[..., :half]], axis=-1)


def _pad_cols(w, width):
    return jnp.pad(w, ((0, 0), (0, width - w.shape[1])))


def _prep_mix_weights(norm_g, w_in, q_a_norm_g, w_uq, kv_a_norm_g, w_uk, gla_gate_w2, gla_gate_b):
    o = 0
    parts = {}
    for name, size in (("cq", MLA_Q_RANK), ("ckv", MLA_KV_RANK), ("kr", MLA_ROPE), ("gq", 256), ("gk", 256),
                       ("gv", 512), ("gg", GLA_GATE_RANK), ("go", 512)):
        parts[name] = w_in[:, o:o + size]
        o += size
    w1 = jnp.concatenate([parts["cq"], parts["ckv"], parts["gq"] * (GLA_DK ** -0.5), parts["gk"], parts["gv"],
                          parts["go"]], axis=1).astype(BF16)
    w2 = jnp.concatenate([_pad_cols(parts["kr"], 128), _pad_cols(_rot_cols(parts["kr"]), 128),
                          _pad_cols(parts["gg"], 128)], axis=1).astype(BF16)
    wuq = w_uq.reshape(MLA_Q_RANK, MLA_HEADS, MLA_NOPE + MLA_ROPE)
    wuq_n = wuq[:, :, :MLA_NOPE].reshape(MLA_Q_RANK, MLA_HEADS * MLA_NOPE).astype(BF16)
    wr = wuq[:, :, MLA_NOPE:]
    widen = lambda w: jnp.pad(w, ((0, 0), (0, 0), (0, 128 - MLA_ROPE))).reshape(MLA_Q_RANK, MLA_HEADS * 128)
    wuq_r = widen(wr).astype(BF16)
    wuq_rr = widen(_rot_cols(wr)).astype(BF16)
    wukt = jnp.transpose(w_uk, (1, 2, 0)).astype(BF16)
    wg2 = jnp.pad(gla_gate_w2, ((0, 128 - GLA_GATE_RANK), (0, 0))).astype(BF16)
    return (norm_g[None], w1, w2, q_a_norm_g[None], wuq_n, wuq_r, wuq_rr, kv_a_norm_g[None], wukt, wg2,
            gla_gate_b[None])


def _prep_wuv(w_uv):
    w = jnp.transpose(w_uv, (1, 0, 2))
    z = jnp.zeros_like(w)
    even = jnp.concatenate([w, z], axis=-1)
    odd = jnp.concatenate([z, w], axis=-1)
    sel = (jnp.arange(MLA_HEADS) % 2 == 0)[:, None, None]
    return jnp.where(sel, even, odd).astype(BF16)


def kernel(x_prompt, x_sample, mem_prompt, cache_ckv, cache_krope, state_gla, cache_mem_k, cache_mem_v,
           page_table, norm_mix_g, w_in, q_a_norm_g, w_uq, kv_a_norm_g, w_uk, w_uv, gla_gate_w2, gla_gate_b,
           gla_norm_g, w_out, norm_mem_g, mem_in_norm_g, w_mem_q, w_mem_k, w_mem_v, w_mem_o, norm_moe_g,
           w_router, b_router, w_moe_up, b_moe_up, w_moe_down, b_moe_down, norm_final_g):
    depth = w_in.shape[0]
    assert depth == 1 and x_prompt.shape[0] == 1
    l = 0
    b, t, _ = x_prompt.shape
    db, ts, _ = x_sample.shape
    n_pages = page_table.shape[1]
    past = n_pages * PAGE

    mixw = _prep_mix_weights(norm_mix_g[l], w_in[l], q_a_norm_g[l], w_uq[l], kv_a_norm_g[l], w_uk[l],
                             gla_gate_w2[l], gla_gate_b[l])
    wuv_pair = _prep_wuv(w_uv[l])
    woa = w_out[l][:MLA_HEADS * MLA_V].astype(BF16)
    wob = w_out[l][MLA_HEADS * MLA_V:].astype(BF16)
    gn = gla_norm_g[l][None]
    gm = norm_mem_g[l][None]
    wmq = w_mem_q[l].astype(BF16)
    wmo = w_mem_o[l].astype(BF16)
    wup = w_moe_up[l].astype(BF16)
    wdn = w_moe_down[l].astype(BF16)
    bup = b_moe_up[l][:, None, :]
    bdn = b_moe_down[l][:, None, :]
    moe_args = (norm_moe_g[l][None], w_router[l], b_router[l][None], wup, bup, wdn, bdn, norm_final_g[None])

    hp = x_prompt.reshape(t, D_MODEL)
    cs_p, sn_p = _rope_tables(jnp.arange(t))
    tm_p = min(512, t)
    q_p, ckv_p, kr_p, kcat_p, gq, gk, gv, gl, go = _mix_in(hp, cs_p, sn_p, mixw, tm_p)
    tq = min(256, t)
    tk = min(512, t)
    mla_p = _mla_prompt(q_p, kcat_p, wuv_pair, tq, tk)
    zero_state = jnp.zeros((1, GLA_HEADS, GLA_DK, GLA_DV), F32)
    nc_p = max(1, min(4, t // GLA_CHUNK))
    gla_p, st_p = _gla(gq[None], gk[None], gv[None], gl[None], go[None], zero_state, gn, nc_p)
    mk_p, mv_p = _mem_kv(mem_prompt[0], mem_in_norm_g[l][None], w_mem_k[l].astype(BF16), w_mem_v[l].astype(BF16))
    hp = _mix_out_mem(mla_p, gla_p[0], hp, woa, wob, gm, wmq, mk_p[None], mv_p[None], wmo, 1, tm_p)
    y_p = _moe(hp, *moe_args, min(1024, t))

    n_s = db * ts
    hs = x_sample.reshape(n_s, D_MODEL)
    cs_s, sn_s = _rope_tables(past + jnp.arange(ts))
    cs_s = jnp.tile(cs_s, (db, 1))
    sn_s = jnp.tile(sn_s, (db, 1))
    tm_s = min(512, n_s)
    q_s, ckv_s, kr_s, kcat_s, gq, gk, gv, gl, go = _mix_in(hs, cs_s, sn_s, mixw, tm_s)
    n_pg = min(8, n_pages)
    mla_s = _mla_sample(page_table, q_s.reshape(MLA_HEADS, db, ts, QK_WIDTH), kcat_s.reshape(db, ts, QK_WIDTH),
                        wuv_pair, cache_ckv[l], cache_krope[l], n_pg)
    r3 = lambda a: a.reshape(db, ts, a.shape[-1])
    gla_s, st_s = _gla(r3(gq), r3(gk), r3(gv), r3(gl), r3(go), state_gla[l], gn, 1)
    groups = min(8, db)
    hs = _mix_out_mem(mla_s.reshape(n_s, -1), gla_s.reshape(n_s, -1), hs, woa, wob, gm, wmq,
                      cache_mem_k[l].reshape(db, MEM_TOKENS, MEM_WIDTH),
                      cache_mem_v[l].reshape(db, MEM_TOKENS, MEM_WIDTH), wmo, groups, ts)
    y_s = _moe(hs, *moe_args, min(1024, n_s))

    return (y_p.reshape(b, t, D_MODEL), y_s.reshape(db, ts, D_MODEL),
            ckv_p.reshape(1, b, t, MLA_KV_RANK), kr_p.reshape(1, b, t, MLA_ROPE),
            st_p.reshape(1, b, GLA_HEADS, GLA_DK, GLA_DV),
            mk_p.reshape(1, b, MEM_TOKENS, MEM_HEADS, MEM_HEAD_DIM),
            mv_p.reshape(1, b, MEM_TOKENS, MEM_HEADS, MEM_HEAD_DIM),
            ckv_s.reshape(1, db, ts, MLA_KV_RANK), kr_s.reshape(1, db, ts, MLA_ROPE),
            st_s.reshape(1, db, GLA_HEADS, GLA_DK, GLA_DV))
```

```python
import functools
import math

import jax
import jax.numpy as jnp
from jax import lax
from jax.experimental import pallas as pl
from jax.experimental.pallas import tpu as pltpu

F32 = jnp.float32
BF16 = jnp.bfloat16

D_MODEL = 1024
MLA_HEADS = 8
MLA_NOPE = 64
MLA_ROPE = 32
MLA_V = 64
MLA_Q_RANK = 384
MLA_KV_RANK = 256
MLA_SCALE = (MLA_NOPE + MLA_ROPE) ** -0.5
MLA_QSCALE = MLA_SCALE * math.log2(math.e)
ROPE_THETA = 10000.0
QK_WIDTH = 384
GLA_HEADS = 4
GLA_DK = 64
GLA_DV = 128
GLA_GATE_RANK = 16
GLA_GATE_NORMALIZER = 16.0
GLA_CHUNK = 64
GLA_SUB = 16
MEM_TOKENS = 256
MEM_HEADS = 4
MEM_HEAD_DIM = 128
MEM_WIDTH = MEM_HEADS * MEM_HEAD_DIM
N_EXPERTS = 32
TOP_K = 4
D_FF = 1024
SWIGLU_LIMIT = 7.0
SWIGLU_ALPHA = 1.702
EPS = 1e-6
PAGE = 128
NEG = -1e30

VMEM_LIMIT = 56 * 1024 * 1024


def _cparams(sem):
    return pltpu.CompilerParams(dimension_semantics=sem, vmem_limit_bytes=VMEM_LIMIT)


def _rms(x, g):
    var = jnp.mean(x * x, axis=-1, keepdims=True)
    return x * lax.rsqrt(var + EPS) * g


def _bdot(a, b):
    return jnp.dot(a.astype(BF16), b.astype(BF16), preferred_element_type=F32)


def _bdot_nt(a, b):
    return lax.dot_general(a.astype(BF16), b.astype(BF16), (((1,), (1,)), ((), ())),
                           preferred_element_type=F32)


def _bdot_tn(a, b):
    return lax.dot_general(a.astype(BF16), b.astype(BF16), (((0,), (0,)), ((), ())),
                           preferred_element_type=F32)


def _split_dot(a, b_exact):
    hi = a.astype(BF16)
    r1 = a - hi.astype(F32)
    mid = r1.astype(BF16)
    lo = (r1 - mid.astype(F32)).astype(BF16)
    return (jnp.dot(hi, b_exact, preferred_element_type=F32)
            + jnp.dot(mid, b_exact, preferred_element_type=F32)
            + jnp.dot(lo, b_exact, preferred_element_type=F32))


def _full(shape):
    n = len(shape)
    return pl.BlockSpec(shape, lambda *_: (0,) * n)


def _mem_kv_kernel(mem_ref, g_ref, wk_ref, wv_ref, k_ref, v_ref):
    mn = _rms(mem_ref[...], g_ref[...]).astype(BF16)
    k_ref[...] = jnp.dot(mn, wk_ref[...], preferred_element_type=F32)
    v_ref[...] = jnp.dot(mn, wv_ref[...], preferred_element_type=F32)


def _mem_kv(mem, g, wk, wv):
    m = mem.shape[0]
    return pl.pallas_call(
        _mem_kv_kernel,
        out_shape=(jax.ShapeDtypeStruct((m, MEM_WIDTH), F32), jax.ShapeDtypeStruct((m, MEM_WIDTH), F32)),
        name="mem_kv",
    )(mem, g, wk, wv)


_C_CQ, _C_CKV, _C_GQ, _C_GK, _C_GV, _C_GO, _C_END = 0, 384, 640, 896, 1152, 1664, 2176


def _log_sigmoid(x):
    return jnp.minimum(x, 0.0) - jnp.log1p(jnp.exp(-jnp.abs(x)))


def _mix_in_kernel(h_ref, cs_ref, sn_ref, g_ref, w1_ref, w2_ref, qag_ref, wuqn_ref, wuqr_ref, wuqrr_ref,
                   kvag_ref, wukt_ref, wg2_ref, bg_ref,
                   q_ref, ckv_ref, kr_ref, kcat_ref, gq_ref, gk_ref, gv_ref, glog_ref, go_ref):
    xb = _rms(h_ref[...], g_ref[...]).astype(BF16)
    z1 = jnp.dot(xb, w1_ref[...], preferred_element_type=F32)
    z2 = jnp.dot(xb, w2_ref[...], preferred_element_type=F32)
    cs = cs_ref[...]
    sn = sn_ref[...]
    ckv = _rms(z1[:, _C_CKV:_C_GQ], kvag_ref[...])
    ckv_ref[...] = ckv
    krp = z2[:, 0:128] * cs + z2[:, 128:256] * sn
    kr_ref[...] = krp[:, 0:MLA_ROPE]
    kcat_ref[:, 0:MLA_KV_RANK] = ckv.astype(BF16)
    kcat_ref[:, MLA_KV_RANK:QK_WIDTH] = krp.astype(BF16)
    cqn = _rms(z1[:, _C_CQ:_C_CKV], qag_ref[...]).astype(BF16)
    qn = jnp.dot(cqn, wuqn_ref[...], preferred_element_type=F32)
    csw = jnp.concatenate([cs] * MLA_HEADS, axis=1)
    snw = jnp.concatenate([sn] * MLA_HEADS, axis=1)
    qr = (jnp.dot(cqn, wuqr_ref[...], preferred_element_type=F32) * csw
          + jnp.dot(cqn, wuqrr_ref[...], preferred_element_type=F32) * snw)
    for h in range(MLA_HEADS):
        ql = _bdot(qn[:, h * MLA_NOPE:(h + 1) * MLA_NOPE], wukt_ref[h])
        q_ref[h, :, 0:MLA_KV_RANK] = (ql * MLA_QSCALE).astype(BF16)
        q_ref[h, :, MLA_KV_RANK:QK_WIDTH] = (qr[:, h * 128:(h + 1) * 128] * MLA_QSCALE).astype(BF16)
    gq_ref[...] = z1[:, _C_GQ:_C_GK]
    gk_ref[...] = z1[:, _C_GK:_C_GV]
    gv_ref[...] = z1[:, _C_GV:_C_GO]
    go_ref[...] = z1[:, _C_GO:_C_END]
    gate = _bdot(z2[:, 256:384], wg2_ref[...]) + bg_ref[...]
    glog_ref[...] = _log_sigmoid(gate) * (1.0 / GLA_GATE_NORMALIZER)


def _mix_in(h, cs, sn, wts, tm):
    t = h.shape[0]
    grid = (t // tm,)
    row = lambda w: pl.BlockSpec((tm, w), lambda i: (i, 0))
    in_specs = [row(D_MODEL), row(128), row(128)] + [_full(w.shape) for w in wts]
    out_shape = (
        jax.ShapeDtypeStruct((MLA_HEADS, t, QK_WIDTH), BF16),
        jax.ShapeDtypeStruct((t, MLA_KV_RANK), F32),
        jax.ShapeDtypeStruct((t, MLA_ROPE), F32),
        jax.ShapeDtypeStruct((t, QK_WIDTH), BF16),
        jax.ShapeDtypeStruct((t, 256), F32),
        jax.ShapeDtypeStruct((t, 256), F32),
        jax.ShapeDtypeStruct((t, 512), F32),
        jax.ShapeDtypeStruct((t, 256), F32),
        jax.ShapeDtypeStruct((t, 512), F32),
    )
    out_specs = (
        pl.BlockSpec((MLA_HEADS, tm, QK_WIDTH), lambda i: (0, i, 0)),
        row(MLA_KV_RANK), row(MLA_ROPE), row(QK_WIDTH), row(256), row(256), row(512), row(256), row(512),
    )
    return pl.pallas_call(
        _mix_in_kernel, grid=grid, in_specs=in_specs, out_specs=out_specs, out_shape=out_shape,
        compiler_params=_cparams(("parallel",)), name="mix_in",
    )(h, cs, sn, *wts)


def _softmax_step(s, m_ref, l_ref, acc_ref, v):
    m_old = m_ref[...]
    m_new = jnp.maximum(m_old, jnp.max(s, axis=-1, keepdims=True))
    alpha = jnp.exp2(m_old - m_new)
    p = jnp.exp2(s - m_new)
    l_ref[...] = alpha * l_ref[...] + jnp.sum(p, axis=-1, keepdims=True)
    acc_ref[...] = alpha * acc_ref[...] + jnp.dot(p.astype(BF16), v, preferred_element_type=F32)
    m_ref[...] = m_new


def _mla_out(acc_ref, l_ref, wuv_ref, o_ref, rows):
    o = acc_ref[...] / l_ref[...]
    for p in range(MLA_HEADS // 2):
        a = o[(2 * p) * rows:(2 * p + 1) * rows].astype(BF16)
        b = o[(2 * p + 1) * rows:(2 * p + 2) * rows].astype(BF16)
        y = (jnp.dot(a, wuv_ref[2 * p], preferred_element_type=F32)
             + jnp.dot(b, wuv_ref[2 * p + 1], preferred_element_type=F32))
        o_ref[:, p * 128:(p + 1) * 128] = y.astype(o_ref.dtype)


def _mla_prompt_kernel(q_ref, k_ref, wuv_ref, o_ref, s_ref, m_ref, l_ref, acc_ref, *, tq, tk):
    i = pl.program_id(0)
    rows = MLA_HEADS * tq
    q = q_ref[...].reshape(rows, QK_WIDTH)
    m_ref[...] = jnp.full(m_ref.shape, NEG, F32)
    l_ref[...] = jnp.zeros(l_ref.shape, F32)
    acc_ref[...] = jnp.zeros(acc_ref.shape, F32)
    q0 = i * tq
    n = q0 // tk + 1

    def kblk(j):
        return k_ref[pl.ds(pl.multiple_of(j * tk, tk), tk), :]

    def scores(j, slot):
        s_ref[slot] = _bdot_nt(q, kblk(j))

    def update(j, slot, masked):
        s = s_ref[slot]
        if masked:
            qpos = q0 + (lax.broadcasted_iota(jnp.int32, (rows, tk), 0) & (tq - 1))
            kpos = j * tk + lax.broadcasted_iota(jnp.int32, (rows, tk), 1)
            s = jnp.where(kpos <= qpos, s, NEG)
        _softmax_step(s, m_ref, l_ref, acc_ref, kblk(j)[:, 0:MLA_KV_RANK])

    scores(0, 0)
    n_pair = (n - 1) // 2

    def pair(jj, c):
        j = 2 * jj
        scores(j + 1, 1)
        update(j, 0, False)
        scores(j + 2, 0)
        update(j + 1, 1, False)
        return c

    lax.fori_loop(0, n_pair, pair, 0)
    left = (n - 1) - 2 * n_pair

    @pl.when(left == 0)
    def _():
        update(n - 1, 0, True)

    @pl.when(left == 1)
    def _():
        scores(n - 1, 1)
        update(n - 2, 0, False)
        update(n - 1, 1, True)

    _mla_out(acc_ref, l_ref, wuv_ref, o_ref, tq)


def _mla_prompt(q, kcat, wuv_pair, tq, tk):
    t = kcat.shape[0]
    assert tk % tq == 0 and t % tk == 0
    rows = MLA_HEADS * tq
    return pl.pallas_call(
        functools.partial(_mla_prompt_kernel, tq=tq, tk=tk),
        grid=(t // tq,),
        in_specs=[pl.BlockSpec((MLA_HEADS, tq, QK_WIDTH), lambda i: (0, i, 0)),
                  pl.BlockSpec(kcat.shape, lambda i: (0, 0), pipeline_mode=pl.Buffered(1)),
                  pl.BlockSpec(wuv_pair.shape, lambda i: (0, 0, 0), pipeline_mode=pl.Buffered(1))],
        out_specs=pl.BlockSpec((tq, MLA_HEADS * MLA_V), lambda i: (i, 0)),
        out_shape=jax.ShapeDtypeStruct((t, MLA_HEADS * MLA_V), BF16),
        scratch_shapes=[pltpu.VMEM((2, rows, tk), F32),
                        pltpu.VMEM((rows, 1), F32), pltpu.VMEM((rows, 1), F32),
                        pltpu.VMEM((rows, MLA_KV_RANK), F32)],
        compiler_params=_cparams(("arbitrary",)), name="mla_prompt",
    )(q, kcat, wuv_pair)


def _mla_sample_kernel(pt_ref, q_ref, knew_ref, wuv_ref, *rest, n_pg, t_new):
    ckv_refs = rest[:n_pg]
    krt_refs = rest[n_pg:2 * n_pg]
    o_ref, kv_ref, krt_ref, m_ref, l_ref, acc_ref = rest[2 * n_pg:]
    g = pl.program_id(1)
    rows = MLA_HEADS * t_new
    q = q_ref[...].astype(F32).reshape(rows, QK_WIDTH).astype(BF16)

    @pl.when(g == 0)
    def _():
        m_ref[...] = jnp.full(m_ref.shape, NEG, F32)
        l_ref[...] = jnp.zeros(l_ref.shape, F32)
        acc_ref[...] = jnp.zeros(acc_ref.shape, F32)

    for i in range(n_pg):
        kv_ref[i * PAGE:(i + 1) * PAGE, :] = ckv_refs[i][0].astype(BF16)
        krt_ref[:, i * PAGE:(i + 1) * PAGE] = krt_refs[i][0].astype(BF16)
    kv = kv_ref[...]
    s = (_bdot_nt(q[:, 0:MLA_KV_RANK], kv)
         + jnp.dot(q[:, MLA_KV_RANK:MLA_KV_RANK + MLA_ROPE], krt_ref[...], preferred_element_type=F32))
    _softmax_step(s, m_ref, l_ref, acc_ref, kv)

    @pl.when(g == pl.num_programs(1) - 1)
    def _():
        kn = knew_ref[0]
        s = _bdot_nt(q, kn)
        qpos = lax.broadcasted_iota(jnp.int32, (rows, t_new), 0) & (t_new - 1)
        kpos = lax.broadcasted_iota(jnp.int32, (rows, t_new), 1)
        s = jnp.where(kpos <= qpos, s, NEG)
        _softmax_step(s, m_ref, l_ref, acc_ref, kn[:, 0:MLA_KV_RANK])
        _mla_out(acc_ref, l_ref, wuv_ref, o_ref.at[0], t_new)


def _mla_sample(page_table, q, knew, wuv_pair, cache_ckv, cache_krt, n_pg):
    db, n_pages = page_table.shape
    t_new = knew.shape[1]
    rows = MLA_HEADS * t_new

    def page_map(i):
        return lambda b, g, pt: (pt[b, g * n_pg + i], 0, 0)

    in_specs = [pl.BlockSpec((MLA_HEADS, 1, t_new, QK_WIDTH), lambda b, g, pt: (0, b, 0, 0)),
                pl.BlockSpec((1, t_new, QK_WIDTH), lambda b, g, pt: (b, 0, 0)),
                pl.BlockSpec(wuv_pair.shape, lambda b, g, pt: (0, 0, 0))]
    in_specs += [pl.BlockSpec((1, PAGE, MLA_KV_RANK), page_map(i)) for i in range(n_pg)]
    in_specs += [pl.BlockSpec((1, MLA_ROPE, PAGE), page_map(i)) for i in range(n_pg)]
    grid_spec = pltpu.PrefetchScalarGridSpec(
        num_scalar_prefetch=1, grid=(db, n_pages // n_pg), in_specs=in_specs,
        out_specs=pl.BlockSpec((1, t_new, MLA_HEADS * MLA_V), lambda b, g, pt: (b, 0, 0)),
        scratch_shapes=[pltpu.VMEM((n_pg * PAGE, MLA_KV_RANK), BF16), pltpu.VMEM((MLA_ROPE, n_pg * PAGE), BF16),
                        pltpu.VMEM((rows, 1), F32), pltpu.VMEM((rows, 1), F32),
                        pltpu.VMEM((rows, MLA_KV_RANK), F32)])
    return pl.pallas_call(
        functools.partial(_mla_sample_kernel, n_pg=n_pg, t_new=t_new),
        grid_spec=grid_spec,
        out_shape=jax.ShapeDtypeStruct((db, t_new, MLA_HEADS * MLA_V), BF16),
        compiler_params=_cparams(("parallel", "arbitrary")), name="mla_sample",
    )(page_table, q, knew, wuv_pair, *([cache_ckv] * n_pg), *([cache_krt] * n_pg))


def _gla_kernel(gq_ref, gk_ref, gv_ref, gl_ref, go_ref, s0_ref, gn_ref, tri_ref, o_ref, st_ref, *, c, sub, nc):
    g_idx = pl.program_id(2)

    @pl.when(g_idx == 0)
    def _():
        st_ref[...] = s0_ref[...]

    tri = tri_ref[...]
    nsub = c // sub
    dk, dv = GLA_DK, GLA_DV
    lane = lax.broadcasted_iota(jnp.int32, (sub, 2 * dk), 1)
    col16 = lax.broadcasted_iota(jnp.int32, (sub, sub), 1)
    row16 = lax.broadcasted_iota(jnp.int32, (sub, sub), 0)
    eye = (lax.broadcasted_iota(jnp.int32, (dk, dk), 0) == lax.broadcasted_iota(jnp.int32, (dk, dk), 1))

    def chunk(ci, carry):
        r0 = pl.multiple_of(ci * c, c)
        q2 = gq_ref[0, pl.ds(r0, c), :]
        k2 = gk_ref[0, pl.ds(r0, c), :]
        g2 = gl_ref[0, pl.ds(r0, c), :]
        b2 = _split_dot_left(tri, g2)
        blast2 = b2[c - 1:c, :]
        qe2 = q2 * jnp.exp(b2)
        kd2 = k2 * jnp.exp(blast2 - b2)
        diag = [[None] * nsub for _ in range(2)]
        for i in range(nsub):
            qi = q2[i * sub:(i + 1) * sub]
            ki = k2[i * sub:(i + 1) * sub]
            bi = b2[i * sub:(i + 1) * sub]
            a0 = jnp.zeros((sub, sub), F32)
            a1 = jnp.zeros((sub, sub), F32)
            for s in range(sub):
                x = qi * ki[s:s + 1] * jnp.exp(jnp.minimum(bi - bi[s:s + 1], 0.0))
                c0 = jnp.sum(jnp.where(lane < dk, x, 0.0), axis=1, keepdims=True)
                c1 = jnp.sum(jnp.where(lane >= dk, x, 0.0), axis=1, keepdims=True)
                a0 = jnp.where(col16 == s, c0, a0)
                a1 = jnp.where(col16 == s, c1, a1)
            diag[0][i] = jnp.where(row16 >= col16, a0, 0.0)
            diag[1][i] = jnp.where(row16 >= col16, a1, 0.0)
        for hh in range(2):
            ls = slice(hh * dk, (hh + 1) * dk)
            v = gv_ref[0, pl.ds(r0, c), hh * dv:(hh + 1) * dv]
            vb = v.astype(BF16)
            st = st_ref[0, hh]
            inter = _bdot(qe2[:, ls], st)
            b = b2[:, ls]
            outs = []
            for i in range(nsub):
                oi = _bdot(diag[hh][i], vb[i * sub:(i + 1) * sub])
                if i > 0:
                    ref = b[i * sub - 1:i * sub]
                    qi = q2[i * sub:(i + 1) * sub, ls] * jnp.exp(b[i * sub:(i + 1) * sub] - ref)
                    kj = k2[0:i * sub, ls] * jnp.exp(ref - b[0:i * sub])
                    oi = oi + _bdot(_bdot_nt(qi, kj), vb[0:i * sub])
                outs.append(oi)
            o = inter + (jnp.concatenate(outs, axis=0) if nsub > 1 else outs[0])
            a_row = jnp.exp(blast2[:, ls])
            a_col = jnp.sum(jnp.where(eye, a_row, 0.0), axis=1, keepdims=True)
            st_ref[0, hh] = a_col * st + _bdot_tn(kd2[:, ls], vb)
            on = _rms(o, gn_ref[...])
            gate = go_ref[0, pl.ds(r0, c), hh * dv:(hh + 1) * dv]
            o_ref[0, pl.ds(r0, c), hh * dv:(hh + 1) * dv] = (on * gate * jax.nn.sigmoid(gate)).astype(o_ref.dtype)
        return carry

    lax.fori_loop(0, nc, chunk, 0)


def _split_dot_left(tri, x):
    hi = x.astype(BF16)
    r1 = x - hi.astype(F32)
    mid = r1.astype(BF16)
    lo = (r1 - mid.astype(F32)).astype(BF16)
    return (jnp.dot(tri, hi, preferred_element_type=F32)
            + jnp.dot(tri, mid, preferred_element_type=F32)
            + jnp.dot(tri, lo, preferred_element_type=F32))


def _gla(gq, gk, gv, glog, go, state0, gn, nc):
    bsz, t, _ = gq.shape
    c = math.gcd(t, GLA_CHUNK)
    sub = min(GLA_SUB, c)
    n_groups = t // (c * nc)
    tri = jnp.tril(jnp.ones((c, c), F32)).astype(BF16)
    qk_spec = pl.BlockSpec((1, c * nc, 2 * GLA_DK), lambda b, p, g: (b, g, p))
    v_spec = pl.BlockSpec((1, c * nc, 2 * GLA_DV), lambda b, p, g: (b, g, p))
    st_spec = pl.BlockSpec((1, 2, GLA_DK, GLA_DV), lambda b, p, g: (b, p, 0, 0))
    return pl.pallas_call(
        functools.partial(_gla_kernel, c=c, sub=sub, nc=nc),
        grid=(bsz, GLA_HEADS // 2, n_groups),
        in_specs=[qk_spec, qk_spec, v_spec, qk_spec, v_spec, st_spec,
                  pl.BlockSpec((1, GLA_DV), lambda b, p, g: (0, 0)),
                  pl.BlockSpec((c, c), lambda b, p, g: (0, 0))],
        out_specs=(v_spec, st_spec),
        out_shape=(jax.ShapeDtypeStruct((bsz, t, GLA_HEADS * GLA_DV), BF16),
                   jax.ShapeDtypeStruct(state0.shape, F32)),
        compiler_params=_cparams(("parallel", "parallel", "arbitrary")), name="gla",
    )(gq, gk, gv, glog, go, state0, gn, tri)


def _mix_out_mem_kernel(mla_ref, gla_ref, h_ref, woa_ref, wob_ref, gm_ref, wq_ref, mk_ref, mv_ref, wo_ref,
                        o_ref, *, groups, r, interleaved):
    h1 = (h_ref[...] + jnp.dot(mla_ref[...], woa_ref[...], preferred_element_type=F32)
          + jnp.dot(gla_ref[...], wob_ref[...], preferred_element_type=F32))
    xn = _rms(h1, gm_ref[...]).astype(BF16)
    q = jnp.dot(xn, wq_ref[...], preferred_element_type=F32).astype(BF16)
    scale = MEM_HEAD_DIM ** -0.5
    outs = []
    for gi in range(groups):
        heads = []
        for hh in range(MEM_HEADS):
            ls = slice(hh * MEM_HEAD_DIM, (hh + 1) * MEM_HEAD_DIM)
            if interleaved:
                rows_h = pl.ds(hh, MEM_TOKENS, stride=MEM_HEADS)
                k = mk_ref[gi, rows_h, :].astype(BF16)
                v = mv_ref[gi, rows_h, :].astype(BF16)
            else:
                k = mk_ref[gi, :, ls].astype(BF16)
                v = mv_ref[gi, :, ls].astype(BF16)
            s = _bdot_nt(q[gi * r:(gi + 1) * r, ls], k) * scale
            s = s - jnp.max(s, axis=-1, keepdims=True)
            p = jnp.exp(s)
            p = p / jnp.sum(p, axis=-1, keepdims=True)
            heads.append(jnp.dot(p.astype(BF16), v, preferred_element_type=F32))
        outs.append(jnp.concatenate(heads, axis=1))
    o = jnp.concatenate(outs, axis=0) if groups > 1 else outs[0]
    o_ref[...] = h1 + jnp.dot(o.astype(BF16), wo_ref[...], preferred_element_type=F32)


def _mix_out_mem(mla_o, gla_o, h, woa, wob, gm, wq, mk, mv, wo, groups, r):
    t = h.shape[0]
    tm = groups * r
    row = lambda w: pl.BlockSpec((tm, w), lambda i: (i, 0))
    interleaved = mk.shape[-1] == MEM_HEAD_DIM
    if interleaved:
        kv_spec = pl.BlockSpec((groups, MEM_TOKENS * MEM_HEADS, MEM_HEAD_DIM), lambda i: (i, 0, 0))
    else:
        kv_spec = pl.BlockSpec((1, MEM_TOKENS, MEM_WIDTH), lambda i: (0, 0, 0))
    return pl.pallas_call(
        functools.partial(_mix_out_mem_kernel, groups=groups, r=r, interleaved=interleaved),
        grid=(t // tm,),
        in_specs=[row(512), row(512), row(D_MODEL), _full(woa.shape), _full(wob.shape), _full(gm.shape),
                  _full(wq.shape), kv_spec, kv_spec, _full(wo.shape)],
        out_specs=row(D_MODEL),
        out_shape=jax.ShapeDtypeStruct((t, D_MODEL), F32),
        compiler_params=_cparams(("parallel",)), name="mix_out_mem",
    )(mla_o, gla_o, h, woa, wob, gm, wq, mk, mv, wo)


def _router_topk(xn, wr_ref, br_ref):
    w = wr_ref[...]
    w_hi = w.astype(BF16)
    w_lo = (w - w_hi.astype(F32)).astype(BF16)
    x_hi = xn.astype(BF16)
    x_lo = (xn - x_hi.astype(F32)).astype(BF16)
    logits = (jnp.dot(x_hi, w_hi, preferred_element_type=F32) + jnp.dot(x_hi, w_lo, preferred_element_type=F32)
              + jnp.dot(x_lo, w_hi, preferred_element_type=F32)) + br_ref[...]
    tm, e = logits.shape
    lane = lax.broadcasted_iota(jnp.int32, (tm, e), 1)
    work = logits
    sel = jnp.zeros((tm, e), jnp.bool_)
    top = None
    for _ in range(TOP_K):
        m = jnp.max(work, axis=-1, keepdims=True)
        if top is None:
            top = m
        idx = jnp.min(jnp.where(work == m, lane, e), axis=-1, keepdims=True)
        pick = lane == idx
        sel = jnp.logical_or(sel, pick)
        work = jnp.where(pick, -jnp.inf, work)
    ex = jnp.where(sel, jnp.exp(logits - top), 0.0)
    return ex / jnp.sum(ex, axis=-1, keepdims=True)


def _moe_kernel(h_ref, g_ref, wr_ref, br_ref, wup_ref, bup_ref, wdn_ref, bdn_ref, gf_ref, y_ref,
                xb_ref, comb_ref, acc_ref):
    e = pl.program_id(1)

    @pl.when(e == 0)
    def _():
        xn = _rms(h_ref[...], g_ref[...])
        xb_ref[...] = xn.astype(BF16)
        comb_ref[...] = _router_topk(xn, wr_ref, br_ref)
        acc_ref[...] = jnp.zeros(acc_ref.shape, F32)

    hu = jnp.dot(xb_ref[...], wup_ref[0], preferred_element_type=F32) + bup_ref[0]
    gate = jnp.minimum(hu[:, 0:D_FF], SWIGLU_LIMIT)
    up = jnp.clip(hu[:, D_FF:2 * D_FF], -SWIGLU_LIMIT, SWIGLU_LIMIT)
    act = (up + 1.0) * gate * jax.nn.sigmoid(SWIGLU_ALPHA * gate)
    y = jnp.dot(act.astype(BF16), wdn_ref[0], preferred_element_type=F32) + bdn_ref[0]
    comb = comb_ref[...]
    lane = lax.broadcasted_iota(jnp.int32, comb.shape, 1)
    ce = jnp.sum(jnp.where(lane == e, comb, 0.0), axis=1, keepdims=True)
    acc_ref[...] += ce * y

    @pl.when(e == pl.num_programs(1) - 1)
    def _():
        y_ref[...] = _rms(h_ref[...] + acc_ref[...], gf_ref[...])


def _moe(h, g, wr, br, wup, bup, wdn, bdn, gf, tm):
    t = h.shape[0]
    row = pl.BlockSpec((tm, D_MODEL), lambda i, e: (i, 0))
    c2 = lambda shape: pl.BlockSpec(shape, lambda i, e: (0, 0))
    return pl.pallas_call(
        _moe_kernel,
        grid=(t // tm, N_EXPERTS),
        in_specs=[row, c2(g.shape), c2(wr.shape), c2(br.shape),
                  pl.BlockSpec((1, D_MODEL, 2 * D_FF), lambda i, e: (e, 0, 0)),
                  pl.BlockSpec((1, 1, 2 * D_FF), lambda i, e: (e, 0, 0)),
                  pl.BlockSpec((1, D_FF, D_MODEL), lambda i, e: (e, 0, 0)),
                  pl.BlockSpec((1, 1, D_MODEL), lambda i, e: (e, 0, 0)),
                  c2(gf.shape)],
        out_specs=row,
        out_shape=jax.ShapeDtypeStruct((t, D_MODEL), F32),
        scratch_shapes=[pltpu.VMEM((tm, D_MODEL), BF16), pltpu.VMEM((tm, N_EXPERTS), F32),
                        pltpu.VMEM((tm, D_MODEL), F32)],
        compiler_params=_cparams(("parallel", "arbitrary")), name="moe",
    )(h, g, wr, br, wup, bup, wdn, bdn, gf)


def _rope_tables(pos):
    half = MLA_ROPE // 2
    inv = ROPE_THETA ** (-jnp.arange(half, dtype=F32) / half)
    ang = pos.astype(F32)[:, None] * inv[None, :]
    cos, sin = jnp.cos(ang), jnp.sin(ang)
    pad = jnp.zeros((pos.shape[0], 128 - MLA_ROPE), F32)
    return jnp.concatenate([cos, cos, pad], axis=1), jnp.concatenate([sin, sin, pad], axis=1)


def _rot_cols(w):
    half = w.shape[-1] // 2
    return jnp.concatenate([-w[..., half:], w[..., :half]], axis=-1)


def _pad_cols(w, width):
    return jnp.pad(w, ((0, 0), (0, width - w.shape[1])))


def _prep_mix_weights(norm_g, w_in, q_a_norm_g, w_uq, kv_a_norm_g, w_uk, gla_gate_w2, gla_gate_b):
    o = 0
    parts = {}
    for name, size in (("cq", MLA_Q_RANK), ("ckv", MLA_KV_RANK), ("kr", MLA_ROPE), ("gq", 256), ("gk", 256),
                       ("gv", 512), ("gg", GLA_GATE_RANK), ("go", 512)):
        parts[name] = w_in[:, o:o + size]
        o += size
    w1 = jnp.concatenate([parts["cq"], parts["ckv"], parts["gq"] * (GLA_DK ** -0.5), parts["gk"], parts["gv"],
                          parts["go"]], axis=1).astype(BF16)
    w2 = jnp.concatenate([_pad_cols(parts["kr"], 128), _pad_cols(_rot_cols(parts["kr"]), 128),
                          _pad_cols(parts["gg"], 128)], axis=1).astype(BF16)
    wuq = w_uq.reshape(MLA_Q_RANK, MLA_HEADS, MLA_NOPE + MLA_ROPE)
    wuq_n = wuq[:, :, :MLA_NOPE].reshape(MLA_Q_RANK, MLA_HEADS * MLA_NOPE).astype(BF16)
    wr = wuq[:, :, MLA_NOPE:]
    widen = lambda w: jnp.pad(w, ((0, 0), (0, 0), (0, 128 - MLA_ROPE))).reshape(MLA_Q_RANK, MLA_HEADS * 128)
    wuq_r = widen(wr).astype(BF16)
    wuq_rr = widen(_rot_cols(wr)).astype(BF16)
    wukt = jnp.transpose(w_uk, (1, 2, 0)).astype(BF16)
    wg2 = jnp.pad(gla_gate_w2, ((0, 128 - GLA_GATE_RANK), (0, 0))).astype(BF16)
    return (norm_g[None], w1, w2, q_a_norm_g[None], wuq_n, wuq_r, wuq_rr, kv_a_norm_g[None], wukt, wg2,
            gla_gate_b[None])


def _prep_wuv(w_uv):
    w = jnp.transpose(w_uv, (1, 0, 2))
    z = jnp.zeros_like(w)
    even = jnp.concatenate([w, z], axis=-1)
    odd = jnp.concatenate([z, w], axis=-1)
    sel = (jnp.arange(MLA_HEADS) % 2 == 0)[:, None, None]
    return jnp.where(sel, even, odd).astype(BF16)


def kernel(x_prompt, x_sample, mem_prompt, cache_ckv, cache_krope, state_gla, cache_mem_k, cache_mem_v,
           page_table, norm_mix_g, w_in, q_a_norm_g, w_uq, kv_a_norm_g, w_uk, w_uv, gla_gate_w2, gla_gate_b,
           gla_norm_g, w_out, norm_mem_g, mem_in_norm_g, w_mem_q, w_mem_k, w_mem_v, w_mem_o, norm_moe_g,
           w_router, b_router, w_moe_up, b_moe_up, w_moe_down, b_moe_down, norm_final_g):
    depth = w_in.shape[0]
    assert depth == 1 and x_prompt.shape[0] == 1
    l = 0
    b, t, _ = x_prompt.shape
    db, ts, _ = x_sample.shape
    n_pages = page_table.shape[1]
    past = n_pages * PAGE

    mixw = _prep_mix_weights(norm_mix_g[l], w_in[l], q_a_norm_g[l], w_uq[l], kv_a_norm_g[l], w_uk[l],
                             gla_gate_w2[l], gla_gate_b[l])
    wuv_pair = _prep_wuv(w_uv[l])
    woa = w_out[l][:MLA_HEADS * MLA_V].astype(BF16)
    wob = w_out[l][MLA_HEADS * MLA_V:].astype(BF16)
    gn = gla_norm_g[l][None]
    gm = norm_mem_g[l][None]
    wmq = w_mem_q[l].astype(BF16)
    wmo = w_mem_o[l].astype(BF16)
    wup = w_moe_up[l].astype(BF16)
    wdn = w_moe_down[l].astype(BF16)
    bup = b_moe_up[l][:, None, :]
    bdn = b_moe_down[l][:, None, :]
    moe_args = (norm_moe_g[l][None], w_router[l], b_router[l][None], wup, bup, wdn, bdn, norm_final_g[None])

    hp = x_prompt.reshape(t, D_MODEL)
    cs_p, sn_p = _rope_tables(jnp.arange(t))
    tm_p = min(512, t)
    q_p, ckv_p, kr_p, kcat_p, gq, gk, gv, gl, go = _mix_in(hp, cs_p, sn_p, mixw, tm_p)
    tq = min(256, t)
    tk = min(512, t)
    mla_p = _mla_prompt(q_p, kcat_p, wuv_pair, tq, tk)
    zero_state = jnp.zeros((1, GLA_HEADS, GLA_DK, GLA_DV), F32)
    nc_p = max(1, min(4, t // GLA_CHUNK))
    gla_p, st_p = _gla(gq[None], gk[None], gv[None], gl[None], go[None], zero_state, gn, nc_p)
    mk_p, mv_p = _mem_kv(mem_prompt[0], mem_in_norm_g[l][None], w_mem_k[l].astype(BF16), w_mem_v[l].astype(BF16))
    hp = _mix_out_mem(mla_p, gla_p[0], hp, woa, wob, gm, wmq, mk_p[None], mv_p[None], wmo, 1, tm_p)
    y_p = _moe(hp, *moe_args, min(1024, t))

    n_s = db * ts
    hs = x_sample.reshape(n_s, D_MODEL)
    cs_s, sn_s = _rope_tables(past + jnp.arange(ts))
    cs_s = jnp.tile(cs_s, (db, 1))
    sn_s = jnp.tile(sn_s, (db, 1))
    tm_s = min(512, n_s)
    q_s, ckv_s, kr_s, kcat_s, gq, gk, gv, gl, go = _mix_in(hs, cs_s, sn_s, mixw, tm_s)
    n_pg = min(16, n_pages)
    mla_s = _mla_sample(page_table, q_s.reshape(MLA_HEADS, db, ts, QK_WIDTH), kcat_s.reshape(db, ts, QK_WIDTH),
                        wuv_pair, cache_ckv[l], jnp.swapaxes(cache_krope[l], 1, 2), n_pg)
    r3 = lambda a: a.reshape(db, ts, a.shape[-1])
    gla_s, st_s = _gla(r3(gq), r3(gk), r3(gv), r3(gl), r3(go), state_gla[l], gn, 1)
    groups = min(8, db)
    hs = _mix_out_mem(mla_s.reshape(n_s, -1), gla_s.reshape(n_s, -1), hs, woa, wob, gm, wmq,
                      cache_mem_k[l].reshape(db, MEM_TOKENS * MEM_HEADS, MEM_HEAD_DIM),
                      cache_mem_v[l].reshape(db, MEM_TOKENS * MEM_HEADS, MEM_HEAD_DIM), wmo, groups, ts)
    y_s = _moe(hs, *moe_args, min(1024, n_s))

    return (y_p.reshape(b, t, D_MODEL), y_s.reshape(db, ts, D_MODEL),
            ckv_p.reshape(1, b, t, MLA_KV_RANK), kr_p.reshape(1, b, t, MLA_ROPE),
            st_p.reshape(1, b, GLA_HEADS, GLA_DK, GLA_DV),
            mk_p.reshape(1, b, MEM_TOKENS, MEM_HEADS, MEM_HEAD_DIM),
            mv_p.reshape(1, b, MEM_TOKENS, MEM_HEADS, MEM_HEAD_DIM),
            ckv_s.reshape(1, db, ts, MLA_KV_RANK), kr_s.reshape(1, db, ts, MLA_ROPE),
            st_s.reshape(1, db, GLA_HEADS, GLA_DK, GLA_DV))
```

```python
import functools
import math

import jax
import jax.numpy as jnp
from jax import lax
from jax.experimental import pallas as pl
from jax.experimental.pallas import tpu as pltpu

F32 = jnp.float32
BF16 = jnp.bfloat16

D_MODEL = 1024
MLA_HEADS = 8
MLA_NOPE = 64
MLA_ROPE = 32
MLA_V = 64
MLA_Q_RANK = 384
MLA_KV_RANK = 256
MLA_SCALE = (MLA_NOPE + MLA_ROPE) ** -0.5
MLA_QSCALE = MLA_SCALE * math.log2(math.e)
ROPE_THETA = 10000.0
QK_WIDTH = 384
GLA_HEADS = 4
GLA_DK = 64
GLA_DV = 128
GLA_GATE_RANK = 16
GLA_GATE_NORMALIZER = 16.0
GLA_CHUNK = 64
GLA_SUB = 16
MEM_TOKENS = 256
MEM_HEADS = 4
MEM_HEAD_DIM = 128
MEM_WIDTH = MEM_HEADS * MEM_HEAD_DIM
N_EXPERTS = 32
TOP_K = 4
D_FF = 1024
SWIGLU_LIMIT = 7.0
SWIGLU_ALPHA = 1.702
EPS = 1e-6
PAGE = 128
NEG = -1e30

VMEM_LIMIT = 56 * 1024 * 1024


def _cparams(sem):
    return pltpu.CompilerParams(dimension_semantics=sem, vmem_limit_bytes=VMEM_LIMIT)


def _rms(x, g):
    var = jnp.mean(x * x, axis=-1, keepdims=True)
    return x * lax.rsqrt(var + EPS) * g


def _bdot(a, b):
    return jnp.dot(a.astype(BF16), b.astype(BF16), preferred_element_type=F32)


def _bdot_nt(a, b):
    return lax.dot_general(a.astype(BF16), b.astype(BF16), (((1,), (1,)), ((), ())),
                           preferred_element_type=F32)


def _bdot_tn(a, b):
    return lax.dot_general(a.astype(BF16), b.astype(BF16), (((0,), (0,)), ((), ())),
                           preferred_element_type=F32)


def _split_dot(a, b_exact):
    hi = a.astype(BF16)
    r1 = a - hi.astype(F32)
    mid = r1.astype(BF16)
    lo = (r1 - mid.astype(F32)).astype(BF16)
    return (jnp.dot(hi, b_exact, preferred_element_type=F32)
            + jnp.dot(mid, b_exact, preferred_element_type=F32)
            + jnp.dot(lo, b_exact, preferred_element_type=F32))


def _full(shape):
    n = len(shape)
    return pl.BlockSpec(shape, lambda *_: (0,) * n)


def _mem_kv_kernel(mem_ref, g_ref, wk_ref, wv_ref, k_ref, v_ref):
    mn = _rms(mem_ref[...], g_ref[...]).astype(BF16)
    k_ref[...] = jnp.dot(mn, wk_ref[...], preferred_element_type=F32)
    v_ref[...] = jnp.dot(mn, wv_ref[...], preferred_element_type=F32)


def _mem_kv(mem, g, wk, wv):
    m = mem.shape[0]
    return pl.pallas_call(
        _mem_kv_kernel,
        out_shape=(jax.ShapeDtypeStruct((m, MEM_WIDTH), F32), jax.ShapeDtypeStruct((m, MEM_WIDTH), F32)),
        name="mem_kv",
    )(mem, g, wk, wv)


_C_CQ, _C_CKV, _C_GQ, _C_GK, _C_GV, _C_GO, _C_END = 0, 384, 640, 896, 1152, 1664, 2176


def _log_sigmoid(x):
    return jnp.minimum(x, 0.0) - jnp.log1p(jnp.exp(-jnp.abs(x)))


def _mix_in_kernel(h_ref, cs_ref, sn_ref, g_ref, w1_ref, w2_ref, qag_ref, wuqn_ref, wuqr_ref, wuqrr_ref,
                   kvag_ref, wukt_ref, wg2_ref, bg_ref,
                   q_ref, ckv_ref, kr_ref, kcat_ref, gq_ref, gk_ref, gv_ref, glog_ref, go_ref):
    xb = _rms(h_ref[...], g_ref[...]).astype(BF16)
    z1 = jnp.dot(xb, w1_ref[...], preferred_element_type=F32)
    z2 = jnp.dot(xb, w2_ref[...], preferred_element_type=F32)
    cs = cs_ref[...]
    sn = sn_ref[...]
    ckv = _rms(z1[:, _C_CKV:_C_GQ], kvag_ref[...])
    ckv_ref[...] = ckv
    krp = z2[:, 0:128] * cs + z2[:, 128:256] * sn
    kr_ref[...] = krp[:, 0:MLA_ROPE]
    kcat_ref[:, 0:MLA_KV_RANK] = ckv.astype(BF16)
    kcat_ref[:, MLA_KV_RANK:QK_WIDTH] = krp.astype(BF16)
    cqn = _rms(z1[:, _C_CQ:_C_CKV], qag_ref[...]).astype(BF16)
    qn = jnp.dot(cqn, wuqn_ref[...], preferred_element_type=F32)
    csw = jnp.concatenate([cs] * MLA_HEADS, axis=1)
    snw = jnp.concatenate([sn] * MLA_HEADS, axis=1)
    qr = (jnp.dot(cqn, wuqr_ref[...], preferred_element_type=F32) * csw
          + jnp.dot(cqn, wuqrr_ref[...], preferred_element_type=F32) * snw)
    for h in range(MLA_HEADS):
        ql = _bdot(qn[:, h * MLA_NOPE:(h + 1) * MLA_NOPE], wukt_ref[h])
        q_ref[h, :, 0:MLA_KV_RANK] = (ql * MLA_QSCALE).astype(BF16)
        q_ref[h, :, MLA_KV_RANK:QK_WIDTH] = (qr[:, h * 128:(h + 1) * 128] * MLA_QSCALE).astype(BF16)
    gq_ref[...] = z1[:, _C_GQ:_C_GK]
    gk_ref[...] = z1[:, _C_GK:_C_GV]
    gv_ref[...] = z1[:, _C_GV:_C_GO]
    go_ref[...] = z1[:, _C_GO:_C_END]
    gate = _bdot(z2[:, 256:384], wg2_ref[...]) + bg_ref[...]
    glog_ref[...] = _log_sigmoid(gate) * (1.0 / GLA_GATE_NORMALIZER)


def _mix_in(h, cs, sn, wts, tm):
    t = h.shape[0]
    grid = (t // tm,)
    row = lambda w: pl.BlockSpec((tm, w), lambda i: (i, 0))
    in_specs = [row(D_MODEL), row(128), row(128)] + [_full(w.shape) for w in wts]
    out_shape = (
        jax.ShapeDtypeStruct((MLA_HEADS, t, QK_WIDTH), BF16),
        jax.ShapeDtypeStruct((t, MLA_KV_RANK), F32),
        jax.ShapeDtypeStruct((t, MLA_ROPE), F32),
        jax.ShapeDtypeStruct((t, QK_WIDTH), BF16),
        jax.ShapeDtypeStruct((t, 256), F32),
        jax.ShapeDtypeStruct((t, 256), F32),
        jax.ShapeDtypeStruct((t, 512), F32),
        jax.ShapeDtypeStruct((t, 256), F32),
        jax.ShapeDtypeStruct((t, 512), F32),
    )
    out_specs = (
        pl.BlockSpec((MLA_HEADS, tm, QK_WIDTH), lambda i: (0, i, 0)),
        row(MLA_KV_RANK), row(MLA_ROPE), row(QK_WIDTH), row(256), row(256), row(512), row(256), row(512),
    )
    return pl.pallas_call(
        _mix_in_kernel, grid=grid, in_specs=in_specs, out_specs=out_specs, out_shape=out_shape,
        compiler_params=_cparams(("parallel",)), name="mix_in",
    )(h, cs, sn, *wts)


def _softmax_step(s, m_ref, l_ref, acc_ref, v):
    m_old = m_ref[...]
    m_new = jnp.maximum(m_old, jnp.max(s, axis=-1, keepdims=True))
    alpha = jnp.exp2(m_old - m_new)
    p = jnp.exp2(s - m_new)
    l_ref[...] = alpha * l_ref[...] + jnp.sum(p, axis=-1, keepdims=True)
    acc_ref[...] = alpha * acc_ref[...] + jnp.dot(p.astype(BF16), v, preferred_element_type=F32)
    m_ref[...] = m_new


def _mla_out(acc_ref, l_ref, wuv_ref, o_ref, rows):
    o = acc_ref[...] / l_ref[...]
    for p in range(MLA_HEADS // 2):
        a = o[(2 * p) * rows:(2 * p + 1) * rows].astype(BF16)
        b = o[(2 * p + 1) * rows:(2 * p + 2) * rows].astype(BF16)
        y = (jnp.dot(a, wuv_ref[2 * p], preferred_element_type=F32)
             + jnp.dot(b, wuv_ref[2 * p + 1], preferred_element_type=F32))
        o_ref[:, p * 128:(p + 1) * 128] = y.astype(o_ref.dtype)


def _mla_prompt_kernel(q_ref, k_ref, wuv_ref, o_ref, s_ref, m_ref, l_ref, acc_ref, *, tq, tk):
    i = pl.program_id(0)
    rows = MLA_HEADS * tq
    q = q_ref[...].reshape(rows, QK_WIDTH)
    m_ref[...] = jnp.full(m_ref.shape, NEG, F32)
    l_ref[...] = jnp.zeros(l_ref.shape, F32)
    acc_ref[...] = jnp.zeros(acc_ref.shape, F32)
    q0 = i * tq
    n = q0 // tk + 1

    def kblk(j):
        return k_ref[pl.ds(pl.multiple_of(j * tk, tk), tk), :]

    def scores(j, slot):
        s_ref[slot] = _bdot_nt(q, kblk(j))

    def update(j, slot, masked):
        s = s_ref[slot]
        if masked:
            qpos = q0 + (lax.broadcasted_iota(jnp.int32, (rows, tk), 0) & (tq - 1))
            kpos = j * tk + lax.broadcasted_iota(jnp.int32, (rows, tk), 1)
            s = jnp.where(kpos <= qpos, s, NEG)
        _softmax_step(s, m_ref, l_ref, acc_ref, kblk(j)[:, 0:MLA_KV_RANK])

    scores(0, 0)
    n_pair = (n - 1) // 2

    def pair(jj, c):
        j = 2 * jj
        scores(j + 1, 1)
        update(j, 0, False)
        scores(j + 2, 0)
        update(j + 1, 1, False)
        return c

    lax.fori_loop(0, n_pair, pair, 0)
    left = (n - 1) - 2 * n_pair

    @pl.when(left == 0)
    def _():
        update(n - 1, 0, True)

    @pl.when(left == 1)
    def _():
        scores(n - 1, 1)
        update(n - 2, 0, False)
        update(n - 1, 1, True)

    _mla_out(acc_ref, l_ref, wuv_ref, o_ref, tq)


def _mla_prompt(q, kcat, wuv_pair, tq, tk):
    t = kcat.shape[0]
    assert tk % tq == 0 and t % tk == 0
    rows = MLA_HEADS * tq
    return pl.pallas_call(
        functools.partial(_mla_prompt_kernel, tq=tq, tk=tk),
        grid=(t // tq,),
        in_specs=[pl.BlockSpec((MLA_HEADS, tq, QK_WIDTH), lambda i: (0, i, 0)),
                  pl.BlockSpec(kcat.shape, lambda i: (0, 0), pipeline_mode=pl.Buffered(1)),
                  pl.BlockSpec(wuv_pair.shape, lambda i: (0, 0, 0), pipeline_mode=pl.Buffered(1))],
        out_specs=pl.BlockSpec((tq, MLA_HEADS * MLA_V), lambda i: (i, 0)),
        out_shape=jax.ShapeDtypeStruct((t, MLA_HEADS * MLA_V), BF16),
        scratch_shapes=[pltpu.VMEM((2, rows, tk), F32),
                        pltpu.VMEM((rows, 1), F32), pltpu.VMEM((rows, 1), F32),
                        pltpu.VMEM((rows, MLA_KV_RANK), F32)],
        compiler_params=_cparams(("arbitrary",)), name="mla_prompt",
    )(q, kcat, wuv_pair)


def _mla_sample_kernel(pt_ref, q_ref, knew_ref, wuv_ref, *rest, n_pg, t_new):
    ckv_refs = rest[:n_pg]
    krt_refs = rest[n_pg:2 * n_pg]
    o_ref, kv_ref, krt_ref, m_ref, l_ref, acc_ref = rest[2 * n_pg:]
    g = pl.program_id(1)
    rows = MLA_HEADS * t_new
    q = q_ref[...].astype(F32).reshape(rows, QK_WIDTH).astype(BF16)

    @pl.when(g == 0)
    def _():
        m_ref[...] = jnp.full(m_ref.shape, NEG, F32)
        l_ref[...] = jnp.zeros(l_ref.shape, F32)
        acc_ref[...] = jnp.zeros(acc_ref.shape, F32)

    for i in range(n_pg):
        kv_ref[i * PAGE:(i + 1) * PAGE, :] = ckv_refs[i][0].astype(BF16)
        krt_ref[:, i * PAGE:(i + 1) * PAGE] = krt_refs[i][0].astype(BF16)
    kv = kv_ref[...]
    s = (_bdot_nt(q[:, 0:MLA_KV_RANK], kv)
         + jnp.dot(q[:, MLA_KV_RANK:MLA_KV_RANK + MLA_ROPE], krt_ref[...], preferred_element_type=F32))
    _softmax_step(s, m_ref, l_ref, acc_ref, kv)

    @pl.when(g == pl.num_programs(1) - 1)
    def _():
        kn = knew_ref[0]
        s = _bdot_nt(q, kn)
        qpos = lax.broadcasted_iota(jnp.int32, (rows, t_new), 0) & (t_new - 1)
        kpos = lax.broadcasted_iota(jnp.int32, (rows, t_new), 1)
        s = jnp.where(kpos <= qpos, s, NEG)
        _softmax_step(s, m_ref, l_ref, acc_ref, kn[:, 0:MLA_KV_RANK])
        _mla_out(acc_ref, l_ref, wuv_ref, o_ref.at[0], t_new)


def _mla_sample(page_table, q, knew, wuv_pair, cache_ckv, cache_krt, n_pg):
    db, n_pages = page_table.shape
    t_new = knew.shape[1]
    rows = MLA_HEADS * t_new

    def page_map(i):
        return lambda b, g, pt: (pt[b, g * n_pg + i], 0, 0)

    in_specs = [pl.BlockSpec((MLA_HEADS, 1, t_new, QK_WIDTH), lambda b, g, pt: (0, b, 0, 0)),
                pl.BlockSpec((1, t_new, QK_WIDTH), lambda b, g, pt: (b, 0, 0)),
                pl.BlockSpec(wuv_pair.shape, lambda b, g, pt: (0, 0, 0))]
    in_specs += [pl.BlockSpec((1, PAGE, MLA_KV_RANK), page_map(i)) for i in range(n_pg)]
    in_specs += [pl.BlockSpec((1, MLA_ROPE, PAGE), page_map(i)) for i in range(n_pg)]
    grid_spec = pltpu.PrefetchScalarGridSpec(
        num_scalar_prefetch=1, grid=(db, n_pages // n_pg), in_specs=in_specs,
        out_specs=pl.BlockSpec((1, t_new, MLA_HEADS * MLA_V), lambda b, g, pt: (b, 0, 0)),
        scratch_shapes=[pltpu.VMEM((n_pg * PAGE, MLA_KV_RANK), BF16), pltpu.VMEM((MLA_ROPE, n_pg * PAGE), BF16),
                        pltpu.VMEM((rows, 1), F32), pltpu.VMEM((rows, 1), F32),
                        pltpu.VMEM((rows, MLA_KV_RANK), F32)])
    return pl.pallas_call(
        functools.partial(_mla_sample_kernel, n_pg=n_pg, t_new=t_new),
        grid_spec=grid_spec,
        out_shape=jax.ShapeDtypeStruct((db, t_new, MLA_HEADS * MLA_V), BF16),
        compiler_params=_cparams(("parallel", "arbitrary")), name="mla_sample",
    )(page_table, q, knew, wuv_pair, *([cache_ckv] * n_pg), *([cache_krt] * n_pg))


def _gla_kernel(gq_ref, gk_ref, gv_ref, gl_ref, go_ref, s0_ref, gn_ref, tri_ref, o_ref, st_ref, *, c, sub, nc):
    g_idx = pl.program_id(2)

    @pl.when(g_idx == 0)
    def _():
        st_ref[...] = s0_ref[...]

    tri = tri_ref[...]
    nsub = c // sub
    dk, dv = GLA_DK, GLA_DV
    lane = lax.broadcasted_iota(jnp.int32, (sub, 2 * dk), 1)
    col16 = lax.broadcasted_iota(jnp.int32, (sub, sub), 1)
    row16 = lax.broadcasted_iota(jnp.int32, (sub, sub), 0)
    eye = (lax.broadcasted_iota(jnp.int32, (dk, dk), 0) == lax.broadcasted_iota(jnp.int32, (dk, dk), 1))

    def chunk(ci, carry):
        r0 = pl.multiple_of(ci * c, c)
        q2 = gq_ref[0, pl.ds(r0, c), :]
        k2 = gk_ref[0, pl.ds(r0, c), :]
        g2 = gl_ref[0, pl.ds(r0, c), :]
        b2 = _split_dot_left(tri, g2)
        blast2 = b2[c - 1:c, :]
        qe2 = q2 * jnp.exp(b2)
        kd2 = k2 * jnp.exp(blast2 - b2)
        diag = [[None] * nsub for _ in range(2)]
        for i in range(nsub):
            qi = q2[i * sub:(i + 1) * sub]
            ki = k2[i * sub:(i + 1) * sub]
            bi = b2[i * sub:(i + 1) * sub]
            a0 = jnp.zeros((sub, sub), F32)
            a1 = jnp.zeros((sub, sub), F32)
            for s in range(sub):
                x = qi * ki[s:s + 1] * jnp.exp(jnp.minimum(bi - bi[s:s + 1], 0.0))
                c0 = jnp.sum(jnp.where(lane < dk, x, 0.0), axis=1, keepdims=True)
                c1 = jnp.sum(jnp.where(lane >= dk, x, 0.0), axis=1, keepdims=True)
                a0 = jnp.where(col16 == s, c0, a0)
                a1 = jnp.where(col16 == s, c1, a1)
            diag[0][i] = jnp.where(row16 >= col16, a0, 0.0)
            diag[1][i] = jnp.where(row16 >= col16, a1, 0.0)
        for hh in range(2):
            ls = slice(hh * dk, (hh + 1) * dk)
            v = gv_ref[0, pl.ds(r0, c), hh * dv:(hh + 1) * dv]
            vb = v.astype(BF16)
            st = st_ref[0, hh]
            inter = _bdot(qe2[:, ls], st)
            b = b2[:, ls]
            outs = []
            for i in range(nsub):
                oi = _bdot(diag[hh][i], vb[i * sub:(i + 1) * sub])
                if i > 0:
                    ref = b[i * sub - 1:i * sub]
                    qi = q2[i * sub:(i + 1) * sub, ls] * jnp.exp(b[i * sub:(i + 1) * sub] - ref)
                    kj = k2[0:i * sub, ls] * jnp.exp(ref - b[0:i * sub])
                    oi = oi + _bdot(_bdot_nt(qi, kj), vb[0:i * sub])
                outs.append(oi)
            o = inter + (jnp.concatenate(outs, axis=0) if nsub > 1 else outs[0])
            a_row = jnp.exp(blast2[:, ls])
            a_col = jnp.sum(jnp.where(eye, a_row, 0.0), axis=1, keepdims=True)
            st_ref[0, hh] = a_col * st + _bdot_tn(kd2[:, ls], vb)
            on = _rms(o, gn_ref[...])
            gate = go_ref[0, pl.ds(r0, c), hh * dv:(hh + 1) * dv]
            o_ref[0, pl.ds(r0, c), hh * dv:(hh + 1) * dv] = (on * gate * jax.nn.sigmoid(gate)).astype(o_ref.dtype)
        return carry

    lax.fori_loop(0, nc, chunk, 0)


def _split_dot_left(tri, x):
    hi = x.astype(BF16)
    r1 = x - hi.astype(F32)
    mid = r1.astype(BF16)
    lo = (r1 - mid.astype(F32)).astype(BF16)
    return (jnp.dot(tri, hi, preferred_element_type=F32)
            + jnp.dot(tri, mid, preferred_element_type=F32)
            + jnp.dot(tri, lo, preferred_element_type=F32))


def _gla(gq, gk, gv, glog, go, state0, gn, nc):
    bsz, t, _ = gq.shape
    c = math.gcd(t, GLA_CHUNK)
    sub = min(GLA_SUB, c)
    n_groups = t // (c * nc)
    tri = jnp.tril(jnp.ones((c, c), F32)).astype(BF16)
    qk_spec = pl.BlockSpec((1, c * nc, 2 * GLA_DK), lambda b, p, g: (b, g, p))
    v_spec = pl.BlockSpec((1, c * nc, 2 * GLA_DV), lambda b, p, g: (b, g, p))
    st_spec = pl.BlockSpec((1, 2, GLA_DK, GLA_DV), lambda b, p, g: (b, p, 0, 0))
    return pl.pallas_call(
        functools.partial(_gla_kernel, c=c, sub=sub, nc=nc),
        grid=(bsz, GLA_HEADS // 2, n_groups),
        in_specs=[qk_spec, qk_spec, v_spec, qk_spec, v_spec, st_spec,
                  pl.BlockSpec((1, GLA_DV), lambda b, p, g: (0, 0)),
                  pl.BlockSpec((c, c), lambda b, p, g: (0, 0))],
        out_specs=(v_spec, st_spec),
        out_shape=(jax.ShapeDtypeStruct((bsz, t, GLA_HEADS * GLA_DV), BF16),
                   jax.ShapeDtypeStruct(state0.shape, F32)),
        compiler_params=_cparams(("parallel", "parallel", "arbitrary")), name="gla",
    )(gq, gk, gv, glog, go, state0, gn, tri)


def _mix_out_mem_kernel(mla_ref, gla_ref, h_ref, woa_ref, wob_ref, gm_ref, wq_ref, mk_ref, mv_ref, wo_ref,
                        o_ref, *, groups, r, interleaved):
    h1 = (h_ref[...] + jnp.dot(mla_ref[...], woa_ref[...], preferred_element_type=F32)
          + jnp.dot(gla_ref[...], wob_ref[...], preferred_element_type=F32))
    xn = _rms(h1, gm_ref[...]).astype(BF16)
    q = jnp.dot(xn, wq_ref[...], preferred_element_type=F32).astype(BF16)
    scale = MEM_HEAD_DIM ** -0.5
    outs = []
    for gi in range(groups):
        heads = []
        for hh in range(MEM_HEADS):
            ls = slice(hh * MEM_HEAD_DIM, (hh + 1) * MEM_HEAD_DIM)
            if interleaved:
                rows_h = pl.ds(hh, MEM_TOKENS, stride=MEM_HEADS)
                k = mk_ref[gi, rows_h, :].astype(BF16)
                v = mv_ref[gi, rows_h, :].astype(BF16)
            else:
                k = mk_ref[gi, :, ls].astype(BF16)
                v = mv_ref[gi, :, ls].astype(BF16)
            s = _bdot_nt(q[gi * r:(gi + 1) * r, ls], k) * scale
            s = s - jnp.max(s, axis=-1, keepdims=True)
            p = jnp.exp(s)
            p = p / jnp.sum(p, axis=-1, keepdims=True)
            heads.append(jnp.dot(p.astype(BF16), v, preferred_element_type=F32))
        outs.append(jnp.concatenate(heads, axis=1))
    o = jnp.concatenate(outs, axis=0) if groups > 1 else outs[0]
    o_ref[...] = h1 + jnp.dot(o.astype(BF16), wo_ref[...], preferred_element_type=F32)


def _mix_out_mem(mla_o, gla_o, h, woa, wob, gm, wq, mk, mv, wo, groups, r):
    t = h.shape[0]
    tm = groups * r
    row = lambda w: pl.BlockSpec((tm, w), lambda i: (i, 0))
    interleaved = mk.shape[-1] == MEM_HEAD_DIM
    if interleaved:
        kv_spec = pl.BlockSpec((groups, MEM_TOKENS * MEM_HEADS, MEM_HEAD_DIM), lambda i: (i, 0, 0))
    else:
        kv_spec = pl.BlockSpec((1, MEM_TOKENS, MEM_WIDTH), lambda i: (0, 0, 0))
    return pl.pallas_call(
        functools.partial(_mix_out_mem_kernel, groups=groups, r=r, interleaved=interleaved),
        grid=(t // tm,),
        in_specs=[row(512), row(512), row(D_MODEL), _full(woa.shape), _full(wob.shape), _full(gm.shape),
                  _full(wq.shape), kv_spec, kv_spec, _full(wo.shape)],
        out_specs=row(D_MODEL),
        out_shape=jax.ShapeDtypeStruct((t, D_MODEL), F32),
        compiler_params=_cparams(("parallel",)), name="mix_out_mem",
    )(mla_o, gla_o, h, woa, wob, gm, wq, mk, mv, wo)


MOE_TB = 512
MOE_CHUNK = 16
MOE_TM = 256
MOE_CPT = MOE_TM // MOE_CHUNK
MOE_ROW_BLOCK = 512


def _moe_local_rows(tb):
    worst = TOP_K * tb + N_EXPERTS * (MOE_CHUNK - 1)
    return -(-worst // MOE_ROW_BLOCK) * MOE_ROW_BLOCK


def _route_kernel(h_ref, g_ref, wrt_ref, brc_ref, u_ref, ltri_ref, xl_ref, pos_ref, gate_ref, cnt_ref, *, rl):
    tb = h_ref.shape[0]
    xn = _rms(h_ref[...], g_ref[...])
    x_hi = xn.astype(BF16)
    x_lo = (xn - x_hi.astype(F32)).astype(BF16)
    w = wrt_ref[...]
    w_hi = w.astype(BF16)
    w_lo = (w - w_hi.astype(F32)).astype(BF16)
    logits = _bdot_nt(w_hi, x_hi) + _bdot_nt(w_lo, x_hi) + _bdot_nt(w_hi, x_lo) + brc_ref[...]
    e_iota = lax.broadcasted_iota(jnp.int32, (N_EXPERTS, tb), 0)
    work = logits
    sel = jnp.zeros((N_EXPERTS, tb), jnp.bool_)
    top = None
    for _ in range(TOP_K):
        m = jnp.max(work, axis=0, keepdims=True)
        if top is None:
            top = m
        idx = jnp.min(jnp.where(work == m, e_iota, N_EXPERTS), axis=0, keepdims=True)
        pick = e_iota == idx
        sel = jnp.logical_or(sel, pick)
        work = jnp.where(pick, -jnp.inf, work)
    ex = jnp.where(sel, jnp.exp(logits - top), 0.0)
    gates = ex / jnp.sum(ex, axis=0, keepdims=True)
    self32 = jnp.where(sel, 1.0, 0.0)
    prefix = jnp.dot(self32.astype(BF16), u_ref[...], preferred_element_type=F32)
    cnt = jnp.sum(self32, axis=1, keepdims=True)
    cnt_ref[0] = jnp.broadcast_to(cnt, (N_EXPERTS, 128))
    padded = jnp.floor((cnt + (MOE_CHUNK - 1)) * (1.0 / MOE_CHUNK)) * MOE_CHUNK
    off = jnp.dot(ltri_ref[...], jnp.broadcast_to(padded, (N_EXPERTS, 128)).astype(BF16),
                  preferred_element_type=F32)[:, 0:1]
    pos = off + prefix
    pending = sel
    pos_rows, gate_rows = [], []
    for _ in range(TOP_K):
        emin = jnp.min(jnp.where(pending, e_iota, N_EXPERTS), axis=0, keepdims=True)
        pick = e_iota == emin
        pos_rows.append(jnp.sum(jnp.where(pick, pos, 0.0), axis=0, keepdims=True))
        gate_rows.append(jnp.sum(jnp.where(pick, gates, 0.0), axis=0, keepdims=True))
        pending = jnp.logical_and(pending, jnp.logical_not(pick))
    zeros4 = jnp.zeros((8 - TOP_K, tb), F32)
    pos8 = jnp.concatenate(pos_rows + [zeros4 - 1.0], axis=0)
    pos_ref[0] = pos8.astype(jnp.int32)
    gate_ref[0] = jnp.concatenate(gate_rows + [zeros4], axis=0)
    pk = [p.astype(jnp.int32) for p in pos_rows]
    for rb in range(rl // MOE_ROW_BLOCK):
        r_iota = rb * MOE_ROW_BLOCK + lax.broadcasted_iota(jnp.int32, (MOE_ROW_BLOCK, tb), 0)
        hit = r_iota == pk[0]
        for k in range(1, TOP_K):
            hit = jnp.logical_or(hit, r_iota == pk[k])
        onehot = jnp.where(hit, 1.0, 0.0).astype(BF16)
        xl_ref[0, rb * MOE_ROW_BLOCK:(rb + 1) * MOE_ROW_BLOCK, :] = jnp.dot(
            onehot, x_hi, preferred_element_type=F32).astype(BF16)


def _moe_route(h, g, wrt, brc, tb, rl):
    n = h.shape[0]
    nb = n // tb
    u = jnp.triu(jnp.ones((tb, tb), F32), 1).astype(BF16)
    ltri = jnp.tril(jnp.ones((N_EXPERTS, N_EXPERTS), F32), -1).astype(BF16)
    c2 = lambda a: pl.BlockSpec(a.shape, lambda i: (0, 0))
    return pl.pallas_call(
        functools.partial(_route_kernel, rl=rl),
        grid=(nb,),
        in_specs=[pl.BlockSpec((tb, D_MODEL), lambda i: (i, 0)), c2(g), c2(wrt), c2(brc), c2(u), c2(ltri)],
        out_specs=(pl.BlockSpec((1, rl, D_MODEL), lambda i: (i, 0, 0)),
                   pl.BlockSpec((1, 8, tb), lambda i: (i, 0, 0)),
                   pl.BlockSpec((1, 8, tb), lambda i: (i, 0, 0)),
                   pl.BlockSpec((1, N_EXPERTS, 128), lambda i: (i, 0, 0))),
        out_shape=(jax.ShapeDtypeStruct((nb, rl, D_MODEL), BF16),
                   jax.ShapeDtypeStruct((nb, 8, tb), jnp.int32),
                   jax.ShapeDtypeStruct((nb, 8, tb), F32),
                   jax.ShapeDtypeStruct((nb, N_EXPERTS, 128), F32)),
        compiler_params=_cparams(("parallel",)), name="moe_route",
    )(h, g, wrt, brc, u, ltri)


def _moe_tables(cnt, rl, n_tiles):
    nb = cnt.shape[0]
    cpl = rl // MOE_CHUNK
    nch = (cnt + (MOE_CHUNK - 1)) // MOE_CHUNK
    loc_off = jnp.cumsum(nch, axis=1) - nch
    seg_end = jnp.cumsum(nch, axis=0)
    seg_start = seg_end - nch
    tot = seg_end[-1]
    totp = (tot + (MOE_CPT - 1)) // MOE_CPT * MOE_CPT
    e_end = jnp.cumsum(totp)
    e_start = e_end - totp
    n_used = e_end[-1] // MOE_CPT
    slots = jnp.arange(n_tiles * MOE_CPT, dtype=jnp.int32)
    e_of = jnp.minimum(jnp.sum(slots[:, None] >= e_end[None, :], axis=1), N_EXPERTS - 1).astype(jnp.int32)
    o = slots - e_start[e_of]
    valid = jnp.logical_and(o < tot[e_of], slots < e_end[-1])
    b_of = jnp.minimum(jnp.sum(o[:, None] >= seg_end.T[e_of], axis=1), nb - 1).astype(jnp.int32)
    loc = loc_off[b_of, e_of] + (o - seg_start[b_of, e_of])
    real = b_of * cpl + loc
    spare = nb * cpl + (slots % (2 * MOE_CPT))
    src = jnp.where(valid, real, 0).astype(jnp.int32)
    dst = jnp.where(valid, real, spare).astype(jnp.int32)
    tile_slot = jnp.minimum(jnp.arange(n_tiles, dtype=jnp.int32), n_used - 1) * MOE_CPT
    te = e_of[tile_slot]
    return te, src, dst, n_used.astype(jnp.int32).reshape(1)


def _experts_kernel(te_ref, src_ref, dst_ref, nu_ref, xl_hbm, wup_ref, bup_ref, wdn_ref, bdn_ref, yinit_hbm,
                    yl_hbm, xbuf, ybuf, isem, osem):
    del yinit_hbm
    i = pl.program_id(0)
    nu = nu_ref[0]
    slot = lax.rem(i, 2)

    def in_copy(tile, sl, c):
        return pltpu.make_async_copy(xl_hbm.at[src_ref[tile * MOE_CPT + c]],
                                     xbuf.at[sl, pl.ds(c * MOE_CHUNK, MOE_CHUNK), :], isem.at[sl])

    def out_copy(tile, sl, c):
        return pltpu.make_async_copy(ybuf.at[sl, pl.ds(c * MOE_CHUNK, MOE_CHUNK), :],
                                     yl_hbm.at[dst_ref[tile * MOE_CPT + c]], osem.at[sl])

    @pl.when(i == 0)
    def _():
        for c in range(MOE_CPT):
            in_copy(0, 0, c).start()

    @pl.when(i < nu)
    def _():
        @pl.when(i + 1 < nu)
        def _():
            for c in range(MOE_CPT):
                in_copy(i + 1, 1 - slot, c).start()

        for c in range(MOE_CPT):
            in_copy(i, slot, c).wait()

        @pl.when(i >= 2)
        def _():
            for c in range(MOE_CPT):
                out_copy(i - 2, slot, c).wait()

        x = xbuf[slot]
        hu = jnp.dot(x, wup_ref[0], preferred_element_type=F32) + bup_ref[0]
        gate = jnp.minimum(hu[:, 0:D_FF], SWIGLU_LIMIT)
        up = jnp.clip(hu[:, D_FF:2 * D_FF], -SWIGLU_LIMIT, SWIGLU_LIMIT)
        act = (up + 1.0) * gate * jax.nn.sigmoid(SWIGLU_ALPHA * gate)
        y = jnp.dot(act.astype(BF16), wdn_ref[0], preferred_element_type=F32) + bdn_ref[0]
        ybuf[slot] = y.astype(BF16)
        for c in range(MOE_CPT):
            out_copy(i, slot, c).start()

        @pl.when(i == nu - 1)
        def _():
            for c in range(MOE_CPT):
                out_copy(i, slot, c).wait()

            @pl.when(i >= 1)
            def _():
                for c in range(MOE_CPT):
                    out_copy(i - 1, 1 - slot, c).wait()


def _moe_experts(te, src, dst, n_used, xl, wup, bup, wdn, bdn, n_tiles):
    nb, rl, _ = xl.shape
    cpl = rl // MOE_CHUNK
    xl_chunks = xl.reshape(nb * cpl, MOE_CHUNK, D_MODEL)
    y_init = jnp.zeros(((nb + 1) * cpl, MOE_CHUNK, D_MODEL), BF16)
    wmap = lambda i, te, src, dst, nu: (te[i], 0, 0)
    grid_spec = pltpu.PrefetchScalarGridSpec(
        num_scalar_prefetch=4, grid=(n_tiles,),
        in_specs=[pl.BlockSpec(memory_space=pl.ANY),
                  pl.BlockSpec((1, D_MODEL, 2 * D_FF), wmap), pl.BlockSpec((1, 1, 2 * D_FF), wmap),
                  pl.BlockSpec((1, D_FF, D_MODEL), wmap), pl.BlockSpec((1, 1, D_MODEL), wmap),
                  pl.BlockSpec(memory_space=pl.ANY)],
        out_specs=pl.BlockSpec(memory_space=pl.ANY),
        scratch_shapes=[pltpu.VMEM((2, MOE_TM, D_MODEL), BF16), pltpu.VMEM((2, MOE_TM, D_MODEL), BF16),
                        pltpu.SemaphoreType.DMA((2,)), pltpu.SemaphoreType.DMA((2,))])
    yl = pl.pallas_call(
        _experts_kernel, grid_spec=grid_spec,
        out_shape=jax.ShapeDtypeStruct(y_init.shape, BF16),
        input_output_aliases={9: 0},
        compiler_params=_cparams(("arbitrary",)), name="moe_experts",
    )(te, src, dst, n_used, xl_chunks, wup, bup, wdn, bdn, y_init)
    return yl.reshape(nb + 1, rl, D_MODEL)


def _combine_kernel(yl_ref, pos_ref, gate_ref, h_ref, gf_ref, y_ref, *, rl):
    tb = h_ref.shape[0]
    pos = pos_ref[0]
    gate = gate_ref[0]
    acc = h_ref[...]
    for rb in range(rl // MOE_ROW_BLOCK):
        r_iota = rb * MOE_ROW_BLOCK + lax.broadcasted_iota(jnp.int32, (MOE_ROW_BLOCK, tb), 0)
        w = jnp.zeros((MOE_ROW_BLOCK, tb), F32)
        for k in range(TOP_K):
            w = jnp.where(r_iota == pos[k:k + 1], gate[k:k + 1], w)
        acc = acc + _bdot_tn(w, yl_ref[0, rb * MOE_ROW_BLOCK:(rb + 1) * MOE_ROW_BLOCK, :])
    y_ref[...] = _rms(acc, gf_ref[...])


def _moe_combine(yl, pos, gate, h, gf, tb):
    n = h.shape[0]
    rl = yl.shape[1]
    return pl.pallas_call(
        functools.partial(_combine_kernel, rl=rl),
        grid=(n // tb,),
        in_specs=[pl.BlockSpec((1, rl, D_MODEL), lambda i: (i, 0, 0)),
                  pl.BlockSpec((1, 8, tb), lambda i: (i, 0, 0)), pl.BlockSpec((1, 8, tb), lambda i: (i, 0, 0)),
                  pl.BlockSpec((tb, D_MODEL), lambda i: (i, 0)), pl.BlockSpec(gf.shape, lambda i: (0, 0))],
        out_specs=pl.BlockSpec((tb, D_MODEL), lambda i: (i, 0)),
        out_shape=jax.ShapeDtypeStruct((n, D_MODEL), F32),
        compiler_params=_cparams(("parallel",)), name="moe_combine",
    )(yl, pos, gate, h, gf)


def _moe(h, g, wr, br, wup, bup, wdn, bdn, gf):
    n = h.shape[0]
    tb = math.gcd(n, MOE_TB)
    rl = _moe_local_rows(tb)
    nb = n // tb
    n_tiles = -(-(nb * (rl // MOE_CHUNK) + N_EXPERTS * (MOE_CPT - 1)) // MOE_CPT)
    xl, pos, gate, cnt = _moe_route(h, g, wr.T, br.reshape(N_EXPERTS, 1), tb, rl)
    te, src, dst, n_used = _moe_tables(cnt[:, :, 0].astype(jnp.int32), rl, n_tiles)
    yl = _moe_experts(te, src, dst, n_used, xl, wup, bup, wdn, bdn, n_tiles)
    return _moe_combine(yl, pos, gate, h, gf, tb)


def _rope_tables(pos):
    half = MLA_ROPE // 2
    inv = ROPE_THETA ** (-jnp.arange(half, dtype=F32) / half)
    ang = pos.astype(F32)[:, None] * inv[None, :]
    cos, sin = jnp.cos(ang), jnp.sin(ang)
    pad = jnp.zeros((pos.shape[0], 128 - MLA_ROPE), F32)
    return jnp.concatenate([cos, cos, pad], axis=1), jnp.concatenate([sin, sin, pad], axis=1)


def _rot_cols(w):
    half = w.shape[-1] // 2
    return jnp.concatenate([-w[..., half:], w[..., :half]], axis=-1)


def _pad_cols(w, width):
    return jnp.pad(w, ((0, 0), (0, width - w.shape[1])))


def _prep_mix_weights(norm_g, w_in, q_a_norm_g, w_uq, kv_a_norm_g, w_uk, gla_gate_w2, gla_gate_b):
    o = 0
    parts = {}
    for name, size in (("cq", MLA_Q_RANK), ("ckv", MLA_KV_RANK), ("kr", MLA_ROPE), ("gq", 256), ("gk", 256),
                       ("gv", 512), ("gg", GLA_GATE_RANK), ("go", 512)):
        parts[name] = w_in[:, o:o + size]
        o += size
    w1 = jnp.concatenate([parts["cq"], parts["ckv"], parts["gq"] * (GLA_DK ** -0.5), parts["gk"], parts["gv"],
                          parts["go"]], axis=1).astype(BF16)
    w2 = jnp.concatenate([_pad_cols(parts["kr"], 128), _pad_cols(_rot_cols(parts["kr"]), 128),
                          _pad_cols(parts["gg"], 128)], axis=1).astype(BF16)
    wuq = w_uq.reshape(MLA_Q_RANK, MLA_HEADS, MLA_NOPE + MLA_ROPE)
    wuq_n = wuq[:, :, :MLA_NOPE].reshape(MLA_Q_RANK, MLA_HEADS * MLA_NOPE).astype(BF16)
    wr = wuq[:, :, MLA_NOPE:]
    widen = lambda w: jnp.pad(w, ((0, 0), (0, 0), (0, 128 - MLA_ROPE))).reshape(MLA_Q_RANK, MLA_HEADS * 128)
    wuq_r = widen(wr).astype(BF16)
    wuq_rr = widen(_rot_cols(wr)).astype(BF16)
    wukt = jnp.transpose(w_uk, (1, 2, 0)).astype(BF16)
    wg2 = jnp.pad(gla_gate_w2, ((0, 128 - GLA_GATE_RANK), (0, 0))).astype(BF16)
    return (norm_g[None], w1, w2, q_a_norm_g[None], wuq_n, wuq_r, wuq_rr, kv_a_norm_g[None], wukt, wg2,
            gla_gate_b[None])


def _prep_wuv(w_uv):
    w = jnp.transpose(w_uv, (1, 0, 2))
    z = jnp.zeros_like(w)
    even = jnp.concatenate([w, z], axis=-1)
    odd = jnp.concatenate([z, w], axis=-1)
    sel = (jnp.arange(MLA_HEADS) % 2 == 0)[:, None, None]
    return jnp.where(sel, even, odd).astype(BF16)


def kernel(x_prompt, x_sample, mem_prompt, cache_ckv, cache_krope, state_gla, cache_mem_k, cache_mem_v,
           page_table, norm_mix_g, w_in, q_a_norm_g, w_uq, kv_a_norm_g, w_uk, w_uv, gla_gate_w2, gla_gate_b,
           gla_norm_g, w_out, norm_mem_g, mem_in_norm_g, w_mem_q, w_mem_k, w_mem_v, w_mem_o, norm_moe_g,
           w_router, b_router, w_moe_up, b_moe_up, w_moe_down, b_moe_down, norm_final_g):
    depth = w_in.shape[0]
    assert depth == 1 and x_prompt.shape[0] == 1
    l = 0
    b, t, _ = x_prompt.shape
    db, ts, _ = x_sample.shape
    n_pages = page_table.shape[1]
    past = n_pages * PAGE

    mixw = _prep_mix_weights(norm_mix_g[l], w_in[l], q_a_norm_g[l], w_uq[l], kv_a_norm_g[l], w_uk[l],
                             gla_gate_w2[l], gla_gate_b[l])
    wuv_pair = _prep_wuv(w_uv[l])
    woa = w_out[l][:MLA_HEADS * MLA_V].astype(BF16)
    wob = w_out[l][MLA_HEADS * MLA_V:].astype(BF16)
    gn = gla_norm_g[l][None]
    gm = norm_mem_g[l][None]
    wmq = w_mem_q[l].astype(BF16)
    wmo = w_mem_o[l].astype(BF16)
    wup = w_moe_up[l].astype(BF16)
    wdn = w_moe_down[l].astype(BF16)
    bup = b_moe_up[l][:, None, :]
    bdn = b_moe_down[l][:, None, :]
    moe_args = (norm_moe_g[l][None], w_router[l], b_router[l], wup, bup, wdn, bdn, norm_final_g[None])

    hp = x_prompt.reshape(t, D_MODEL)
    cs_p, sn_p = _rope_tables(jnp.arange(t))
    tm_p = min(512, t)
    q_p, ckv_p, kr_p, kcat_p, gq, gk, gv, gl, go = _mix_in(hp, cs_p, sn_p, mixw, tm_p)
    tq = min(256, t)
    tk = min(512, t)
    mla_p = _mla_prompt(q_p, kcat_p, wuv_pair, tq, tk)
    zero_state = jnp.zeros((1, GLA_HEADS, GLA_DK, GLA_DV), F32)
    nc_p = max(1, min(4, t // GLA_CHUNK))
    gla_p, st_p = _gla(gq[None], gk[None], gv[None], gl[None], go[None], zero_state, gn, nc_p)
    mk_p, mv_p = _mem_kv(mem_prompt[0], mem_in_norm_g[l][None], w_mem_k[l].astype(BF16), w_mem_v[l].astype(BF16))
    hp = _mix_out_mem(mla_p, gla_p[0], hp, woa, wob, gm, wmq, mk_p[None], mv_p[None], wmo, 1, tm_p)

    n_s = db * ts
    hs = x_sample.reshape(n_s, D_MODEL)
    cs_s, sn_s = _rope_tables(past + jnp.arange(ts))
    cs_s = jnp.tile(cs_s, (db, 1))
    sn_s = jnp.tile(sn_s, (db, 1))
    tm_s = min(512, n_s)
    q_s, ckv_s, kr_s, kcat_s, gq, gk, gv, gl, go = _mix_in(hs, cs_s, sn_s, mixw, tm_s)
    n_pg = min(16, n_pages)
    mla_s = _mla_sample(page_table, q_s.reshape(MLA_HEADS, db, ts, QK_WIDTH), kcat_s.reshape(db, ts, QK_WIDTH),
                        wuv_pair, cache_ckv[l], jnp.swapaxes(cache_krope[l], 1, 2), n_pg)
    r3 = lambda a: a.reshape(db, ts, a.shape[-1])
    gla_s, st_s = _gla(r3(gq), r3(gk), r3(gv), r3(gl), r3(go), state_gla[l], gn, 1)
    groups = min(8, db)
    hs = _mix_out_mem(mla_s.reshape(n_s, -1), gla_s.reshape(n_s, -1), hs, woa, wob, gm, wmq,
                      cache_mem_k[l].reshape(db, MEM_TOKENS * MEM_HEADS, MEM_HEAD_DIM),
                      cache_mem_v[l].reshape(db, MEM_TOKENS * MEM_HEADS, MEM_HEAD_DIM), wmo, groups, ts)
    y = _moe(jnp.concatenate([hp, hs], axis=0), *moe_args)
    y_p, y_s = y[:t], y[t:]

    return (y_p.reshape(b, t, D_MODEL), y_s.reshape(db, ts, D_MODEL),
            ckv_p.reshape(1, b, t, MLA_KV_RANK), kr_p.reshape(1, b, t, MLA_ROPE),
            st_p.reshape(1, b, GLA_HEADS, GLA_DK, GLA_DV),
            mk_p.reshape(1, b, MEM_TOKENS, MEM_HEADS, MEM_HEAD_DIM),
            mv_p.reshape(1, b, MEM_TOKENS, MEM_HEADS, MEM_HEAD_DIM),
            ckv_s.reshape(1, db, ts, MLA_KV_RANK), kr_s.reshape(1, db, ts, MLA_ROPE),
            st_s.reshape(1, db, GLA_HEADS, GLA_DK, GLA_DV))
```

```python
import functools
import math

import jax
import jax.numpy as jnp
from jax import lax
from jax.experimental import pallas as pl
from jax.experimental.pallas import tpu as pltpu

F32 = jnp.float32
BF16 = jnp.bfloat16

D_MODEL = 1024
MLA_HEADS = 8
MLA_NOPE = 64
MLA_ROPE = 32
MLA_V = 64
MLA_Q_RANK = 384
MLA_KV_RANK = 256
MLA_SCALE = (MLA_NOPE + MLA_ROPE) ** -0.5
MLA_QSCALE = MLA_SCALE * math.log2(math.e)
ROPE_THETA = 10000.0
QK_WIDTH = 384
GLA_HEADS = 4
GLA_DK = 64
GLA_DV = 128
GLA_GATE_RANK = 16
GLA_GATE_NORMALIZER = 16.0
GLA_CHUNK = 64
GLA_SUB = 16
MEM_TOKENS = 256
MEM_HEADS = 4
MEM_HEAD_DIM = 128
MEM_WIDTH = MEM_HEADS * MEM_HEAD_DIM
N_EXPERTS = 32
TOP_K = 4
D_FF = 1024
SWIGLU_LIMIT = 7.0
SWIGLU_ALPHA = 1.702
EPS = 1e-6
PAGE = 128
NEG = -1e30
SAMPLE_SLOTS = 4
SAMPLE_AHEAD = 2

VMEM_LIMIT = 56 * 1024 * 1024


def _cparams(sem):
    return pltpu.CompilerParams(dimension_semantics=sem, vmem_limit_bytes=VMEM_LIMIT)


def _rms(x, g):
    var = jnp.mean(x * x, axis=-1, keepdims=True)
    return x * lax.rsqrt(var + EPS) * g


def _bdot(a, b):
    return jnp.dot(a.astype(BF16), b.astype(BF16), preferred_element_type=F32)


def _bdot_nt(a, b):
    return lax.dot_general(a.astype(BF16), b.astype(BF16), (((1,), (1,)), ((), ())),
                           preferred_element_type=F32)


def _bdot_tn(a, b):
    return lax.dot_general(a.astype(BF16), b.astype(BF16), (((0,), (0,)), ((), ())),
                           preferred_element_type=F32)


def _split_dot(a, b_exact):
    hi = a.astype(BF16)
    r1 = a - hi.astype(F32)
    mid = r1.astype(BF16)
    lo = (r1 - mid.astype(F32)).astype(BF16)
    return (jnp.dot(hi, b_exact, preferred_element_type=F32)
            + jnp.dot(mid, b_exact, preferred_element_type=F32)
            + jnp.dot(lo, b_exact, preferred_element_type=F32))


def _full(shape):
    n = len(shape)
    return pl.BlockSpec(shape, lambda *_: (0,) * n)


def _mem_kv_kernel(mem_ref, g_ref, wk_ref, wv_ref, k_ref, v_ref):
    mn = _rms(mem_ref[...], g_ref[...]).astype(BF16)
    k_ref[...] = jnp.dot(mn, wk_ref[...], preferred_element_type=F32)
    v_ref[...] = jnp.dot(mn, wv_ref[...], preferred_element_type=F32)


def _mem_kv(mem, g, wk, wv):
    m = mem.shape[0]
    return pl.pallas_call(
        _mem_kv_kernel,
        out_shape=(jax.ShapeDtypeStruct((m, MEM_WIDTH), F32), jax.ShapeDtypeStruct((m, MEM_WIDTH), F32)),
        name="mem_kv",
    )(mem, g, wk, wv)


_C_CQ, _C_CKV, _C_GQ, _C_GK, _C_GV, _C_GO, _C_END = 0, 384, 640, 896, 1152, 1664, 2176


def _log_sigmoid(x):
    return jnp.minimum(x, 0.0) - jnp.log1p(jnp.exp(-jnp.abs(x)))


def _mix_in_kernel(h_ref, cs_ref, sn_ref, g_ref, w1_ref, w2_ref, qag_ref, wuqn_ref, wuqr_ref, wuqrr_ref,
                   kvag_ref, wukt_ref, wg2_ref, bg_ref,
                   q_ref, ckv_ref, kr_ref, kcat_ref, gq_ref, gk_ref, gv_ref, glog_ref, go_ref):
    xb = _rms(h_ref[...], g_ref[...]).astype(BF16)
    z1 = jnp.dot(xb, w1_ref[...], preferred_element_type=F32)
    z2 = jnp.dot(xb, w2_ref[...], preferred_element_type=F32)
    cs = cs_ref[...]
    sn = sn_ref[...]
    ckv = _rms(z1[:, _C_CKV:_C_GQ], kvag_ref[...])
    ckv_ref[...] = ckv
    krp = z2[:, 0:128] * cs + z2[:, 128:256] * sn
    kr_ref[...] = krp[:, 0:MLA_ROPE]
    kcat_ref[:, 0:MLA_KV_RANK] = ckv.astype(BF16)
    kcat_ref[:, MLA_KV_RANK:QK_WIDTH] = krp.astype(BF16)
    cqn = _rms(z1[:, _C_CQ:_C_CKV], qag_ref[...]).astype(BF16)
    qn = jnp.dot(cqn, wuqn_ref[...], preferred_element_type=F32)
    csw = jnp.concatenate([cs] * MLA_HEADS, axis=1)
    snw = jnp.concatenate([sn] * MLA_HEADS, axis=1)
    qr = (jnp.dot(cqn, wuqr_ref[...], preferred_element_type=F32) * csw
          + jnp.dot(cqn, wuqrr_ref[...], preferred_element_type=F32) * snw)
    for h in range(MLA_HEADS):
        ql = _bdot(qn[:, h * MLA_NOPE:(h + 1) * MLA_NOPE], wukt_ref[h])
        q_ref[h, :, 0:MLA_KV_RANK] = (ql * MLA_QSCALE).astype(BF16)
        q_ref[h, :, MLA_KV_RANK:QK_WIDTH] = (qr[:, h * 128:(h + 1) * 128] * MLA_QSCALE).astype(BF16)
    gq_ref[...] = z1[:, _C_GQ:_C_GK]
    gk_ref[...] = z1[:, _C_GK:_C_GV]
    gv_ref[...] = z1[:, _C_GV:_C_GO]
    go_ref[...] = z1[:, _C_GO:_C_END]
    gate = _bdot(z2[:, 256:384], wg2_ref[...]) + bg_ref[...]
    glog_ref[...] = _log_sigmoid(gate) * (1.0 / GLA_GATE_NORMALIZER)


def _mix_in(h, cs, sn, wts, tm):
    t = h.shape[0]
    grid = (t // tm,)
    row = lambda w: pl.BlockSpec((tm, w), lambda i: (i, 0))
    in_specs = [row(D_MODEL), row(128), row(128)] + [_full(w.shape) for w in wts]
    out_shape = (
        jax.ShapeDtypeStruct((MLA_HEADS, t, QK_WIDTH), BF16),
        jax.ShapeDtypeStruct((t, MLA_KV_RANK), F32),
        jax.ShapeDtypeStruct((t, MLA_ROPE), F32),
        jax.ShapeDtypeStruct((t, QK_WIDTH), BF16),
        jax.ShapeDtypeStruct((t, 256), F32),
        jax.ShapeDtypeStruct((t, 256), F32),
        jax.ShapeDtypeStruct((t, 512), F32),
        jax.ShapeDtypeStruct((t, 256), F32),
        jax.ShapeDtypeStruct((t, 512), F32),
    )
    out_specs = (
        pl.BlockSpec((MLA_HEADS, tm, QK_WIDTH), lambda i: (0, i, 0)),
        row(MLA_KV_RANK), row(MLA_ROPE), row(QK_WIDTH), row(256), row(256), row(512), row(256), row(512),
    )
    return pl.pallas_call(
        _mix_in_kernel, grid=grid, in_specs=in_specs, out_specs=out_specs, out_shape=out_shape,
        compiler_params=_cparams(("parallel",)), name="mix_in",
    )(h, cs, sn, *wts)


def _softmax_step(s, m_ref, l_ref, acc_ref, v):
    m_old = m_ref[...]
    m_new = jnp.maximum(m_old, jnp.max(s, axis=-1, keepdims=True))
    alpha = jnp.exp2(m_old - m_new)
    p = jnp.exp2(s - m_new)
    l_ref[...] = alpha * l_ref[...] + jnp.sum(p, axis=-1, keepdims=True)
    acc_ref[...] = alpha * acc_ref[...] + jnp.dot(p.astype(BF16), v, preferred_element_type=F32)
    m_ref[...] = m_new


def _mla_out(acc_ref, l_ref, wuv_ref, o_ref, rows):
    o = acc_ref[...] / l_ref[...]
    for p in range(MLA_HEADS // 2):
        a = o[(2 * p) * rows:(2 * p + 1) * rows].astype(BF16)
        b = o[(2 * p + 1) * rows:(2 * p + 2) * rows].astype(BF16)
        y = (jnp.dot(a, wuv_ref[2 * p], preferred_element_type=F32)
             + jnp.dot(b, wuv_ref[2 * p + 1], preferred_element_type=F32))
        o_ref[:, p * 128:(p + 1) * 128] = y.astype(o_ref.dtype)


def _mla_prompt_kernel(q_ref, k_ref, wuv_ref, o_ref, s_ref, m_ref, l_ref, acc_ref, *, tq, tk):
    i = pl.program_id(0)
    rows = MLA_HEADS * tq
    q = q_ref[...].reshape(rows, QK_WIDTH)
    m_ref[...] = jnp.full(m_ref.shape, NEG, F32)
    l_ref[...] = jnp.zeros(l_ref.shape, F32)
    acc_ref[...] = jnp.zeros(acc_ref.shape, F32)
    q0 = i * tq
    n = q0 // tk + 1

    def kblk(j):
        return k_ref[pl.ds(pl.multiple_of(j * tk, tk), tk), :]

    def scores(j, slot):
        s_ref[slot] = _bdot_nt(q, kblk(j))

    def update(j, slot, masked):
        s = s_ref[slot]
        if masked:
            qpos = q0 + (lax.broadcasted_iota(jnp.int32, (rows, tk), 0) & (tq - 1))
            kpos = j * tk + lax.broadcasted_iota(jnp.int32, (rows, tk), 1)
            s = jnp.where(kpos <= qpos, s, NEG)
        _softmax_step(s, m_ref, l_ref, acc_ref, kblk(j)[:, 0:MLA_KV_RANK])

    scores(0, 0)
    n_pair = (n - 1) // 2

    def pair(jj, c):
        j = 2 * jj
        scores(j + 1, 1)
        update(j, 0, False)
        scores(j + 2, 0)
        update(j + 1, 1, False)
        return c

    lax.fori_loop(0, n_pair, pair, 0)
    left = (n - 1) - 2 * n_pair

    @pl.when(left == 0)
    def _():
        update(n - 1, 0, True)

    @pl.when(left == 1)
    def _():
        scores(n - 1, 1)
        update(n - 2, 0, False)
        update(n - 1, 1, True)

    _mla_out(acc_ref, l_ref, wuv_ref, o_ref, tq)


def _mla_prompt(q, kcat, wuv_pair, tq, tk):
    t = kcat.shape[0]
    assert tk % tq == 0 and t % tk == 0
    rows = MLA_HEADS * tq
    return pl.pallas_call(
        functools.partial(_mla_prompt_kernel, tq=tq, tk=tk),
        grid=(t // tq,),
        in_specs=[pl.BlockSpec((MLA_HEADS, tq, QK_WIDTH), lambda i: (0, i, 0)),
                  pl.BlockSpec(kcat.shape, lambda i: (0, 0), pipeline_mode=pl.Buffered(1)),
                  pl.BlockSpec(wuv_pair.shape, lambda i: (0, 0, 0), pipeline_mode=pl.Buffered(1))],
        out_specs=pl.BlockSpec((tq, MLA_HEADS * MLA_V), lambda i: (i, 0)),
        out_shape=jax.ShapeDtypeStruct((t, MLA_HEADS * MLA_V), BF16),
        scratch_shapes=[pltpu.VMEM((2, rows, tk), F32),
                        pltpu.VMEM((rows, 1), F32), pltpu.VMEM((rows, 1), F32),
                        pltpu.VMEM((rows, MLA_KV_RANK), F32)],
        compiler_params=_cparams(("arbitrary",)), name="mla_prompt",
    )(q, kcat, wuv_pair)


def _mla_sample_kernel(pt_ref, q_ref, knew_ref, wuv_ref, ckv_hbm, krt_hbm, o_ref,
                       ckv_buf, krt_buf, sem_c, sem_r, m_ref, l_ref, acc_ref, *, n_pg, n_groups, t_new):
    b = pl.program_id(0)
    nb = pl.num_programs(0)
    rows = MLA_HEADS * t_new
    q = q_ref[...].astype(F32).reshape(rows, QK_WIDTH).astype(BF16)

    def page_copies(bb, g, lookup):
        slot = g % SAMPLE_SLOTS
        out = []
        for i in range(n_pg):
            page = pt_ref[bb, g * n_pg + i] if lookup else 0
            out.append(pltpu.make_async_copy(ckv_hbm.at[page], ckv_buf.at[slot, i], sem_c.at[slot]))
            out.append(pltpu.make_async_copy(krt_hbm.at[page], krt_buf.at[slot, i], sem_r.at[slot]))
        return out

    def start_group(g):
        if g < n_groups:
            for c in page_copies(b, g, True):
                c.start()
        else:
            @pl.when(b + 1 < nb)
            def _():
                for c in page_copies(b + 1, g - n_groups, True):
                    c.start()

    def keys(g):
        slot = g % SAMPLE_SLOTS
        kv = ckv_buf[slot].reshape(n_pg * PAGE, MLA_KV_RANK).astype(BF16)
        krt = jnp.concatenate([krt_buf[slot, i] for i in range(n_pg)], axis=1).astype(BF16)
        return kv, krt

    def scores(g):
        for c in page_copies(b, g, False):
            c.wait()
        kv, krt = keys(g)
        return (_bdot_nt(q[:, 0:MLA_KV_RANK], kv)
                + jnp.dot(q[:, MLA_KV_RANK:MLA_KV_RANK + MLA_ROPE], krt, preferred_element_type=F32))

    @pl.when(b == 0)
    def _():
        for g in range(SAMPLE_AHEAD):
            for c in page_copies(0, g, True):
                c.start()

    m_ref[...] = jnp.full(m_ref.shape, NEG, F32)
    l_ref[...] = jnp.zeros(l_ref.shape, F32)
    acc_ref[...] = jnp.zeros(acc_ref.shape, F32)
    s = scores(0)
    for g in range(n_groups):
        start_group(g + SAMPLE_AHEAD)
        s_next = scores(g + 1) if g + 1 < n_groups else None
        _softmax_step(s, m_ref, l_ref, acc_ref, keys(g)[0])
        s = s_next

    kn = knew_ref[0]
    s = _bdot_nt(q, kn)
    qpos = lax.broadcasted_iota(jnp.int32, (rows, t_new), 0) & (t_new - 1)
    kpos = lax.broadcasted_iota(jnp.int32, (rows, t_new), 1)
    s = jnp.where(kpos <= qpos, s, NEG)
    _softmax_step(s, m_ref, l_ref, acc_ref, kn[:, 0:MLA_KV_RANK])
    _mla_out(acc_ref, l_ref, wuv_ref, o_ref.at[0], t_new)


def _mla_sample(page_table, q, knew, wuv_pair, cache_ckv, cache_krt, n_pg):
    db, n_pages = page_table.shape
    n_groups = n_pages // n_pg
    assert n_pages % n_pg == 0 and n_groups % SAMPLE_SLOTS == 0 and SAMPLE_AHEAD <= n_groups
    t_new = knew.shape[1]
    rows = MLA_HEADS * t_new
    grid_spec = pltpu.PrefetchScalarGridSpec(
        num_scalar_prefetch=1, grid=(db,),
        in_specs=[pl.BlockSpec((MLA_HEADS, 1, t_new, QK_WIDTH), lambda b, pt: (0, b, 0, 0)),
                  pl.BlockSpec((1, t_new, QK_WIDTH), lambda b, pt: (b, 0, 0)),
                  pl.BlockSpec(wuv_pair.shape, lambda b, pt: (0, 0, 0)),
                  pl.BlockSpec(memory_space=pl.ANY), pl.BlockSpec(memory_space=pl.ANY)],
        out_specs=pl.BlockSpec((1, t_new, MLA_HEADS * MLA_V), lambda b, pt: (b, 0, 0)),
        scratch_shapes=[pltpu.VMEM((SAMPLE_SLOTS, n_pg, PAGE, MLA_KV_RANK), F32),
                        pltpu.VMEM((SAMPLE_SLOTS, n_pg, MLA_ROPE, PAGE), F32),
                        pltpu.SemaphoreType.DMA((SAMPLE_SLOTS,)), pltpu.SemaphoreType.DMA((SAMPLE_SLOTS,)),
                        pltpu.VMEM((rows, 1), F32), pltpu.VMEM((rows, 1), F32),
                        pltpu.VMEM((rows, MLA_KV_RANK), F32)])
    return pl.pallas_call(
        functools.partial(_mla_sample_kernel, n_pg=n_pg, n_groups=n_groups, t_new=t_new),
        grid_spec=grid_spec,
        out_shape=jax.ShapeDtypeStruct((db, t_new, MLA_HEADS * MLA_V), BF16),
        compiler_params=_cparams(("arbitrary",)), name="mla_sample",
    )(page_table, q, knew, wuv_pair, cache_ckv, cache_krt)


def _gla_kernel(gq_ref, gk_ref, gv_ref, gl_ref, go_ref, s0_ref, gn_ref, tri_ref, o_ref, st_ref, *, c, sub, nc, bb):
    g_idx = pl.program_id(2)

    @pl.when(g_idx == 0)
    def _():
        st_ref[...] = s0_ref[...]

    tri = tri_ref[...]
    nsub = c // sub
    dk, dv = GLA_DK, GLA_DV
    lane = lax.broadcasted_iota(jnp.int32, (sub, 2 * dk), 1)
    col16 = lax.broadcasted_iota(jnp.int32, (sub, sub), 1)
    row16 = lax.broadcasted_iota(jnp.int32, (sub, sub), 0)
    eye = (lax.broadcasted_iota(jnp.int32, (dk, dk), 0) == lax.broadcasted_iota(jnp.int32, (dk, dk), 1))

    def chunk(bx, ci):
        r0 = ci * c
        q2 = gq_ref[bx, pl.ds(r0, c), :]
        k2 = gk_ref[bx, pl.ds(r0, c), :]
        g2 = gl_ref[bx, pl.ds(r0, c), :]
        b2 = _split_dot_left(tri, g2)
        blast2 = b2[c - 1:c, :]
        qe2 = q2 * jnp.exp(b2)
        kd2 = k2 * jnp.exp(blast2 - b2)
        diag = [[None] * nsub for _ in range(2)]
        for i in range(nsub):
            qi = q2[i * sub:(i + 1) * sub]
            ki = k2[i * sub:(i + 1) * sub]
            bi = b2[i * sub:(i + 1) * sub]
            a0 = jnp.zeros((sub, sub), F32)
            a1 = jnp.zeros((sub, sub), F32)
            for s in range(sub):
                x = qi * ki[s:s + 1] * jnp.exp(jnp.minimum(bi - bi[s:s + 1], 0.0))
                c0 = jnp.sum(jnp.where(lane < dk, x, 0.0), axis=1, keepdims=True)
                c1 = jnp.sum(jnp.where(lane >= dk, x, 0.0), axis=1, keepdims=True)
                a0 = jnp.where(col16 == s, c0, a0)
                a1 = jnp.where(col16 == s, c1, a1)
            diag[0][i] = jnp.where(row16 >= col16, a0, 0.0)
            diag[1][i] = jnp.where(row16 >= col16, a1, 0.0)
        for hh in range(2):
            ls = slice(hh * dk, (hh + 1) * dk)
            v = gv_ref[bx, pl.ds(r0, c), hh * dv:(hh + 1) * dv]
            vb = v.astype(BF16)
            st = st_ref[bx, hh]
            inter = _bdot(qe2[:, ls], st)
            b = b2[:, ls]
            outs = []
            for i in range(nsub):
                oi = _bdot(diag[hh][i], vb[i * sub:(i + 1) * sub])
                if i > 0:
                    ref = b[i * sub - 1:i * sub]
                    qi = q2[i * sub:(i + 1) * sub, ls] * jnp.exp(b[i * sub:(i + 1) * sub] - ref)
                    kj = k2[0:i * sub, ls] * jnp.exp(ref - b[0:i * sub])
                    oi = oi + _bdot(_bdot_nt(qi, kj), vb[0:i * sub])
                outs.append(oi)
            o = inter + (jnp.concatenate(outs, axis=0) if nsub > 1 else outs[0])
            a_row = jnp.exp(blast2[:, ls])
            a_col = jnp.sum(jnp.where(eye, a_row, 0.0), axis=1, keepdims=True)
            st_ref[bx, hh] = a_col * st + _bdot_tn(kd2[:, ls], vb)
            on = _rms(o, gn_ref[...])
            gate = go_ref[bx, pl.ds(r0, c), hh * dv:(hh + 1) * dv]
            o_ref[bx, pl.ds(r0, c), hh * dv:(hh + 1) * dv] = (on * gate * jax.nn.sigmoid(gate)).astype(o_ref.dtype)

    for bx in range(bb):
        for ci in range(nc):
            chunk(bx, ci)


def _split_dot_left(tri, x):
    hi = x.astype(BF16)
    r1 = x - hi.astype(F32)
    mid = r1.astype(BF16)
    lo = (r1 - mid.astype(F32)).astype(BF16)
    return (jnp.dot(tri, hi, preferred_element_type=F32)
            + jnp.dot(tri, mid, preferred_element_type=F32)
            + jnp.dot(tri, lo, preferred_element_type=F32))


def _gla(gq, gk, gv, glog, go, state0, gn, nc, bb):
    bsz, t, _ = gq.shape
    c = math.gcd(t, GLA_CHUNK)
    sub = min(GLA_SUB, c)
    n_groups = t // (c * nc)
    tri = jnp.tril(jnp.ones((c, c), F32)).astype(BF16)
    qk_spec = pl.BlockSpec((bb, c * nc, 2 * GLA_DK), lambda b, p, g: (b, g, p))
    v_spec = pl.BlockSpec((bb, c * nc, 2 * GLA_DV), lambda b, p, g: (b, g, p))
    st_spec = pl.BlockSpec((bb, 2, GLA_DK, GLA_DV), lambda b, p, g: (b, p, 0, 0))
    return pl.pallas_call(
        functools.partial(_gla_kernel, c=c, sub=sub, nc=nc, bb=bb),
        grid=(bsz // bb, GLA_HEADS // 2, n_groups),
        in_specs=[qk_spec, qk_spec, v_spec, qk_spec, v_spec, st_spec,
                  pl.BlockSpec((1, GLA_DV), lambda b, p, g: (0, 0)),
                  pl.BlockSpec((c, c), lambda b, p, g: (0, 0))],
        out_specs=(v_spec, st_spec),
        out_shape=(jax.ShapeDtypeStruct((bsz, t, GLA_HEADS * GLA_DV), BF16),
                   jax.ShapeDtypeStruct(state0.shape, F32)),
        compiler_params=_cparams(("parallel", "parallel", "arbitrary")), name="gla",
    )(gq, gk, gv, glog, go, state0, gn, tri)


def _mix_out_mem_kernel(mla_ref, gla_ref, h_ref, woa_ref, wob_ref, gm_ref, wq_ref, mk_ref, mv_ref, wo_ref,
                        o_ref, *, groups, r, interleaved):
    h1 = (h_ref[...] + jnp.dot(mla_ref[...], woa_ref[...], preferred_element_type=F32)
          + jnp.dot(gla_ref[...], wob_ref[...], preferred_element_type=F32))
    xn = _rms(h1, gm_ref[...]).astype(BF16)
    scale = MEM_HEAD_DIM ** -0.5
    if interleaved:
        q = jnp.dot(xn, wq_ref[...], preferred_element_type=F32)
        hr = MEM_HEADS * r
        s_parts = []
        for gi in range(groups):
            qg = jnp.concatenate([q[gi * r:(gi + 1) * r, hh * MEM_HEAD_DIM:(hh + 1) * MEM_HEAD_DIM]
                                  for hh in range(MEM_HEADS)], axis=0)
            s_parts.append(_bdot_nt(qg, mk_ref[gi]))
        s = (jnp.concatenate(s_parts, axis=0) if groups > 1 else s_parts[0]) * scale
        row = lax.broadcasted_iota(jnp.int32, s.shape, 0)
        col = lax.broadcasted_iota(jnp.int32, s.shape, 1)
        own = (col & (MEM_HEADS - 1)) == ((row >> (r.bit_length() - 1)) & (MEM_HEADS - 1))
        s = jnp.where(own, s, NEG)
        s = s - jnp.max(s, axis=-1, keepdims=True)
        p = jnp.exp(s)
        p = (p / jnp.sum(p, axis=-1, keepdims=True)).astype(BF16)
        outs = []
        for gi in range(groups):
            og = _bdot(p[gi * hr:(gi + 1) * hr], mv_ref[gi])
            outs.append(jnp.concatenate([og[hh * r:(hh + 1) * r] for hh in range(MEM_HEADS)], axis=1))
        o = jnp.concatenate(outs, axis=0) if groups > 1 else outs[0]
    else:
        q = jnp.dot(xn, wq_ref[...], preferred_element_type=F32).astype(BF16)
        heads = []
        for hh in range(MEM_HEADS):
            ls = slice(hh * MEM_HEAD_DIM, (hh + 1) * MEM_HEAD_DIM)
            s = _bdot_nt(q[:, ls], mk_ref[0, :, ls]) * scale
            s = s - jnp.max(s, axis=-1, keepdims=True)
            p = jnp.exp(s)
            p = p / jnp.sum(p, axis=-1, keepdims=True)
            heads.append(_bdot(p, mv_ref[0, :, ls]))
        o = jnp.concatenate(heads, axis=1)
    o_ref[...] = h1 + jnp.dot(o.astype(BF16), wo_ref[...], preferred_element_type=F32)


def _mix_out_mem(mla_o, gla_o, h, woa, wob, gm, wq, mk, mv, wo, groups, r):
    t = h.shape[0]
    tm = groups * r
    row = lambda w: pl.BlockSpec((tm, w), lambda i: (i, 0))
    interleaved = mk.shape[-1] == MEM_HEAD_DIM
    assert not interleaved or (r & (r - 1) == 0 and MEM_HEADS & (MEM_HEADS - 1) == 0)
    assert interleaved or groups == 1
    if interleaved:
        kv_spec = pl.BlockSpec((groups, MEM_TOKENS * MEM_HEADS, MEM_HEAD_DIM), lambda i: (i, 0, 0))
    else:
        kv_spec = pl.BlockSpec((1, MEM_TOKENS, MEM_WIDTH), lambda i: (0, 0, 0))
    return pl.pallas_call(
        functools.partial(_mix_out_mem_kernel, groups=groups, r=r, interleaved=interleaved),
        grid=(t // tm,),
        in_specs=[row(512), row(512), row(D_MODEL), _full(woa.shape), _full(wob.shape), _full(gm.shape),
                  _full(wq.shape), kv_spec, kv_spec, _full(wo.shape)],
        out_specs=row(D_MODEL),
        out_shape=jax.ShapeDtypeStruct((t, D_MODEL), F32),
        compiler_params=_cparams(("parallel",)), name="mix_out_mem",
    )(mla_o, gla_o, h, woa, wob, gm, wq, mk, mv, wo)


MOE_TB = 512
MOE_CHUNK = 16
MOE_TM = 256
MOE_CPT = MOE_TM // MOE_CHUNK
MOE_ROW_BLOCK = 512


def _moe_local_rows(tb):
    worst = TOP_K * tb + N_EXPERTS * (MOE_CHUNK - 1)
    return -(-worst // MOE_ROW_BLOCK) * MOE_ROW_BLOCK


def _route_kernel(hp_ref, hs_ref, g_ref, wrt_ref, brc_ref, u_ref, ltri_ref, xl_ref, pos_ref, gate_ref, cnt_ref,
                  *, rl, nbp):
    tb = hp_ref.shape[0]
    h = jnp.where(pl.program_id(0) < nbp, hp_ref[...], hs_ref[...])
    xn = _rms(h, g_ref[...])
    x_hi = xn.astype(BF16)
    x_lo = (xn - x_hi.astype(F32)).astype(BF16)
    w = wrt_ref[...]
    w_hi = w.astype(BF16)
    w_lo = (w - w_hi.astype(F32)).astype(BF16)
    logits = _bdot_nt(w_hi, x_hi) + _bdot_nt(w_lo, x_hi) + _bdot_nt(w_hi, x_lo) + brc_ref[...]
    e_iota = lax.broadcasted_iota(jnp.int32, (N_EXPERTS, tb), 0)
    work = logits
    sel = jnp.zeros((N_EXPERTS, tb), jnp.bool_)
    top = None
    for _ in range(TOP_K):
        m = jnp.max(work, axis=0, keepdims=True)
        if top is None:
            top = m
        idx = jnp.min(jnp.where(work == m, e_iota, N_EXPERTS), axis=0, keepdims=True)
        pick = e_iota == idx
        sel = jnp.logical_or(sel, pick)
        work = jnp.where(pick, -jnp.inf, work)
    ex = jnp.where(sel, jnp.exp(logits - top), 0.0)
    gates = ex / jnp.sum(ex, axis=0, keepdims=True)
    self32 = jnp.where(sel, 1.0, 0.0)
    prefix = jnp.dot(self32.astype(BF16), u_ref[...], preferred_element_type=F32)
    cnt = jnp.sum(self32, axis=1, keepdims=True)
    cnt_ref[0] = jnp.broadcast_to(cnt, (N_EXPERTS, 128))
    padded = jnp.floor((cnt + (MOE_CHUNK - 1)) * (1.0 / MOE_CHUNK)) * MOE_CHUNK
    off = jnp.dot(ltri_ref[...], jnp.broadcast_to(padded, (N_EXPERTS, 128)).astype(BF16),
                  preferred_element_type=F32)[:, 0:1]
    pos = off + prefix
    pending = sel
    pos_rows, gate_rows = [], []
    for _ in range(TOP_K):
        emin = jnp.min(jnp.where(pending, e_iota, N_EXPERTS), axis=0, keepdims=True)
        pick = e_iota == emin
        pos_rows.append(jnp.sum(jnp.where(pick, pos, 0.0), axis=0, keepdims=True))
        gate_rows.append(jnp.sum(jnp.where(pick, gates, 0.0), axis=0, keepdims=True))
        pending = jnp.logical_and(pending, jnp.logical_not(pick))
    zeros4 = jnp.zeros((8 - TOP_K, tb), F32)
    pos8 = jnp.concatenate(pos_rows + [zeros4 - 1.0], axis=0)
    pos_ref[0] = pos8.astype(jnp.int32)
    gate_ref[0] = jnp.concatenate(gate_rows + [zeros4], axis=0)
    pk = [p.astype(jnp.int32) for p in pos_rows]
    for rb in range(rl // MOE_ROW_BLOCK):
        r_iota = rb * MOE_ROW_BLOCK + lax.broadcasted_iota(jnp.int32, (MOE_ROW_BLOCK, tb), 0)
        hit = r_iota == pk[0]
        for k in range(1, TOP_K):
            hit = jnp.logical_or(hit, r_iota == pk[k])
        onehot = jnp.where(hit, 1.0, 0.0).astype(BF16)
        xl_ref[0, rb * MOE_ROW_BLOCK:(rb + 1) * MOE_ROW_BLOCK, :] = jnp.dot(
            onehot, x_hi, preferred_element_type=F32).astype(BF16)


def _moe_route(hp, hs, g, wrt, brc, tb, rl):
    nbp = hp.shape[0] // tb
    nb = nbp + hs.shape[0] // tb
    u = jnp.triu(jnp.ones((tb, tb), F32), 1).astype(BF16)
    ltri = jnp.tril(jnp.ones((N_EXPERTS, N_EXPERTS), F32), -1).astype(BF16)
    c2 = lambda a: pl.BlockSpec(a.shape, lambda i: (0, 0))
    return pl.pallas_call(
        functools.partial(_route_kernel, rl=rl, nbp=nbp),
        grid=(nb,),
        in_specs=[pl.BlockSpec((tb, D_MODEL), lambda i: (jnp.minimum(i, nbp - 1), 0)),
                  pl.BlockSpec((tb, D_MODEL), lambda i: (jnp.maximum(i - nbp, 0), 0)),
                  c2(g), c2(wrt), c2(brc), c2(u), c2(ltri)],
        out_specs=(pl.BlockSpec((1, rl, D_MODEL), lambda i: (i, 0, 0)),
                   pl.BlockSpec((1, 8, tb), lambda i: (i, 0, 0)),
                   pl.BlockSpec((1, 8, tb), lambda i: (i, 0, 0)),
                   pl.BlockSpec((1, N_EXPERTS, 128), lambda i: (i, 0, 0))),
        out_shape=(jax.ShapeDtypeStruct((nb, rl, D_MODEL), BF16),
                   jax.ShapeDtypeStruct((nb, 8, tb), jnp.int32),
                   jax.ShapeDtypeStruct((nb, 8, tb), F32),
                   jax.ShapeDtypeStruct((nb, N_EXPERTS, 128), F32)),
        compiler_params=_cparams(("parallel",)), name="moe_route",
    )(hp, hs, g, wrt, brc, u, ltri)


def _moe_tables(cnt, rl, n_tiles):
    nb = cnt.shape[0]
    cpl = rl // MOE_CHUNK
    nch = (cnt + (MOE_CHUNK - 1)) // MOE_CHUNK
    loc_off = jnp.cumsum(nch, axis=1) - nch
    seg_end = jnp.cumsum(nch, axis=0)
    seg_start = seg_end - nch
    tot = seg_end[-1]
    totp = (tot + (MOE_CPT - 1)) // MOE_CPT * MOE_CPT
    e_end = jnp.cumsum(totp)
    e_start = e_end - totp
    n_used = e_end[-1] // MOE_CPT
    slots = jnp.arange(n_tiles * MOE_CPT, dtype=jnp.int32)
    e_of = jnp.minimum(jnp.sum(slots[:, None] >= e_end[None, :], axis=1), N_EXPERTS - 1).astype(jnp.int32)
    o = slots - e_start[e_of]
    valid = jnp.logical_and(o < tot[e_of], slots < e_end[-1])
    b_of = jnp.minimum(jnp.sum(o[:, None] >= seg_end.T[e_of], axis=1), nb - 1).astype(jnp.int32)
    loc = loc_off[b_of, e_of] + (o - seg_start[b_of, e_of])
    real = b_of * cpl + loc
    spare = nb * cpl + (slots % (2 * MOE_CPT))
    src = jnp.where(valid, real, 0).astype(jnp.int32)
    dst = jnp.where(valid, real, spare).astype(jnp.int32)
    tile_slot = jnp.minimum(jnp.arange(n_tiles, dtype=jnp.int32), n_used - 1) * MOE_CPT
    te = e_of[tile_slot]
    return te, src, dst, n_used.astype(jnp.int32).reshape(1)


def _experts_kernel(te_ref, src_ref, dst_ref, nu_ref, xl_hbm, wup_ref, bup_ref, wdn_ref, bdn_ref, yinit_hbm,
                    yl_hbm, xbuf, ybuf, wub, wdb, isem, osem):
    del yinit_hbm
    i = pl.program_id(0)
    nu = nu_ref[0]
    slot = lax.rem(i, 2)

    def in_copy(tile, sl, c):
        return pltpu.make_async_copy(xl_hbm.at[src_ref[tile * MOE_CPT + c]],
                                     xbuf.at[sl, pl.ds(c * MOE_CHUNK, MOE_CHUNK), :], isem.at[sl])

    def out_copy(tile, sl, c):
        return pltpu.make_async_copy(ybuf.at[sl, pl.ds(c * MOE_CHUNK, MOE_CHUNK), :],
                                     yl_hbm.at[dst_ref[tile * MOE_CPT + c]], osem.at[sl])

    @pl.when(i == 0)
    def _():
        for c in range(MOE_CPT):
            in_copy(0, 0, c).start()

    @pl.when(i < nu)
    def _():
        @pl.when(i + 1 < nu)
        def _():
            for c in range(MOE_CPT):
                in_copy(i + 1, 1 - slot, c).start()

        for c in range(MOE_CPT):
            in_copy(i, slot, c).wait()

        @pl.when(i >= 2)
        def _():
            for c in range(MOE_CPT):
                out_copy(i - 2, slot, c).wait()

        @pl.when(jnp.logical_or(i == 0, te_ref[i] != te_ref[jnp.maximum(i - 1, 0)]))
        def _():
            wub[...] = wup_ref[0].astype(BF16)
            wdb[...] = wdn_ref[0].astype(BF16)

        x = xbuf[slot]
        hu = jnp.dot(x, wub[...], preferred_element_type=F32) + bup_ref[0]
        gate = jnp.minimum(hu[:, 0:D_FF], SWIGLU_LIMIT)
        up = jnp.clip(hu[:, D_FF:2 * D_FF], -SWIGLU_LIMIT, SWIGLU_LIMIT)
        act = (up + 1.0) * gate * jax.nn.sigmoid(SWIGLU_ALPHA * gate)
        y = jnp.dot(act.astype(BF16), wdb[...], preferred_element_type=F32) + bdn_ref[0]
        ybuf[slot] = y.astype(BF16)
        for c in range(MOE_CPT):
            out_copy(i, slot, c).start()

        @pl.when(i == nu - 1)
        def _():
            for c in range(MOE_CPT):
                out_copy(i, slot, c).wait()

            @pl.when(i >= 1)
            def _():
                for c in range(MOE_CPT):
                    out_copy(i - 1, 1 - slot, c).wait()


def _moe_experts(te, src, dst, n_used, xl, wup, bup, wdn, bdn, n_tiles):
    nb, rl, _ = xl.shape
    cpl = rl // MOE_CHUNK
    xl_chunks = xl.reshape(nb * cpl, MOE_CHUNK, D_MODEL)
    y_init = jnp.zeros(((nb + 1) * cpl, MOE_CHUNK, D_MODEL), BF16)
    wmap = lambda i, te, src, dst, nu: (te[i], 0, 0)
    grid_spec = pltpu.PrefetchScalarGridSpec(
        num_scalar_prefetch=4, grid=(n_tiles,),
        in_specs=[pl.BlockSpec(memory_space=pl.ANY),
                  pl.BlockSpec((1, D_MODEL, 2 * D_FF), wmap), pl.BlockSpec((1, 1, 2 * D_FF), wmap),
                  pl.BlockSpec((1, D_FF, D_MODEL), wmap), pl.BlockSpec((1, 1, D_MODEL), wmap),
                  pl.BlockSpec(memory_space=pl.ANY)],
        out_specs=pl.BlockSpec(memory_space=pl.ANY),
        scratch_shapes=[pltpu.VMEM((2, MOE_TM, D_MODEL), BF16), pltpu.VMEM((2, MOE_TM, D_MODEL), BF16),
                        pltpu.VMEM((D_MODEL, 2 * D_FF), BF16), pltpu.VMEM((D_FF, D_MODEL), BF16),
                        pltpu.SemaphoreType.DMA((2,)), pltpu.SemaphoreType.DMA((2,))])
    yl = pl.pallas_call(
        _experts_kernel, grid_spec=grid_spec,
        out_shape=jax.ShapeDtypeStruct(y_init.shape, BF16),
        input_output_aliases={9: 0},
        compiler_params=_cparams(("arbitrary",)), name="moe_experts",
    )(te, src, dst, n_used, xl_chunks, wup, bup, wdn, bdn, y_init)
    return yl.reshape(nb + 1, rl, D_MODEL)


def _combine_kernel(yl_ref, pos_ref, gate_ref, hp_ref, hs_ref, gf_ref, yp_ref, ys_ref, *, rl, nbp):
    tb = hp_ref.shape[0]
    i = pl.program_id(0)
    pos = pos_ref[0]
    gate = gate_ref[0]
    acc = jnp.where(i < nbp, hp_ref[...], hs_ref[...])
    for rb in range(rl // MOE_ROW_BLOCK):
        r_iota = rb * MOE_ROW_BLOCK + lax.broadcasted_iota(jnp.int32, (MOE_ROW_BLOCK, tb), 0)
        w = jnp.zeros((MOE_ROW_BLOCK, tb), F32)
        for k in range(TOP_K):
            w = jnp.where(r_iota == pos[k:k + 1], gate[k:k + 1], w)
        acc = acc + _bdot_tn(w, yl_ref[0, rb * MOE_ROW_BLOCK:(rb + 1) * MOE_ROW_BLOCK, :])
    y = _rms(acc, gf_ref[...])

    @pl.when(i < nbp)
    def _():
        yp_ref[...] = y

    @pl.when(i >= nbp)
    def _():
        ys_ref[...] = y


def _moe_combine(yl, pos, gate, hp, hs, gf, tb):
    rl = yl.shape[1]
    nbp = hp.shape[0] // tb
    nb = nbp + hs.shape[0] // tb
    p_map = lambda i: (jnp.minimum(i, nbp - 1), 0)
    s_map = lambda i: (jnp.maximum(i - nbp, 0), 0)
    return pl.pallas_call(
        functools.partial(_combine_kernel, rl=rl, nbp=nbp),
        grid=(nb,),
        in_specs=[pl.BlockSpec((1, rl, D_MODEL), lambda i: (i, 0, 0)),
                  pl.BlockSpec((1, 8, tb), lambda i: (i, 0, 0)), pl.BlockSpec((1, 8, tb), lambda i: (i, 0, 0)),
                  pl.BlockSpec((tb, D_MODEL), p_map), pl.BlockSpec((tb, D_MODEL), s_map),
                  pl.BlockSpec(gf.shape, lambda i: (0, 0))],
        out_specs=(pl.BlockSpec((tb, D_MODEL), p_map), pl.BlockSpec((tb, D_MODEL), s_map)),
        out_shape=(jax.ShapeDtypeStruct(hp.shape, F32), jax.ShapeDtypeStruct(hs.shape, F32)),
        compiler_params=_cparams(("arbitrary",)), name="moe_combine",
    )(yl, pos, gate, hp, hs, gf)


def _moe(hp, hs, g, wr, br, wup, bup, wdn, bdn, gf):
    tb = math.gcd(math.gcd(hp.shape[0], hs.shape[0]), MOE_TB)
    rl = _moe_local_rows(tb)
    nb = (hp.shape[0] + hs.shape[0]) // tb
    n_tiles = -(-(nb * (rl // MOE_CHUNK) + N_EXPERTS * (MOE_CPT - 1)) // MOE_CPT)
    xl, pos, gate, cnt = _moe_route(hp, hs, g, wr.T, br.reshape(N_EXPERTS, 1), tb, rl)
    te, src, dst, n_used = _moe_tables(cnt[:, :, 0].astype(jnp.int32), rl, n_tiles)
    yl = _moe_experts(te, src, dst, n_used, xl, wup, bup, wdn, bdn, n_tiles)
    return _moe_combine(yl, pos, gate, hp, hs, gf, tb)


def _rope_tables(pos):
    half = MLA_ROPE // 2
    inv = ROPE_THETA ** (-jnp.arange(half, dtype=F32) / half)
    ang = pos.astype(F32)[:, None] * inv[None, :]
    cos, sin = jnp.cos(ang), jnp.sin(ang)
    pad = jnp.zeros((pos.shape[0], 128 - MLA_ROPE), F32)
    return jnp.concatenate([cos, cos, pad], axis=1), jnp.concatenate([sin, sin, pad], axis=1)


def _rot_cols(w):
    half = w.shape[-1] // 2
    return jnp.concatenate([-w[..., half:], w[..., :half]], axis=-1)


def _pad_cols(w, width):
    return jnp.pad(w, ((0, 0), (0, width - w.shape[1])))


def _prep_mix_weights(norm_g, w_in, q_a_norm_g, w_uq, kv_a_norm_g, w_uk, gla_gate_w2, gla_gate_b):
    o = 0
    parts = {}
    for name, size in (("cq", MLA_Q_RANK), ("ckv", MLA_KV_RANK), ("kr", MLA_ROPE), ("gq", 256), ("gk", 256),
                       ("gv", 512), ("gg", GLA_GATE_RANK), ("go", 512)):
        parts[name] = w_in[:, o:o + size]
        o += size
    w1 = jnp.concatenate([parts["cq"], parts["ckv"], parts["gq"] * (GLA_DK ** -0.5), parts["gk"], parts["gv"],
                          parts["go"]], axis=1).astype(BF16)
    w2 = jnp.concatenate([_pad_cols(parts["kr"], 128), _pad_cols(_rot_cols(parts["kr"]), 128),
                          _pad_cols(parts["gg"], 128)], axis=1).astype(BF16)
    wuq = w_uq.reshape(MLA_Q_RANK, MLA_HEADS, MLA_NOPE + MLA_ROPE)
    wuq_n = wuq[:, :, :MLA_NOPE].reshape(MLA_Q_RANK, MLA_HEADS * MLA_NOPE).astype(BF16)
    wr = wuq[:, :, MLA_NOPE:]
    widen = lambda w: jnp.pad(w, ((0, 0), (0, 0), (0, 128 - MLA_ROPE))).reshape(MLA_Q_RANK, MLA_HEADS * 128)
    wuq_r = widen(wr).astype(BF16)
    wuq_rr = widen(_rot_cols(wr)).astype(BF16)
    wukt = jnp.transpose(w_uk, (1, 2, 0)).astype(BF16)
    wg2 = jnp.pad(gla_gate_w2, ((0, 128 - GLA_GATE_RANK), (0, 0))).astype(BF16)
    return (norm_g[None], w1, w2, q_a_norm_g[None], wuq_n, wuq_r, wuq_rr, kv_a_norm_g[None], wukt, wg2,
            gla_gate_b[None])


def _prep_wuv(w_uv):
    w = jnp.transpose(w_uv, (1, 0, 2))
    z = jnp.zeros_like(w)
    even = jnp.concatenate([w, z], axis=-1)
    odd = jnp.concatenate([z, w], axis=-1)
    sel = (jnp.arange(MLA_HEADS) % 2 == 0)[:, None, None]
    return jnp.where(sel, even, odd).astype(BF16)


def kernel(x_prompt, x_sample, mem_prompt, cache_ckv, cache_krope, state_gla, cache_mem_k, cache_mem_v,
           page_table, norm_mix_g, w_in, q_a_norm_g, w_uq, kv_a_norm_g, w_uk, w_uv, gla_gate_w2, gla_gate_b,
           gla_norm_g, w_out, norm_mem_g, mem_in_norm_g, w_mem_q, w_mem_k, w_mem_v, w_mem_o, norm_moe_g,
           w_router, b_router, w_moe_up, b_moe_up, w_moe_down, b_moe_down, norm_final_g):
    depth = w_in.shape[0]
    assert depth == 1 and x_prompt.shape[0] == 1
    l = 0
    b, t, _ = x_prompt.shape
    db, ts, _ = x_sample.shape
    n_pages = page_table.shape[1]
    past = n_pages * PAGE

    mixw = _prep_mix_weights(norm_mix_g[l], w_in[l], q_a_norm_g[l], w_uq[l], kv_a_norm_g[l], w_uk[l],
                             gla_gate_w2[l], gla_gate_b[l])
    wuv_pair = _prep_wuv(w_uv[l])
    woa = w_out[l][:MLA_HEADS * MLA_V].astype(BF16)
    wob = w_out[l][MLA_HEADS * MLA_V:].astype(BF16)
    gn = gla_norm_g[l][None]
    gm = norm_mem_g[l][None]
    wmq = w_mem_q[l].astype(BF16)
    wmo = w_mem_o[l].astype(BF16)
    wup = w_moe_up[l]
    wdn = w_moe_down[l]
    bup = b_moe_up[l][:, None, :]
    bdn = b_moe_down[l][:, None, :]
    moe_args = (norm_moe_g[l][None], w_router[l], b_router[l], wup, bup, wdn, bdn, norm_final_g[None])

    hp = x_prompt.reshape(t, D_MODEL)
    cs_p, sn_p = _rope_tables(jnp.arange(t))
    tm_p = min(512, t)
    q_p, ckv_p, kr_p, kcat_p, gq, gk, gv, gl, go = _mix_in(hp, cs_p, sn_p, mixw, tm_p)
    tq = min(256, t)
    tk = min(512, t)
    mla_p = _mla_prompt(q_p, kcat_p, wuv_pair, tq, tk)
    zero_state = jnp.zeros((1, GLA_HEADS, GLA_DK, GLA_DV), F32)
    nc_p = max(1, min(4, t // GLA_CHUNK))
    gla_p, st_p = _gla(gq[None], gk[None], gv[None], gl[None], go[None], zero_state, gn, nc_p, 1)
    mk_p, mv_p = _mem_kv(mem_prompt[0], mem_in_norm_g[l][None], w_mem_k[l].astype(BF16), w_mem_v[l].astype(BF16))
    hp = _mix_out_mem(mla_p, gla_p[0], hp, woa, wob, gm, wmq, mk_p[None], mv_p[None], wmo, 1, tm_p)

    n_s = db * ts
    hs = x_sample.reshape(n_s, D_MODEL)
    cs_s, sn_s = _rope_tables(past + jnp.arange(ts))
    cs_s = jnp.tile(cs_s, (db, 1))
    sn_s = jnp.tile(sn_s, (db, 1))
    tm_s = min(512, n_s)
    q_s, ckv_s, kr_s, kcat_s, gq, gk, gv, gl, go = _mix_in(hs, cs_s, sn_s, mixw, tm_s)
    n_pg = max(1, n_pages // 8)
    mla_s = _mla_sample(page_table, q_s.reshape(MLA_HEADS, db, ts, QK_WIDTH), kcat_s.reshape(db, ts, QK_WIDTH),
                        wuv_pair, cache_ckv[l], jnp.swapaxes(cache_krope[l], 1, 2), n_pg)
    r3 = lambda a: a.reshape(db, ts, a.shape[-1])
    gla_s, st_s = _gla(r3(gq), r3(gk), r3(gv), r3(gl), r3(go), state_gla[l], gn, 1, math.gcd(db, 8))
    groups = min(8, db)
    hs = _mix_out_mem(mla_s.reshape(n_s, -1), gla_s.reshape(n_s, -1), hs, woa, wob, gm, wmq,
                      cache_mem_k[l].reshape(db, MEM_TOKENS * MEM_HEADS, MEM_HEAD_DIM),
                      cache_mem_v[l].reshape(db, MEM_TOKENS * MEM_HEADS, MEM_HEAD_DIM), wmo, groups, ts)
    y_p, y_s = _moe(hp, hs, *moe_args)

    return (y_p.reshape(b, t, D_MODEL), y_s.reshape(db, ts, D_MODEL),
            ckv_p.reshape(1, b, t, MLA_KV_RANK), kr_p.reshape(1, b, t, MLA_ROPE),
            st_p.reshape(1, b, GLA_HEADS, GLA_DK, GLA_DV),
            mk_p.reshape(1, b, MEM_TOKENS, MEM_HEADS, MEM_HEAD_DIM),
            mv_p.reshape(1, b, MEM_TOKENS, MEM_HEADS, MEM_HEAD_DIM),
            ckv_s.reshape(1, db, ts, MLA_KV_RANK), kr_s.reshape(1, db, ts, MLA_ROPE),
            st_s.reshape(1, db, GLA_HEADS, GLA_DK, GLA_DV))
```

```python
import functools
import math

import jax
import jax.numpy as jnp
from jax import lax
from jax.experimental import pallas as pl
from jax.experimental.pallas import tpu as pltpu

F32 = jnp.float32
BF16 = jnp.bfloat16

D_MODEL = 1024
MLA_HEADS = 8
MLA_NOPE = 64
MLA_ROPE = 32
MLA_V = 64
MLA_Q_RANK = 384
MLA_KV_RANK = 256
MLA_SCALE = (MLA_NOPE + MLA_ROPE) ** -0.5
MLA_QSCALE = MLA_SCALE * math.log2(math.e)
ROPE_THETA = 10000.0
QK_WIDTH = 384
GLA_HEADS = 4
GLA_DK = 64
GLA_DV = 128
GLA_GATE_RANK = 16
GLA_GATE_NORMALIZER = 16.0
GLA_CHUNK = 64
GLA_SUB = 16
MEM_TOKENS = 256
MEM_HEADS = 4
MEM_HEAD_DIM = 128
MEM_WIDTH = MEM_HEADS * MEM_HEAD_DIM
N_EXPERTS = 32
TOP_K = 4
D_FF = 1024
SWIGLU_LIMIT = 7.0
SWIGLU_ALPHA = 1.702
EPS = 1e-6
PAGE = 128
NEG = -1e30
SAMPLE_SLOTS = 8
SAMPLE_AHEAD = 4

VMEM_LIMIT = 56 * 1024 * 1024


def _cparams(sem):
    return pltpu.CompilerParams(dimension_semantics=sem, vmem_limit_bytes=VMEM_LIMIT)


def _rms(x, g):
    var = jnp.mean(x * x, axis=-1, keepdims=True)
    return x * lax.rsqrt(var + EPS) * g


def _bdot(a, b):
    return jnp.dot(a.astype(BF16), b.astype(BF16), preferred_element_type=F32)


def _bdot_nt(a, b):
    return lax.dot_general(a.astype(BF16), b.astype(BF16), (((1,), (1,)), ((), ())),
                           preferred_element_type=F32)


def _bdot_tn(a, b):
    return lax.dot_general(a.astype(BF16), b.astype(BF16), (((0,), (0,)), ((), ())),
                           preferred_element_type=F32)


def _split_dot(a, b_exact):
    hi = a.astype(BF16)
    r1 = a - hi.astype(F32)
    mid = r1.astype(BF16)
    lo = (r1 - mid.astype(F32)).astype(BF16)
    return (jnp.dot(hi, b_exact, preferred_element_type=F32)
            + jnp.dot(mid, b_exact, preferred_element_type=F32)
            + jnp.dot(lo, b_exact, preferred_element_type=F32))


def _full(shape):
    n = len(shape)
    return pl.BlockSpec(shape, lambda *_: (0,) * n)


def _mem_kv_kernel(mem_ref, g_ref, wk_ref, wv_ref, k_ref, v_ref):
    mn = _rms(mem_ref[...], g_ref[...]).astype(BF16)
    k_ref[...] = jnp.dot(mn, wk_ref[...], preferred_element_type=F32)
    v_ref[...] = jnp.dot(mn, wv_ref[...], preferred_element_type=F32)


def _mem_kv(mem, g, wk, wv):
    m = mem.shape[0]
    return pl.pallas_call(
        _mem_kv_kernel,
        out_shape=(jax.ShapeDtypeStruct((m, MEM_WIDTH), F32), jax.ShapeDtypeStruct((m, MEM_WIDTH), F32)),
        name="mem_kv",
    )(mem, g, wk, wv)


_C_CQ, _C_CKV, _C_GQ, _C_GK, _C_GV, _C_GO, _C_END = 0, 384, 640, 896, 1152, 1664, 2176


def _log_sigmoid(x):
    return jnp.minimum(x, 0.0) - jnp.log1p(jnp.exp(-jnp.abs(x)))


def _mix_in_kernel(h_ref, cs_ref, sn_ref, g_ref, w1_ref, w2_ref, qag_ref, wuqn_ref, wuqr_ref, wuqrr_ref,
                   kvag_ref, wukt_ref, wg2_ref, bg_ref,
                   q_ref, ckv_ref, kr_ref, kcat_ref, gq_ref, gk_ref, gv_ref, glog_ref, go_ref):
    xb = _rms(h_ref[...], g_ref[...]).astype(BF16)
    z1 = jnp.dot(xb, w1_ref[...], preferred_element_type=F32)
    z2 = jnp.dot(xb, w2_ref[...], preferred_element_type=F32)
    cs = cs_ref[...]
    sn = sn_ref[...]
    ckv = _rms(z1[:, _C_CKV:_C_GQ], kvag_ref[...])
    ckv_ref[...] = ckv
    krp = z2[:, 0:128] * cs + z2[:, 128:256] * sn
    kr_ref[...] = krp[:, 0:MLA_ROPE]
    kcat_ref[:, 0:MLA_KV_RANK] = ckv.astype(BF16)
    kcat_ref[:, MLA_KV_RANK:QK_WIDTH] = krp.astype(BF16)
    cqn = _rms(z1[:, _C_CQ:_C_CKV], qag_ref[...]).astype(BF16)
    qn = jnp.dot(cqn, wuqn_ref[...], preferred_element_type=F32)
    csw = jnp.concatenate([cs] * MLA_HEADS, axis=1)
    snw = jnp.concatenate([sn] * MLA_HEADS, axis=1)
    qr = (jnp.dot(cqn, wuqr_ref[...], preferred_element_type=F32) * csw
          + jnp.dot(cqn, wuqrr_ref[...], preferred_element_type=F32) * snw)
    for h in range(MLA_HEADS):
        ql = _bdot(qn[:, h * MLA_NOPE:(h + 1) * MLA_NOPE], wukt_ref[h])
        q_ref[h, :, 0:MLA_KV_RANK] = (ql * MLA_QSCALE).astype(BF16)
        q_ref[h, :, MLA_KV_RANK:QK_WIDTH] = (qr[:, h * 128:(h + 1) * 128] * MLA_QSCALE).astype(BF16)
    gq_ref[...] = z1[:, _C_GQ:_C_GK]
    gk_ref[...] = z1[:, _C_GK:_C_GV]
    gv_ref[...] = z1[:, _C_GV:_C_GO]
    go_ref[...] = z1[:, _C_GO:_C_END]
    gate = _bdot(z2[:, 256:384], wg2_ref[...]) + bg_ref[...]
    glog_ref[...] = _log_sigmoid(gate) * (1.0 / GLA_GATE_NORMALIZER)


def _mix_in(h, cs, sn, wts, tm):
    t = h.shape[0]
    grid = (t // tm,)
    row = lambda w: pl.BlockSpec((tm, w), lambda i: (i, 0))
    in_specs = [row(D_MODEL), row(128), row(128)] + [_full(w.shape) for w in wts]
    out_shape = (
        jax.ShapeDtypeStruct((MLA_HEADS, t, QK_WIDTH), BF16),
        jax.ShapeDtypeStruct((t, MLA_KV_RANK), F32),
        jax.ShapeDtypeStruct((t, MLA_ROPE), F32),
        jax.ShapeDtypeStruct((t, QK_WIDTH), BF16),
        jax.ShapeDtypeStruct((t, 256), F32),
        jax.ShapeDtypeStruct((t, 256), F32),
        jax.ShapeDtypeStruct((t, 512), F32),
        jax.ShapeDtypeStruct((t, 256), F32),
        jax.ShapeDtypeStruct((t, 512), F32),
    )
    out_specs = (
        pl.BlockSpec((MLA_HEADS, tm, QK_WIDTH), lambda i: (0, i, 0)),
        row(MLA_KV_RANK), row(MLA_ROPE), row(QK_WIDTH), row(256), row(256), row(512), row(256), row(512),
    )
    return pl.pallas_call(
        _mix_in_kernel, grid=grid, in_specs=in_specs, out_specs=out_specs, out_shape=out_shape,
        compiler_params=_cparams(("parallel",)), name="mix_in",
    )(h, cs, sn, *wts)


def _softmax_step(s, m_ref, l_ref, acc_ref, v):
    m_old = m_ref[...]
    m_new = jnp.maximum(m_old, jnp.max(s, axis=-1, keepdims=True))
    alpha = jnp.exp2(m_old - m_new)
    p = jnp.exp2(s - m_new)
    l_ref[...] = alpha * l_ref[...] + jnp.sum(p, axis=-1, keepdims=True)
    acc_ref[...] = alpha * acc_ref[...] + jnp.dot(p.astype(BF16), v, preferred_element_type=F32)
    m_ref[...] = m_new


def _mla_out(acc_ref, l_ref, wuv_ref, o_ref, rows):
    o = acc_ref[...] / l_ref[...]
    for p in range(MLA_HEADS // 2):
        a = o[(2 * p) * rows:(2 * p + 1) * rows].astype(BF16)
        b = o[(2 * p + 1) * rows:(2 * p + 2) * rows].astype(BF16)
        y = (jnp.dot(a, wuv_ref[2 * p], preferred_element_type=F32)
             + jnp.dot(b, wuv_ref[2 * p + 1], preferred_element_type=F32))
        o_ref[:, p * 128:(p + 1) * 128] = y.astype(o_ref.dtype)


def _mla_prompt_kernel(q_ref, k_ref, wuv_ref, o_ref, s_ref, m_ref, l_ref, acc_ref, *, tq, tk):
    i = pl.program_id(0)
    rows = MLA_HEADS * tq
    q = q_ref[...].reshape(rows, QK_WIDTH)
    m_ref[...] = jnp.full(m_ref.shape, NEG, F32)
    l_ref[...] = jnp.zeros(l_ref.shape, F32)
    acc_ref[...] = jnp.zeros(acc_ref.shape, F32)
    q0 = i * tq
    n = q0 // tk + 1

    def kblk(j):
        return k_ref[pl.ds(pl.multiple_of(j * tk, tk), tk), :]

    def scores(j, slot):
        s_ref[slot] = _bdot_nt(q, kblk(j))

    def update(j, slot, masked):
        s = s_ref[slot]
        if masked:
            qpos = q0 + (lax.broadcasted_iota(jnp.int32, (rows, tk), 0) & (tq - 1))
            kpos = j * tk + lax.broadcasted_iota(jnp.int32, (rows, tk), 1)
            s = jnp.where(kpos <= qpos, s, NEG)
        _softmax_step(s, m_ref, l_ref, acc_ref, kblk(j)[:, 0:MLA_KV_RANK])

    scores(0, 0)
    n_pair = (n - 1) // 2

    def pair(jj, c):
        j = 2 * jj
        scores(j + 1, 1)
        update(j, 0, False)
        scores(j + 2, 0)
        update(j + 1, 1, False)
        return c

    lax.fori_loop(0, n_pair, pair, 0)
    left = (n - 1) - 2 * n_pair

    @pl.when(left == 0)
    def _():
        update(n - 1, 0, True)

    @pl.when(left == 1)
    def _():
        scores(n - 1, 1)
        update(n - 2, 0, False)
        update(n - 1, 1, True)

    _mla_out(acc_ref, l_ref, wuv_ref, o_ref, tq)


def _mla_prompt(q, kcat, wuv_pair, tq, tk):
    t = kcat.shape[0]
    assert tk % tq == 0 and t % tk == 0
    rows = MLA_HEADS * tq
    return pl.pallas_call(
        functools.partial(_mla_prompt_kernel, tq=tq, tk=tk),
        grid=(t // tq,),
        in_specs=[pl.BlockSpec((MLA_HEADS, tq, QK_WIDTH), lambda i: (0, i, 0)),
                  pl.BlockSpec(kcat.shape, lambda i: (0, 0), pipeline_mode=pl.Buffered(1)),
                  pl.BlockSpec(wuv_pair.shape, lambda i: (0, 0, 0), pipeline_mode=pl.Buffered(1))],
        out_specs=pl.BlockSpec((tq, MLA_HEADS * MLA_V), lambda i: (i, 0)),
        out_shape=jax.ShapeDtypeStruct((t, MLA_HEADS * MLA_V), BF16),
        scratch_shapes=[pltpu.VMEM((2, rows, tk), F32),
                        pltpu.VMEM((rows, 1), F32), pltpu.VMEM((rows, 1), F32),
                        pltpu.VMEM((rows, MLA_KV_RANK), F32)],
        compiler_params=_cparams(("arbitrary",)), name="mla_prompt",
    )(q, kcat, wuv_pair)


def _mla_sample_kernel(pt_ref, q_ref, knew_ref, wuv_ref, ckv_hbm, krt_hbm, o_ref,
                       ckv_buf, krt_buf, sem_c, sem_r, m_ref, l_ref, acc_ref, *, n_pg, n_groups, t_new):
    b = pl.program_id(0)
    nb = pl.num_programs(0)
    rows = MLA_HEADS * t_new
    q = q_ref[...].astype(F32).reshape(rows, QK_WIDTH).astype(BF16)

    def page_copies(bb, g, lookup):
        slot = g % SAMPLE_SLOTS
        out = []
        for i in range(n_pg):
            page = pt_ref[bb, g * n_pg + i] if lookup else 0
            out.append(pltpu.make_async_copy(ckv_hbm.at[page], ckv_buf.at[slot, i], sem_c.at[slot]))
            out.append(pltpu.make_async_copy(krt_hbm.at[page], krt_buf.at[slot, i], sem_r.at[slot]))
        return out

    def start_group(g):
        if g < n_groups:
            for c in page_copies(b, g, True):
                c.start()
        else:
            @pl.when(b + 1 < nb)
            def _():
                for c in page_copies(b + 1, g - n_groups, True):
                    c.start()

    def keys(g):
        slot = g % SAMPLE_SLOTS
        kv = ckv_buf[slot].reshape(n_pg * PAGE, MLA_KV_RANK).astype(BF16)
        krt = jnp.concatenate([krt_buf[slot, i] for i in range(n_pg)], axis=1).astype(BF16)
        return kv, krt

    def scores(g):
        for c in page_copies(b, g, False):
            c.wait()
        kv, krt = keys(g)
        return (_bdot_nt(q[:, 0:MLA_KV_RANK], kv)
                + jnp.dot(q[:, MLA_KV_RANK:MLA_KV_RANK + MLA_ROPE], krt, preferred_element_type=F32))

    @pl.when(b == 0)
    def _():
        for g in range(SAMPLE_AHEAD):
            for c in page_copies(0, g, True):
                c.start()

    m_ref[...] = jnp.full(m_ref.shape, NEG, F32)
    l_ref[...] = jnp.zeros(l_ref.shape, F32)
    acc_ref[...] = jnp.zeros(acc_ref.shape, F32)
    s = scores(0)
    for g in range(n_groups):
        start_group(g + SAMPLE_AHEAD)
        s_next = scores(g + 1) if g + 1 < n_groups else None
        _softmax_step(s, m_ref, l_ref, acc_ref, keys(g)[0])
        s = s_next

    kn = knew_ref[0]
    s = _bdot_nt(q, kn)
    qpos = lax.broadcasted_iota(jnp.int32, (rows, t_new), 0) & (t_new - 1)
    kpos = lax.broadcasted_iota(jnp.int32, (rows, t_new), 1)
    s = jnp.where(kpos <= qpos, s, NEG)
    _softmax_step(s, m_ref, l_ref, acc_ref, kn[:, 0:MLA_KV_RANK])
    _mla_out(acc_ref, l_ref, wuv_ref, o_ref.at[0], t_new)


def _mla_sample(page_table, q, knew, wuv_pair, cache_ckv, cache_krt, n_pg):
    db, n_pages = page_table.shape
    n_groups = n_pages // n_pg
    assert n_pages % n_pg == 0 and n_groups % SAMPLE_SLOTS == 0 and SAMPLE_AHEAD <= n_groups
    t_new = knew.shape[1]
    rows = MLA_HEADS * t_new
    grid_spec = pltpu.PrefetchScalarGridSpec(
        num_scalar_prefetch=1, grid=(db,),
        in_specs=[pl.BlockSpec((MLA_HEADS, 1, t_new, QK_WIDTH), lambda b, pt: (0, b, 0, 0)),
                  pl.BlockSpec((1, t_new, QK_WIDTH), lambda b, pt: (b, 0, 0)),
                  pl.BlockSpec(wuv_pair.shape, lambda b, pt: (0, 0, 0)),
                  pl.BlockSpec(memory_space=pl.ANY), pl.BlockSpec(memory_space=pl.ANY)],
        out_specs=pl.BlockSpec((1, t_new, MLA_HEADS * MLA_V), lambda b, pt: (b, 0, 0)),
        scratch_shapes=[pltpu.VMEM((SAMPLE_SLOTS, n_pg, PAGE, MLA_KV_RANK), F32),
                        pltpu.VMEM((SAMPLE_SLOTS, n_pg, MLA_ROPE, PAGE), F32),
                        pltpu.SemaphoreType.DMA((SAMPLE_SLOTS,)), pltpu.SemaphoreType.DMA((SAMPLE_SLOTS,)),
                        pltpu.VMEM((rows, 1), F32), pltpu.VMEM((rows, 1), F32),
                        pltpu.VMEM((rows, MLA_KV_RANK), F32)])
    return pl.pallas_call(
        functools.partial(_mla_sample_kernel, n_pg=n_pg, n_groups=n_groups, t_new=t_new),
        grid_spec=grid_spec,
        out_shape=jax.ShapeDtypeStruct((db, t_new, MLA_HEADS * MLA_V), BF16),
        compiler_params=_cparams(("arbitrary",)), name="mla_sample",
    )(page_table, q, knew, wuv_pair, cache_ckv, cache_krt)


def _gla_kernel(gq_ref, gk_ref, gv_ref, gl_ref, go_ref, s0_ref, gn_ref, tri_ref, o_ref, st_ref, *, c, sub, nc, bb):
    g_idx = pl.program_id(2)

    @pl.when(g_idx == 0)
    def _():
        st_ref[...] = s0_ref[...]

    tri = tri_ref[...]
    nsub = c // sub
    dk, dv = GLA_DK, GLA_DV
    lane = lax.broadcasted_iota(jnp.int32, (sub, 2 * dk), 1)
    col16 = lax.broadcasted_iota(jnp.int32, (sub, sub), 1)
    row16 = lax.broadcasted_iota(jnp.int32, (sub, sub), 0)
    eye = (lax.broadcasted_iota(jnp.int32, (dk, dk), 0) == lax.broadcasted_iota(jnp.int32, (dk, dk), 1))

    def chunk(bx, ci):
        r0 = ci * c
        q2 = gq_ref[bx, pl.ds(r0, c), :]
        k2 = gk_ref[bx, pl.ds(r0, c), :]
        g2 = gl_ref[bx, pl.ds(r0, c), :]
        b2 = _split_dot_left(tri, g2)
        blast2 = b2[c - 1:c, :]
        qe2 = q2 * jnp.exp(b2)
        kd2 = k2 * jnp.exp(blast2 - b2)
        diag = [[None] * nsub for _ in range(2)]
        for i in range(nsub):
            qi = q2[i * sub:(i + 1) * sub]
            ki = k2[i * sub:(i + 1) * sub]
            bi = b2[i * sub:(i + 1) * sub]
            a0 = jnp.zeros((sub, sub), F32)
            a1 = jnp.zeros((sub, sub), F32)
            for s in range(sub):
                x = qi * ki[s:s + 1] * jnp.exp(jnp.minimum(bi - bi[s:s + 1], 0.0))
                c0 = jnp.sum(jnp.where(lane < dk, x, 0.0), axis=1, keepdims=True)
                c1 = jnp.sum(jnp.where(lane >= dk, x, 0.0), axis=1, keepdims=True)
                a0 = jnp.where(col16 == s, c0, a0)
                a1 = jnp.where(col16 == s, c1, a1)
            diag[0][i] = jnp.where(row16 >= col16, a0, 0.0)
            diag[1][i] = jnp.where(row16 >= col16, a1, 0.0)
        for hh in range(2):
            ls = slice(hh * dk, (hh + 1) * dk)
            v = gv_ref[bx, pl.ds(r0, c), hh * dv:(hh + 1) * dv]
            vb = v.astype(BF16)
            st = st_ref[bx, hh]
            inter = _bdot(qe2[:, ls], st)
            b = b2[:, ls]
            outs = []
            for i in range(nsub):
                oi = _bdot(diag[hh][i], vb[i * sub:(i + 1) * sub])
                if i > 0:
                    ref = b[i * sub - 1:i * sub]
                    qi = q2[i * sub:(i + 1) * sub, ls] * jnp.exp(b[i * sub:(i + 1) * sub] - ref)
                    kj = k2[0:i * sub, ls] * jnp.exp(ref - b[0:i * sub])
                    oi = oi + _bdot(_bdot_nt(qi, kj), vb[0:i * sub])
                outs.append(oi)
            o = inter + (jnp.concatenate(outs, axis=0) if nsub > 1 else outs[0])
            a_row = jnp.exp(blast2[:, ls])
            a_col = jnp.sum(jnp.where(eye, a_row, 0.0), axis=1, keepdims=True)
            st_ref[bx, hh] = a_col * st + _bdot_tn(kd2[:, ls], vb)
            on = _rms(o, gn_ref[...])
            gate = go_ref[bx, pl.ds(r0, c), hh * dv:(hh + 1) * dv]
            o_ref[bx, pl.ds(r0, c), hh * dv:(hh + 1) * dv] = (on * gate * jax.nn.sigmoid(gate)).astype(o_ref.dtype)

    for bx in range(bb):
        for ci in range(nc):
            chunk(bx, ci)


def _split_dot_left(tri, x):
    hi = x.astype(BF16)
    r1 = x - hi.astype(F32)
    mid = r1.astype(BF16)
    lo = (r1 - mid.astype(F32)).astype(BF16)
    return (jnp.dot(tri, hi, preferred_element_type=F32)
            + jnp.dot(tri, mid, preferred_element_type=F32)
            + jnp.dot(tri, lo, preferred_element_type=F32))


def _gla(gq, gk, gv, glog, go, state0, gn, nc, bb):
    bsz, t, _ = gq.shape
    c = math.gcd(t, GLA_CHUNK)
    sub = min(GLA_SUB, c)
    n_groups = t // (c * nc)
    tri = jnp.tril(jnp.ones((c, c), F32)).astype(BF16)
    qk_spec = pl.BlockSpec((bb, c * nc, 2 * GLA_DK), lambda b, p, g: (b, g, p))
    v_spec = pl.BlockSpec((bb, c * nc, 2 * GLA_DV), lambda b, p, g: (b, g, p))
    st_spec = pl.BlockSpec((bb, 2, GLA_DK, GLA_DV), lambda b, p, g: (b, p, 0, 0))
    return pl.pallas_call(
        functools.partial(_gla_kernel, c=c, sub=sub, nc=nc, bb=bb),
        grid=(bsz // bb, GLA_HEADS // 2, n_groups),
        in_specs=[qk_spec, qk_spec, v_spec, qk_spec, v_spec, st_spec,
                  pl.BlockSpec((1, GLA_DV), lambda b, p, g: (0, 0)),
                  pl.BlockSpec((c, c), lambda b, p, g: (0, 0))],
        out_specs=(v_spec, st_spec),
        out_shape=(jax.ShapeDtypeStruct((bsz, t, GLA_HEADS * GLA_DV), BF16),
                   jax.ShapeDtypeStruct(state0.shape, F32)),
        compiler_params=_cparams(("parallel", "parallel", "arbitrary")), name="gla",
    )(gq, gk, gv, glog, go, state0, gn, tri)


def _mix_out_mem_kernel(mla_ref, gla_ref, h_ref, woa_ref, wob_ref, gm_ref, wq_ref, mk_ref, mv_ref, wo_ref,
                        o_ref, *, groups, r, interleaved):
    h1 = (h_ref[...] + jnp.dot(mla_ref[...], woa_ref[...], preferred_element_type=F32)
          + jnp.dot(gla_ref[...], wob_ref[...], preferred_element_type=F32))
    xn = _rms(h1, gm_ref[...]).astype(BF16)
    scale = MEM_HEAD_DIM ** -0.5
    if interleaved:
        q = jnp.dot(xn, wq_ref[...], preferred_element_type=F32)
        hr = MEM_HEADS * r
        s_parts = []
        for gi in range(groups):
            qg = jnp.concatenate([q[gi * r:(gi + 1) * r, hh * MEM_HEAD_DIM:(hh + 1) * MEM_HEAD_DIM]
                                  for hh in range(MEM_HEADS)], axis=0)
            s_parts.append(_bdot_nt(qg, mk_ref[gi]))
        s = (jnp.concatenate(s_parts, axis=0) if groups > 1 else s_parts[0]) * scale
        row = lax.broadcasted_iota(jnp.int32, s.shape, 0)
        col = lax.broadcasted_iota(jnp.int32, s.shape, 1)
        own = (col & (MEM_HEADS - 1)) == ((row >> (r.bit_length() - 1)) & (MEM_HEADS - 1))
        s = jnp.where(own, s, NEG)
        s = s - jnp.max(s, axis=-1, keepdims=True)
        p = jnp.exp(s)
        p = (p / jnp.sum(p, axis=-1, keepdims=True)).astype(BF16)
        outs = []
        for gi in range(groups):
            og = _bdot(p[gi * hr:(gi + 1) * hr], mv_ref[gi])
            outs.append(jnp.concatenate([og[hh * r:(hh + 1) * r] for hh in range(MEM_HEADS)], axis=1))
        o = jnp.concatenate(outs, axis=0) if groups > 1 else outs[0]
    else:
        q = jnp.dot(xn, wq_ref[...], preferred_element_type=F32).astype(BF16)
        heads = []
        for hh in range(MEM_HEADS):
            ls = slice(hh * MEM_HEAD_DIM, (hh + 1) * MEM_HEAD_DIM)
            s = _bdot_nt(q[:, ls], mk_ref[0, :, ls]) * scale
            s = s - jnp.max(s, axis=-1, keepdims=True)
            p = jnp.exp(s)
            p = p / jnp.sum(p, axis=-1, keepdims=True)
            heads.append(_bdot(p, mv_ref[0, :, ls]))
        o = jnp.concatenate(heads, axis=1)
    o_ref[...] = h1 + jnp.dot(o.astype(BF16), wo_ref[...], preferred_element_type=F32)


def _mix_out_mem(mla_o, gla_o, h, woa, wob, gm, wq, mk, mv, wo, groups, r):
    t = h.shape[0]
    tm = groups * r
    row = lambda w: pl.BlockSpec((tm, w), lambda i: (i, 0))
    interleaved = mk.shape[-1] == MEM_HEAD_DIM
    assert not interleaved or (r & (r - 1) == 0 and MEM_HEADS & (MEM_HEADS - 1) == 0)
    assert interleaved or groups == 1
    if interleaved:
        kv_spec = pl.BlockSpec((groups, MEM_TOKENS * MEM_HEADS, MEM_HEAD_DIM), lambda i: (i, 0, 0))
    else:
        kv_spec = pl.BlockSpec((1, MEM_TOKENS, MEM_WIDTH), lambda i: (0, 0, 0))
    return pl.pallas_call(
        functools.partial(_mix_out_mem_kernel, groups=groups, r=r, interleaved=interleaved),
        grid=(t // tm,),
        in_specs=[row(512), row(512), row(D_MODEL), _full(woa.shape), _full(wob.shape), _full(gm.shape),
                  _full(wq.shape), kv_spec, kv_spec, _full(wo.shape)],
        out_specs=row(D_MODEL),
        out_shape=jax.ShapeDtypeStruct((t, D_MODEL), F32),
        compiler_params=_cparams(("parallel",)), name="mix_out_mem",
    )(mla_o, gla_o, h, woa, wob, gm, wq, mk, mv, wo)


MOE_TB = 512
MOE_CHUNK = 16
MOE_TM = 256
MOE_CPT = MOE_TM // MOE_CHUNK
MOE_ROW_BLOCK = 512


def _moe_local_rows(tb):
    worst = TOP_K * tb + N_EXPERTS * (MOE_CHUNK - 1)
    return -(-worst // MOE_ROW_BLOCK) * MOE_ROW_BLOCK


def _route_kernel(hp_ref, hs_ref, g_ref, wrt_ref, brc_ref, u_ref, ltri_ref, xl_ref, pos_ref, gate_ref, cnt_ref,
                  *, rl, nbp):
    tb = hp_ref.shape[0]
    h = jnp.where(pl.program_id(0) < nbp, hp_ref[...], hs_ref[...])
    xn = _rms(h, g_ref[...])
    x_hi = xn.astype(BF16)
    x_lo = (xn - x_hi.astype(F32)).astype(BF16)
    w = wrt_ref[...]
    w_hi = w.astype(BF16)
    w_lo = (w - w_hi.astype(F32)).astype(BF16)
    logits = _bdot_nt(w_hi, x_hi) + _bdot_nt(w_lo, x_hi) + _bdot_nt(w_hi, x_lo) + brc_ref[...]
    e_iota = lax.broadcasted_iota(jnp.int32, (N_EXPERTS, tb), 0)
    work = logits
    sel = jnp.zeros((N_EXPERTS, tb), jnp.bool_)
    top = None
    for _ in range(TOP_K):
        m = jnp.max(work, axis=0, keepdims=True)
        if top is None:
            top = m
        idx = jnp.min(jnp.where(work == m, e_iota, N_EXPERTS), axis=0, keepdims=True)
        pick = e_iota == idx
        sel = jnp.logical_or(sel, pick)
        work = jnp.where(pick, -jnp.inf, work)
    ex = jnp.where(sel, jnp.exp(logits - top), 0.0)
    gates = ex / jnp.sum(ex, axis=0, keepdims=True)
    self32 = jnp.where(sel, 1.0, 0.0)
    prefix = jnp.dot(self32.astype(BF16), u_ref[...], preferred_element_type=F32)
    cnt = jnp.sum(self32, axis=1, keepdims=True)
    cnt_ref[0] = jnp.broadcast_to(cnt, (N_EXPERTS, 128))
    padded = jnp.floor((cnt + (MOE_CHUNK - 1)) * (1.0 / MOE_CHUNK)) * MOE_CHUNK
    off = jnp.dot(ltri_ref[...], jnp.broadcast_to(padded, (N_EXPERTS, 128)).astype(BF16),
                  preferred_element_type=F32)[:, 0:1]
    pos = off + prefix
    pending = sel
    pos_rows, gate_rows = [], []
    for _ in range(TOP_K):
        emin = jnp.min(jnp.where(pending, e_iota, N_EXPERTS), axis=0, keepdims=True)
        pick = e_iota == emin
        pos_rows.append(jnp.sum(jnp.where(pick, pos, 0.0), axis=0, keepdims=True))
        gate_rows.append(jnp.sum(jnp.where(pick, gates, 0.0), axis=0, keepdims=True))
        pending = jnp.logical_and(pending, jnp.logical_not(pick))
    zeros4 = jnp.zeros((8 - TOP_K, tb), F32)
    pos8 = jnp.concatenate(pos_rows + [zeros4 - 1.0], axis=0)
    pos_ref[0] = pos8.astype(jnp.int32)
    gate_ref[0] = jnp.concatenate(gate_rows + [zeros4], axis=0)
    pk = [p.astype(jnp.int32) for p in pos_rows]
    for rb in range(rl // MOE_ROW_BLOCK):
        r_iota = rb * MOE_ROW_BLOCK + lax.broadcasted_iota(jnp.int32, (MOE_ROW_BLOCK, tb), 0)
        hit = r_iota == pk[0]
        for k in range(1, TOP_K):
            hit = jnp.logical_or(hit, r_iota == pk[k])
        onehot = jnp.where(hit, 1.0, 0.0).astype(BF16)
        xl_ref[0, rb * MOE_ROW_BLOCK:(rb + 1) * MOE_ROW_BLOCK, :] = jnp.dot(
            onehot, x_hi, preferred_element_type=F32).astype(BF16)


def _moe_route(hp, hs, g, wrt, brc, tb, rl):
    nbp = hp.shape[0] // tb
    nb = nbp + hs.shape[0] // tb
    u = jnp.triu(jnp.ones((tb, tb), F32), 1).astype(BF16)
    ltri = jnp.tril(jnp.ones((N_EXPERTS, N_EXPERTS), F32), -1).astype(BF16)
    c2 = lambda a: pl.BlockSpec(a.shape, lambda i: (0, 0))
    return pl.pallas_call(
        functools.partial(_route_kernel, rl=rl, nbp=nbp),
        grid=(nb,),
        in_specs=[pl.BlockSpec((tb, D_MODEL), lambda i: (jnp.minimum(i, nbp - 1), 0)),
                  pl.BlockSpec((tb, D_MODEL), lambda i: (jnp.maximum(i - nbp, 0), 0)),
                  c2(g), c2(wrt), c2(brc), c2(u), c2(ltri)],
        out_specs=(pl.BlockSpec((1, rl, D_MODEL), lambda i: (i, 0, 0)),
                   pl.BlockSpec((1, 8, tb), lambda i: (i, 0, 0)),
                   pl.BlockSpec((1, 8, tb), lambda i: (i, 0, 0)),
                   pl.BlockSpec((1, N_EXPERTS, 128), lambda i: (i, 0, 0))),
        out_shape=(jax.ShapeDtypeStruct((nb, rl, D_MODEL), BF16),
                   jax.ShapeDtypeStruct((nb, 8, tb), jnp.int32),
                   jax.ShapeDtypeStruct((nb, 8, tb), F32),
                   jax.ShapeDtypeStruct((nb, N_EXPERTS, 128), F32)),
        compiler_params=_cparams(("parallel",)), name="moe_route",
    )(hp, hs, g, wrt, brc, u, ltri)


def _moe_tables(cnt, rl, n_tiles):
    nb = cnt.shape[0]
    cpl = rl // MOE_CHUNK
    nch = (cnt + (MOE_CHUNK - 1)) // MOE_CHUNK
    loc_off = jnp.cumsum(nch, axis=1) - nch
    seg_end = jnp.cumsum(nch, axis=0)
    seg_start = seg_end - nch
    tot = seg_end[-1]
    totp = (tot + (MOE_CPT - 1)) // MOE_CPT * MOE_CPT
    e_end = jnp.cumsum(totp)
    e_start = e_end - totp
    n_used = e_end[-1] // MOE_CPT
    tiles = jnp.arange(n_tiles, dtype=jnp.int32)
    first = tiles * MOE_CPT
    e_of = jnp.minimum(jnp.sum(first[:, None] >= e_end[None, :], axis=1), N_EXPERTS - 1).astype(jnp.int32)
    o = (first - e_start[e_of])[:, None] + jnp.arange(MOE_CPT, dtype=jnp.int32)[None, :]
    valid = jnp.logical_and(o < tot[e_of][:, None], (first < e_end[-1])[:, None])
    se = seg_end.T[e_of]
    b_of = jnp.minimum(jnp.sum(o[:, :, None] >= se[:, None, :], axis=2), nb - 1)
    pick = b_of[:, :, None] == jnp.arange(nb, dtype=jnp.int32)[None, None, :]
    run_loc = (loc_off - seg_start).T[e_of]
    loc = jnp.sum(jnp.where(pick, run_loc[:, None, :], 0), axis=2) + o
    real = (b_of * cpl + loc).reshape(-1)
    valid = valid.reshape(-1)
    slots = jnp.arange(n_tiles * MOE_CPT, dtype=jnp.int32)
    spare = nb * cpl + (slots % (2 * MOE_CPT))
    src = jnp.where(valid, real, 0).astype(jnp.int32)
    dst = jnp.where(valid, real, spare).astype(jnp.int32)
    te = e_of[jnp.minimum(tiles, n_used - 1)]
    return te, src, dst, n_used.astype(jnp.int32).reshape(1)


def _experts_kernel(te_ref, src_ref, dst_ref, nu_ref, xl_hbm, wup_ref, bup_ref, wdn_ref, bdn_ref, yinit_hbm,
                    yl_hbm, xbuf, ybuf, wub, wdb, isem, osem):
    del yinit_hbm
    i = pl.program_id(0)
    nu = nu_ref[0]
    slot = lax.rem(i, 2)

    def in_copy(tile, sl, c):
        return pltpu.make_async_copy(xl_hbm.at[src_ref[tile * MOE_CPT + c]],
                                     xbuf.at[sl, pl.ds(c * MOE_CHUNK, MOE_CHUNK), :], isem.at[sl])

    def out_copy(tile, sl, c):
        return pltpu.make_async_copy(ybuf.at[sl, pl.ds(c * MOE_CHUNK, MOE_CHUNK), :],
                                     yl_hbm.at[dst_ref[tile * MOE_CPT + c]], osem.at[sl])

    @pl.when(i == 0)
    def _():
        for c in range(MOE_CPT):
            in_copy(0, 0, c).start()

    @pl.when(i < nu)
    def _():
        @pl.when(i + 1 < nu)
        def _():
            for c in range(MOE_CPT):
                in_copy(i + 1, 1 - slot, c).start()

        for c in range(MOE_CPT):
            in_copy(i, slot, c).wait()

        @pl.when(i >= 2)
        def _():
            for c in range(MOE_CPT):
                out_copy(i - 2, slot, c).wait()

        @pl.when(jnp.logical_or(i == 0, te_ref[i] != te_ref[jnp.maximum(i - 1, 0)]))
        def _():
            wub[...] = wup_ref[0].astype(BF16)
            wdb[...] = wdn_ref[0].astype(BF16)

        x = xbuf[slot]
        hu = jnp.dot(x, wub[...], preferred_element_type=F32) + bup_ref[0]
        gate = jnp.minimum(hu[:, 0:D_FF], SWIGLU_LIMIT)
        up = jnp.clip(hu[:, D_FF:2 * D_FF], -SWIGLU_LIMIT, SWIGLU_LIMIT)
        act = (up + 1.0) * gate * jax.nn.sigmoid(SWIGLU_ALPHA * gate)
        y = jnp.dot(act.astype(BF16), wdb[...], preferred_element_type=F32) + bdn_ref[0]
        ybuf[slot] = y.astype(BF16)
        for c in range(MOE_CPT):
            out_copy(i, slot, c).start()

        @pl.when(i == nu - 1)
        def _():
            for c in range(MOE_CPT):
                out_copy(i, slot, c).wait()

            @pl.when(i >= 1)
            def _():
                for c in range(MOE_CPT):
                    out_copy(i - 1, 1 - slot, c).wait()


def _moe_experts(te, src, dst, n_used, xl, wup, bup, wdn, bdn, n_tiles):
    nb, rl, _ = xl.shape
    cpl = rl // MOE_CHUNK
    xl_chunks = xl.reshape(nb * cpl, MOE_CHUNK, D_MODEL)
    y_init = jnp.zeros(((nb + 1) * cpl, MOE_CHUNK, D_MODEL), BF16)
    wmap = lambda i, te, src, dst, nu: (te[i], 0, 0)
    grid_spec = pltpu.PrefetchScalarGridSpec(
        num_scalar_prefetch=4, grid=(n_tiles,),
        in_specs=[pl.BlockSpec(memory_space=pl.ANY),
                  pl.BlockSpec((1, D_MODEL, 2 * D_FF), wmap), pl.BlockSpec((1, 1, 2 * D_FF), wmap),
                  pl.BlockSpec((1, D_FF, D_MODEL), wmap), pl.BlockSpec((1, 1, D_MODEL), wmap),
                  pl.BlockSpec(memory_space=pl.ANY)],
        out_specs=pl.BlockSpec(memory_space=pl.ANY),
        scratch_shapes=[pltpu.VMEM((2, MOE_TM, D_MODEL), BF16), pltpu.VMEM((2, MOE_TM, D_MODEL), BF16),
                        pltpu.VMEM((D_MODEL, 2 * D_FF), BF16), pltpu.VMEM((D_FF, D_MODEL), BF16),
                        pltpu.SemaphoreType.DMA((2,)), pltpu.SemaphoreType.DMA((2,))])
    yl = pl.pallas_call(
        _experts_kernel, grid_spec=grid_spec,
        out_shape=jax.ShapeDtypeStruct(y_init.shape, BF16),
        input_output_aliases={9: 0},
        compiler_params=_cparams(("arbitrary",)), name="moe_experts",
    )(te, src, dst, n_used, xl_chunks, wup, bup, wdn, bdn, y_init)
    return yl.reshape(nb + 1, rl, D_MODEL)


def _combine_kernel(yl_ref, pos_ref, gate_ref, hp_ref, hs_ref, gf_ref, yp_ref, ys_ref, *, rl, nbp):
    tb = hp_ref.shape[0]
    i = pl.program_id(0)
    pos = pos_ref[0]
    gate = gate_ref[0]
    acc = jnp.where(i < nbp, hp_ref[...], hs_ref[...])
    for rb in range(rl // MOE_ROW_BLOCK):
        r_iota = rb * MOE_ROW_BLOCK + lax.broadcasted_iota(jnp.int32, (MOE_ROW_BLOCK, tb), 0)
        w = jnp.zeros((MOE_ROW_BLOCK, tb), F32)
        for k in range(TOP_K):
            w = jnp.where(r_iota == pos[k:k + 1], gate[k:k + 1], w)
        acc = acc + _bdot_tn(w, yl_ref[0, rb * MOE_ROW_BLOCK:(rb + 1) * MOE_ROW_BLOCK, :])
    y = _rms(acc, gf_ref[...])

    @pl.when(i < nbp)
    def _():
        yp_ref[...] = y

    @pl.when(i >= nbp)
    def _():
        ys_ref[...] = y


def _moe_combine(yl, pos, gate, hp, hs, gf, tb):
    rl = yl.shape[1]
    nbp = hp.shape[0] // tb
    nb = nbp + hs.shape[0] // tb
    p_map = lambda i: (jnp.minimum(i, nbp - 1), 0)
    s_map = lambda i: (jnp.maximum(i - nbp, 0), 0)
    return pl.pallas_call(
        functools.partial(_combine_kernel, rl=rl, nbp=nbp),
        grid=(nb,),
        in_specs=[pl.BlockSpec((1, rl, D_MODEL), lambda i: (i, 0, 0)),
                  pl.BlockSpec((1, 8, tb), lambda i: (i, 0, 0)), pl.BlockSpec((1, 8, tb), lambda i: (i, 0, 0)),
                  pl.BlockSpec((tb, D_MODEL), p_map), pl.BlockSpec((tb, D_MODEL), s_map),
                  pl.BlockSpec(gf.shape, lambda i: (0, 0))],
        out_specs=(pl.BlockSpec((tb, D_MODEL), p_map), pl.BlockSpec((tb, D_MODEL), s_map)),
        out_shape=(jax.ShapeDtypeStruct(hp.shape, F32), jax.ShapeDtypeStruct(hs.shape, F32)),
        compiler_params=_cparams(("arbitrary",)), name="moe_combine",
    )(yl, pos, gate, hp, hs, gf)


def _moe(hp, hs, g, wr, br, wup, bup, wdn, bdn, gf):
    tb = math.gcd(math.gcd(hp.shape[0], hs.shape[0]), MOE_TB)
    rl = _moe_local_rows(tb)
    nb = (hp.shape[0] + hs.shape[0]) // tb
    n_tiles = -(-(nb * (rl // MOE_CHUNK) + N_EXPERTS * (MOE_CPT - 1)) // MOE_CPT)
    xl, pos, gate, cnt = _moe_route(hp, hs, g, wr.T, br.reshape(N_EXPERTS, 1), tb, rl)
    te, src, dst, n_used = _moe_tables(cnt[:, :, 0].astype(jnp.int32), rl, n_tiles)
    yl = _moe_experts(te, src, dst, n_used, xl, wup, bup, wdn, bdn, n_tiles)
    return _moe_combine(yl, pos, gate, hp, hs, gf, tb)


def _rope_tables(pos):
    half = MLA_ROPE // 2
    inv = ROPE_THETA ** (-jnp.arange(half, dtype=F32) / half)
    ang = pos.astype(F32)[:, None] * inv[None, :]
    cos, sin = jnp.cos(ang), jnp.sin(ang)
    pad = jnp.zeros((pos.shape[0], 128 - MLA_ROPE), F32)
    return jnp.concatenate([cos, cos, pad], axis=1), jnp.concatenate([sin, sin, pad], axis=1)


def _rot_cols(w):
    half = w.shape[-1] // 2
    return jnp.concatenate([-w[..., half:], w[..., :half]], axis=-1)


def _pad_cols(w, width):
    return jnp.pad(w, ((0, 0), (0, width - w.shape[1])))


def _prep_mix_weights(norm_g, w_in, q_a_norm_g, w_uq, kv_a_norm_g, w_uk, gla_gate_w2, gla_gate_b):
    o = 0
    parts = {}
    for name, size in (("cq", MLA_Q_RANK), ("ckv", MLA_KV_RANK), ("kr", MLA_ROPE), ("gq", 256), ("gk", 256),
                       ("gv", 512), ("gg", GLA_GATE_RANK), ("go", 512)):
        parts[name] = w_in[:, o:o + size]
        o += size
    w1 = jnp.concatenate([parts["cq"], parts["ckv"], parts["gq"] * (GLA_DK ** -0.5), parts["gk"], parts["gv"],
                          parts["go"]], axis=1).astype(BF16)
    w2 = jnp.concatenate([_pad_cols(parts["kr"], 128), _pad_cols(_rot_cols(parts["kr"]), 128),
                          _pad_cols(parts["gg"], 128)], axis=1).astype(BF16)
    wuq = w_uq.reshape(MLA_Q_RANK, MLA_HEADS, MLA_NOPE + MLA_ROPE)
    wuq_n = wuq[:, :, :MLA_NOPE].reshape(MLA_Q_RANK, MLA_HEADS * MLA_NOPE).astype(BF16)
    wr = wuq[:, :, MLA_NOPE:]
    widen = lambda w: jnp.pad(w, ((0, 0), (0, 0), (0, 128 - MLA_ROPE))).reshape(MLA_Q_RANK, MLA_HEADS * 128)
    wuq_r = widen(wr).astype(BF16)
    wuq_rr = widen(_rot_cols(wr)).astype(BF16)
    wukt = jnp.transpose(w_uk, (1, 2, 0)).astype(BF16)
    wg2 = jnp.pad(gla_gate_w2, ((0, 128 - GLA_GATE_RANK), (0, 0))).astype(BF16)
    return (norm_g[None], w1, w2, q_a_norm_g[None], wuq_n, wuq_r, wuq_rr, kv_a_norm_g[None], wukt, wg2,
            gla_gate_b[None])


def _prep_wuv(w_uv):
    w = jnp.transpose(w_uv, (1, 0, 2))
    z = jnp.zeros_like(w)
    even = jnp.concatenate([w, z], axis=-1)
    odd = jnp.concatenate([z, w], axis=-1)
    sel = (jnp.arange(MLA_HEADS) % 2 == 0)[:, None, None]
    return jnp.where(sel, even, odd).astype(BF16)


def kernel(x_prompt, x_sample, mem_prompt, cache_ckv, cache_krope, state_gla, cache_mem_k, cache_mem_v,
           page_table, norm_mix_g, w_in, q_a_norm_g, w_uq, kv_a_norm_g, w_uk, w_uv, gla_gate_w2, gla_gate_b,
           gla_norm_g, w_out, norm_mem_g, mem_in_norm_g, w_mem_q, w_mem_k, w_mem_v, w_mem_o, norm_moe_g,
           w_router, b_router, w_moe_up, b_moe_up, w_moe_down, b_moe_down, norm_final_g):
    depth = w_in.shape[0]
    assert depth == 1 and x_prompt.shape[0] == 1
    l = 0
    b, t, _ = x_prompt.shape
    db, ts, _ = x_sample.shape
    n_pages = page_table.shape[1]
    past = n_pages * PAGE

    mixw = _prep_mix_weights(norm_mix_g[l], w_in[l], q_a_norm_g[l], w_uq[l], kv_a_norm_g[l], w_uk[l],
                             gla_gate_w2[l], gla_gate_b[l])
    wuv_pair = _prep_wuv(w_uv[l])
    woa = w_out[l][:MLA_HEADS * MLA_V].astype(BF16)
    wob = w_out[l][MLA_HEADS * MLA_V:].astype(BF16)
    gn = gla_norm_g[l][None]
    gm = norm_mem_g[l][None]
    wmq = w_mem_q[l].astype(BF16)
    wmo = w_mem_o[l].astype(BF16)
    wup = w_moe_up[l]
    wdn = w_moe_down[l]
    bup = b_moe_up[l][:, None, :]
    bdn = b_moe_down[l][:, None, :]
    moe_args = (norm_moe_g[l][None], w_router[l], b_router[l], wup, bup, wdn, bdn, norm_final_g[None])

    hp = x_prompt.reshape(t, D_MODEL)
    cs_p, sn_p = _rope_tables(jnp.arange(t))
    tm_p = min(512, t)
    q_p, ckv_p, kr_p, kcat_p, gq, gk, gv, gl, go = _mix_in(hp, cs_p, sn_p, mixw, tm_p)
    tq = min(256, t)
    tk = min(512, t)
    mla_p = _mla_prompt(q_p, kcat_p, wuv_pair, tq, tk)
    zero_state = jnp.zeros((1, GLA_HEADS, GLA_DK, GLA_DV), F32)
    nc_p = max(1, min(4, t // GLA_CHUNK))
    gla_p, st_p = _gla(gq[None], gk[None], gv[None], gl[None], go[None], zero_state, gn, nc_p, 1)
    mk_p, mv_p = _mem_kv(mem_prompt[0], mem_in_norm_g[l][None], w_mem_k[l].astype(BF16), w_mem_v[l].astype(BF16))
    hp = _mix_out_mem(mla_p, gla_p[0], hp, woa, wob, gm, wmq, mk_p[None], mv_p[None], wmo, 1, tm_p)

    n_s = db * ts
    hs = x_sample.reshape(n_s, D_MODEL)
    cs_s, sn_s = _rope_tables(past + jnp.arange(ts))
    cs_s = jnp.tile(cs_s, (db, 1))
    sn_s = jnp.tile(sn_s, (db, 1))
    tm_s = min(512, n_s)
    q_s, ckv_s, kr_s, kcat_s, gq, gk, gv, gl, go = _mix_in(hs, cs_s, sn_s, mixw, tm_s)
    n_pg = max(1, n_pages // 8)
    mla_s = _mla_sample(page_table, q_s.reshape(MLA_HEADS, db, ts, QK_WIDTH), kcat_s.reshape(db, ts, QK_WIDTH),
                        wuv_pair, cache_ckv[l], jnp.swapaxes(cache_krope[l], 1, 2), n_pg)
    r3 = lambda a: a.reshape(db, ts, a.shape[-1])
    gla_s, st_s = _gla(r3(gq), r3(gk), r3(gv), r3(gl), r3(go), state_gla[l], gn, 1, math.gcd(db, 8))
    groups = min(8, db)
    hs = _mix_out_mem(mla_s.reshape(n_s, -1), gla_s.reshape(n_s, -1), hs, woa, wob, gm, wmq,
                      cache_mem_k[l].reshape(db, MEM_TOKENS * MEM_HEADS, MEM_HEAD_DIM),
                      cache_mem_v[l].reshape(db, MEM_TOKENS * MEM_HEADS, MEM_HEAD_DIM), wmo, groups, ts)
    y_p, y_s = _moe(hp, hs, *moe_args)

    return (y_p.reshape(b, t, D_MODEL), y_s.reshape(db, ts, D_MODEL),
            ckv_p.reshape(1, b, t, MLA_KV_RANK), kr_p.reshape(1, b, t, MLA_ROPE),
            st_p.reshape(1, b, GLA_HEADS, GLA_DK, GLA_DV),
            mk_p.reshape(1, b, MEM_TOKENS, MEM_HEADS, MEM_HEAD_DIM),
            mv_p.reshape(1, b, MEM_TOKENS, MEM_HEADS, MEM_HEAD_DIM),
            ckv_s.reshape(1, db, ts, MLA_KV_RANK), kr_s.reshape(1, db, ts, MLA_ROPE),
            st_s.reshape(1, db, GLA_HEADS, GLA_DK, GLA_DV))
```

```python
import functools
import math

import jax
import jax.numpy as jnp
from jax import lax
from jax.experimental import pallas as pl
from jax.experimental.pallas import tpu as pltpu

F32 = jnp.float32
BF16 = jnp.bfloat16

D_MODEL = 1024
MLA_HEADS = 8
MLA_NOPE = 64
MLA_ROPE = 32
MLA_V = 64
MLA_Q_RANK = 384
MLA_KV_RANK = 256
MLA_SCALE = (MLA_NOPE + MLA_ROPE) ** -0.5
MLA_QSCALE = MLA_SCALE * math.log2(math.e)
ROPE_THETA = 10000.0
QK_WIDTH = 384
GLA_HEADS = 4
GLA_DK = 64
GLA_DV = 128
GLA_GATE_RANK = 16
GLA_GATE_NORMALIZER = 16.0
GLA_CHUNK = 64
GLA_SUB = 16
MEM_TOKENS = 256
MEM_HEADS = 4
MEM_HEAD_DIM = 128
MEM_WIDTH = MEM_HEADS * MEM_HEAD_DIM
N_EXPERTS = 32
TOP_K = 4
D_FF = 1024
SWIGLU_LIMIT = 7.0
SWIGLU_ALPHA = 1.702
EPS = 1e-6
PAGE = 128
NEG = -1e30
SAMPLE_SLOTS = 4
SAMPLE_AHEAD = 2

VMEM_LIMIT = 56 * 1024 * 1024


def _cparams(sem):
    return pltpu.CompilerParams(dimension_semantics=sem, vmem_limit_bytes=VMEM_LIMIT)


def _rms(x, g):
    var = jnp.mean(x * x, axis=-1, keepdims=True)
    return x * lax.rsqrt(var + EPS) * g


def _bdot(a, b):
    return jnp.dot(a.astype(BF16), b.astype(BF16), preferred_element_type=F32)


def _bdot_nt(a, b):
    return lax.dot_general(a.astype(BF16), b.astype(BF16), (((1,), (1,)), ((), ())),
                           preferred_element_type=F32)


def _bdot_tn(a, b):
    return lax.dot_general(a.astype(BF16), b.astype(BF16), (((0,), (0,)), ((), ())),
                           preferred_element_type=F32)


def _split_dot(a, b_exact):
    hi = a.astype(BF16)
    r1 = a - hi.astype(F32)
    mid = r1.astype(BF16)
    lo = (r1 - mid.astype(F32)).astype(BF16)
    return (jnp.dot(hi, b_exact, preferred_element_type=F32)
            + jnp.dot(mid, b_exact, preferred_element_type=F32)
            + jnp.dot(lo, b_exact, preferred_element_type=F32))


def _full(shape):
    n = len(shape)
    return pl.BlockSpec(shape, lambda *_: (0,) * n)


def _mem_kv_kernel(mem_ref, g_ref, wk_ref, wv_ref, k_ref, v_ref):
    mn = _rms(mem_ref[...], g_ref[...]).astype(BF16)
    k_ref[...] = jnp.dot(mn, wk_ref[...], preferred_element_type=F32)
    v_ref[...] = jnp.dot(mn, wv_ref[...], preferred_element_type=F32)


def _mem_kv(mem, g, wk, wv):
    m = mem.shape[0]
    return pl.pallas_call(
        _mem_kv_kernel,
        out_shape=(jax.ShapeDtypeStruct((m, MEM_WIDTH), F32), jax.ShapeDtypeStruct((m, MEM_WIDTH), F32)),
        name="mem_kv",
    )(mem, g, wk, wv)


_C_CQ, _C_CKV, _C_GQ, _C_GK, _C_GV, _C_GO, _C_END = 0, 384, 640, 896, 1152, 1664, 2176


def _log_sigmoid(x):
    return jnp.minimum(x, 0.0) - jnp.log1p(jnp.exp(-jnp.abs(x)))


def _mix_in_kernel(h_ref, cs_ref, sn_ref, g_ref, w1_ref, w2_ref, qag_ref, wuqn_ref, wuqr_ref, wuqrr_ref,
                   kvag_ref, wukt_ref, wg2_ref, bg_ref,
                   q_ref, ckv_ref, kr_ref, kcat_ref, gq_ref, gk_ref, gv_ref, glog_ref, go_ref):
    xb = _rms(h_ref[...], g_ref[...]).astype(BF16)
    z1 = jnp.dot(xb, w1_ref[...], preferred_element_type=F32)
    z2 = jnp.dot(xb, w2_ref[...], preferred_element_type=F32)
    cs = cs_ref[...]
    sn = sn_ref[...]
    ckv = _rms(z1[:, _C_CKV:_C_GQ], kvag_ref[...])
    ckv_ref[...] = ckv
    krp = z2[:, 0:128] * cs + z2[:, 128:256] * sn
    kr_ref[...] = krp[:, 0:MLA_ROPE]
    kcat_ref[:, 0:MLA_KV_RANK] = ckv.astype(BF16)
    kcat_ref[:, MLA_KV_RANK:QK_WIDTH] = krp.astype(BF16)
    cqn = _rms(z1[:, _C_CQ:_C_CKV], qag_ref[...]).astype(BF16)
    qn = jnp.dot(cqn, wuqn_ref[...], preferred_element_type=F32)
    csw = jnp.concatenate([cs] * MLA_HEADS, axis=1)
    snw = jnp.concatenate([sn] * MLA_HEADS, axis=1)
    qr = (jnp.dot(cqn, wuqr_ref[...], preferred_element_type=F32) * csw
          + jnp.dot(cqn, wuqrr_ref[...], preferred_element_type=F32) * snw)
    for h in range(MLA_HEADS):
        ql = _bdot(qn[:, h * MLA_NOPE:(h + 1) * MLA_NOPE], wukt_ref[h])
        q_ref[h, :, 0:MLA_KV_RANK] = (ql * MLA_QSCALE).astype(BF16)
        q_ref[h, :, MLA_KV_RANK:QK_WIDTH] = (qr[:, h * 128:(h + 1) * 128] * MLA_QSCALE).astype(BF16)
    gq_ref[...] = z1[:, _C_GQ:_C_GK]
    gk_ref[...] = z1[:, _C_GK:_C_GV]
    gv_ref[...] = z1[:, _C_GV:_C_GO]
    go_ref[...] = z1[:, _C_GO:_C_END]
    gate = _bdot(z2[:, 256:384], wg2_ref[...]) + bg_ref[...]
    glog_ref[...] = _log_sigmoid(gate) * (1.0 / GLA_GATE_NORMALIZER)


def _mix_in(h, cs, sn, wts, tm):
    t = h.shape[0]
    grid = (t // tm,)
    row = lambda w: pl.BlockSpec((tm, w), lambda i: (i, 0))
    in_specs = [row(D_MODEL), row(128), row(128)] + [_full(w.shape) for w in wts]
    out_shape = (
        jax.ShapeDtypeStruct((MLA_HEADS, t, QK_WIDTH), BF16),
        jax.ShapeDtypeStruct((t, MLA_KV_RANK), F32),
        jax.ShapeDtypeStruct((t, MLA_ROPE), F32),
        jax.ShapeDtypeStruct((t, QK_WIDTH), BF16),
        jax.ShapeDtypeStruct((t, 256), F32),
        jax.ShapeDtypeStruct((t, 256), F32),
        jax.ShapeDtypeStruct((t, 512), F32),
        jax.ShapeDtypeStruct((t, 256), F32),
        jax.ShapeDtypeStruct((t, 512), F32),
    )
    out_specs = (
        pl.BlockSpec((MLA_HEADS, tm, QK_WIDTH), lambda i: (0, i, 0)),
        row(MLA_KV_RANK), row(MLA_ROPE), row(QK_WIDTH), row(256), row(256), row(512), row(256), row(512),
    )
    return pl.pallas_call(
        _mix_in_kernel, grid=grid, in_specs=in_specs, out_specs=out_specs, out_shape=out_shape,
        compiler_params=_cparams(("parallel",)), name="mix_in",
    )(h, cs, sn, *wts)


def _softmax_step(s, m_ref, l_ref, acc_ref, v):
    m_old = m_ref[...]
    m_new = jnp.maximum(m_old, jnp.max(s, axis=-1, keepdims=True))
    alpha = jnp.exp2(m_old - m_new)
    p = jnp.exp2(s - m_new)
    l_ref[...] = alpha * l_ref[...] + jnp.sum(p, axis=-1, keepdims=True)
    acc_ref[...] = alpha * acc_ref[...] + jnp.dot(p.astype(BF16), v, preferred_element_type=F32)
    m_ref[...] = m_new


def _mla_out(acc_ref, l_ref, wuv_ref, o_ref, rows):
    o = acc_ref[...] / l_ref[...]
    for p in range(MLA_HEADS // 2):
        a = o[(2 * p) * rows:(2 * p + 1) * rows].astype(BF16)
        b = o[(2 * p + 1) * rows:(2 * p + 2) * rows].astype(BF16)
        y = (jnp.dot(a, wuv_ref[2 * p], preferred_element_type=F32)
             + jnp.dot(b, wuv_ref[2 * p + 1], preferred_element_type=F32))
        o_ref[:, p * 128:(p + 1) * 128] = y.astype(o_ref.dtype)


def _mla_prompt_kernel(q_ref, k_ref, wuv_ref, o_ref, s_ref, m_ref, l_ref, acc_ref, *, tq, tk):
    i = pl.program_id(0)
    rows = MLA_HEADS * tq
    q = q_ref[...].reshape(rows, QK_WIDTH)
    m_ref[...] = jnp.full(m_ref.shape, NEG, F32)
    l_ref[...] = jnp.zeros(l_ref.shape, F32)
    acc_ref[...] = jnp.zeros(acc_ref.shape, F32)
    q0 = i * tq
    n = q0 // tk + 1

    def kblk(j):
        return k_ref[pl.ds(pl.multiple_of(j * tk, tk), tk), :]

    def scores(j, slot):
        s_ref[slot] = _bdot_nt(q, kblk(j))

    def update(j, slot, masked):
        s = s_ref[slot]
        if masked:
            qpos = q0 + (lax.broadcasted_iota(jnp.int32, (rows, tk), 0) & (tq - 1))
            kpos = j * tk + lax.broadcasted_iota(jnp.int32, (rows, tk), 1)
            s = jnp.where(kpos <= qpos, s, NEG)
        _softmax_step(s, m_ref, l_ref, acc_ref, kblk(j)[:, 0:MLA_KV_RANK])

    scores(0, 0)
    n_pair = (n - 1) // 2

    def pair(jj, c):
        j = 2 * jj
        scores(j + 1, 1)
        update(j, 0, False)
        scores(j + 2, 0)
        update(j + 1, 1, False)
        return c

    lax.fori_loop(0, n_pair, pair, 0)
    left = (n - 1) - 2 * n_pair

    @pl.when(left == 0)
    def _():
        update(n - 1, 0, True)

    @pl.when(left == 1)
    def _():
        scores(n - 1, 1)
        update(n - 2, 0, False)
        update(n - 1, 1, True)

    _mla_out(acc_ref, l_ref, wuv_ref, o_ref, tq)


def _mla_prompt(q, kcat, wuv_pair, tq, tk):
    t = kcat.shape[0]
    assert tk % tq == 0 and t % tk == 0
    rows = MLA_HEADS * tq
    return pl.pallas_call(
        functools.partial(_mla_prompt_kernel, tq=tq, tk=tk),
        grid=(t // tq,),
        in_specs=[pl.BlockSpec((MLA_HEADS, tq, QK_WIDTH), lambda i: (0, i, 0)),
                  pl.BlockSpec(kcat.shape, lambda i: (0, 0), pipeline_mode=pl.Buffered(1)),
                  pl.BlockSpec(wuv_pair.shape, lambda i: (0, 0, 0), pipeline_mode=pl.Buffered(1))],
        out_specs=pl.BlockSpec((tq, MLA_HEADS * MLA_V), lambda i: (i, 0)),
        out_shape=jax.ShapeDtypeStruct((t, MLA_HEADS * MLA_V), BF16),
        scratch_shapes=[pltpu.VMEM((2, rows, tk), F32),
                        pltpu.VMEM((rows, 1), F32), pltpu.VMEM((rows, 1), F32),
                        pltpu.VMEM((rows, MLA_KV_RANK), F32)],
        compiler_params=_cparams(("arbitrary",)), name="mla_prompt",
    )(q, kcat, wuv_pair)


def _mla_sample_kernel(pt_ref, q_ref, knew_ref, wuv_ref, ckv_hbm, krt_hbm, o_ref,
                       ckv_buf, krt_buf, sem_c, sem_r, m_ref, l_ref, acc_ref, *, n_pg, n_groups, t_new):
    b = pl.program_id(0)
    nb = pl.num_programs(0)
    rows = MLA_HEADS * t_new
    q = q_ref[...].astype(F32).reshape(rows, QK_WIDTH).astype(BF16)

    def page_copies(bb, g, lookup):
        slot = g % SAMPLE_SLOTS
        out = []
        for i in range(n_pg):
            page = pt_ref[bb, g * n_pg + i] if lookup else 0
            out.append(pltpu.make_async_copy(ckv_hbm.at[page], ckv_buf.at[slot, i], sem_c.at[slot]))
            out.append(pltpu.make_async_copy(krt_hbm.at[page], krt_buf.at[slot, i], sem_r.at[slot]))
        return out

    def start_group(g):
        if g < n_groups:
            for c in page_copies(b, g, True):
                c.start()
        else:
            @pl.when(b + 1 < nb)
            def _():
                for c in page_copies(b + 1, g - n_groups, True):
                    c.start()

    def keys(g):
        slot = g % SAMPLE_SLOTS
        kv = ckv_buf[slot].reshape(n_pg * PAGE, MLA_KV_RANK).astype(BF16)
        krt = jnp.concatenate([krt_buf[slot, i] for i in range(n_pg)], axis=1).astype(BF16)
        return kv, krt

    def scores(g):
        for c in page_copies(b, g, False):
            c.wait()
        kv, krt = keys(g)
        return (_bdot_nt(q[:, 0:MLA_KV_RANK], kv)
                + jnp.dot(q[:, MLA_KV_RANK:MLA_KV_RANK + MLA_ROPE], krt, preferred_element_type=F32))

    @pl.when(b == 0)
    def _():
        for g in range(SAMPLE_AHEAD):
            for c in page_copies(0, g, True):
                c.start()

    m_ref[...] = jnp.full(m_ref.shape, NEG, F32)
    l_ref[...] = jnp.zeros(l_ref.shape, F32)
    acc_ref[...] = jnp.zeros(acc_ref.shape, F32)
    s = scores(0)
    for g in range(n_groups):
        start_group(g + SAMPLE_AHEAD)
        s_next = scores(g + 1) if g + 1 < n_groups else None
        _softmax_step(s, m_ref, l_ref, acc_ref, keys(g)[0])
        s = s_next

    kn = knew_ref[0]
    s = _bdot_nt(q, kn)
    qpos = lax.broadcasted_iota(jnp.int32, (rows, t_new), 0) & (t_new - 1)
    kpos = lax.broadcasted_iota(jnp.int32, (rows, t_new), 1)
    s = jnp.where(kpos <= qpos, s, NEG)
    _softmax_step(s, m_ref, l_ref, acc_ref, kn[:, 0:MLA_KV_RANK])
    _mla_out(acc_ref, l_ref, wuv_ref, o_ref.at[0], t_new)


def _mla_sample(page_table, q, knew, wuv_pair, cache_ckv, cache_krt, n_pg):
    db, n_pages = page_table.shape
    n_groups = n_pages // n_pg
    assert n_pages % n_pg == 0 and n_groups % SAMPLE_SLOTS == 0 and SAMPLE_AHEAD <= n_groups
    t_new = knew.shape[1]
    rows = MLA_HEADS * t_new
    grid_spec = pltpu.PrefetchScalarGridSpec(
        num_scalar_prefetch=1, grid=(db,),
        in_specs=[pl.BlockSpec((MLA_HEADS, 1, t_new, QK_WIDTH), lambda b, pt: (0, b, 0, 0)),
                  pl.BlockSpec((1, t_new, QK_WIDTH), lambda b, pt: (b, 0, 0)),
                  pl.BlockSpec(wuv_pair.shape, lambda b, pt: (0, 0, 0)),
                  pl.BlockSpec(memory_space=pl.ANY), pl.BlockSpec(memory_space=pl.ANY)],
        out_specs=pl.BlockSpec((1, t_new, MLA_HEADS * MLA_V), lambda b, pt: (b, 0, 0)),
        scratch_shapes=[pltpu.VMEM((SAMPLE_SLOTS, n_pg, PAGE, MLA_KV_RANK), F32),
                        pltpu.VMEM((SAMPLE_SLOTS, n_pg, MLA_ROPE, PAGE), F32),
                        pltpu.SemaphoreType.DMA((SAMPLE_SLOTS,)), pltpu.SemaphoreType.DMA((SAMPLE_SLOTS,)),
                        pltpu.VMEM((rows, 1), F32), pltpu.VMEM((rows, 1), F32),
                        pltpu.VMEM((rows, MLA_KV_RANK), F32)])
    return pl.pallas_call(
        functools.partial(_mla_sample_kernel, n_pg=n_pg, n_groups=n_groups, t_new=t_new),
        grid_spec=grid_spec,
        out_shape=jax.ShapeDtypeStruct((db, t_new, MLA_HEADS * MLA_V), BF16),
        compiler_params=_cparams(("arbitrary",)), name="mla_sample",
    )(page_table, q, knew, wuv_pair, cache_ckv, cache_krt)


def _gla_kernel(gq_ref, gk_ref, gv_ref, gl_ref, go_ref, s0_ref, gn_ref, tri_ref, o_ref, st_ref, *, c, sub, nc, bb):
    g_idx = pl.program_id(2)

    @pl.when(g_idx == 0)
    def _():
        st_ref[...] = s0_ref[...]

    tri = tri_ref[...]
    nsub = c // sub
    dk, dv = GLA_DK, GLA_DV
    lane = lax.broadcasted_iota(jnp.int32, (sub, 2 * dk), 1)
    col16 = lax.broadcasted_iota(jnp.int32, (sub, sub), 1)
    row16 = lax.broadcasted_iota(jnp.int32, (sub, sub), 0)
    eye = (lax.broadcasted_iota(jnp.int32, (dk, dk), 0) == lax.broadcasted_iota(jnp.int32, (dk, dk), 1))

    def chunk(bx, ci):
        r0 = ci * c
        q2 = gq_ref[bx, pl.ds(r0, c), :]
        k2 = gk_ref[bx, pl.ds(r0, c), :]
        g2 = gl_ref[bx, pl.ds(r0, c), :]
        b2 = _split_dot_left(tri, g2)
        blast2 = b2[c - 1:c, :]
        qe2 = q2 * jnp.exp(b2)
        kd2 = k2 * jnp.exp(blast2 - b2)
        diag = [[None] * nsub for _ in range(2)]
        for i in range(nsub):
            qi = q2[i * sub:(i + 1) * sub]
            ki = k2[i * sub:(i + 1) * sub]
            bi = b2[i * sub:(i + 1) * sub]
            a0 = jnp.zeros((sub, sub), F32)
            a1 = jnp.zeros((sub, sub), F32)
            for s in range(sub):
                x = qi * ki[s:s + 1] * jnp.exp(jnp.minimum(bi - bi[s:s + 1], 0.0))
                c0 = jnp.sum(jnp.where(lane < dk, x, 0.0), axis=1, keepdims=True)
                c1 = jnp.sum(jnp.where(lane >= dk, x, 0.0), axis=1, keepdims=True)
                a0 = jnp.where(col16 == s, c0, a0)
                a1 = jnp.where(col16 == s, c1, a1)
            diag[0][i] = jnp.where(row16 >= col16, a0, 0.0)
            diag[1][i] = jnp.where(row16 >= col16, a1, 0.0)
        for hh in range(2):
            ls = slice(hh * dk, (hh + 1) * dk)
            v = gv_ref[bx, pl.ds(r0, c), hh * dv:(hh + 1) * dv]
            vb = v.astype(BF16)
            st = st_ref[bx, hh]
            inter = _bdot(qe2[:, ls], st)
            b = b2[:, ls]
            outs = []
            for i in range(nsub):
                oi = _bdot(diag[hh][i], vb[i * sub:(i + 1) * sub])
                if i > 0:
                    ref = b[i * sub - 1:i * sub]
                    qi = q2[i * sub:(i + 1) * sub, ls] * jnp.exp(b[i * sub:(i + 1) * sub] - ref)
                    kj = k2[0:i * sub, ls] * jnp.exp(ref - b[0:i * sub])
                    oi = oi + _bdot(_bdot_nt(qi, kj), vb[0:i * sub])
                outs.append(oi)
            o = inter + (jnp.concatenate(outs, axis=0) if nsub > 1 else outs[0])
            a_row = jnp.exp(blast2[:, ls])
            a_col = jnp.sum(jnp.where(eye, a_row, 0.0), axis=1, keepdims=True)
            st_ref[bx, hh] = a_col * st + _bdot_tn(kd2[:, ls], vb)
            on = _rms(o, gn_ref[...])
            gate = go_ref[bx, pl.ds(r0, c), hh * dv:(hh + 1) * dv]
            o_ref[bx, pl.ds(r0, c), hh * dv:(hh + 1) * dv] = (on * gate * jax.nn.sigmoid(gate)).astype(o_ref.dtype)

    for bx in range(bb):
        for ci in range(nc):
            chunk(bx, ci)


def _split_dot_left(tri, x):
    hi = x.astype(BF16)
    r1 = x - hi.astype(F32)
    mid = r1.astype(BF16)
    lo = (r1 - mid.astype(F32)).astype(BF16)
    return (jnp.dot(tri, hi, preferred_element_type=F32)
            + jnp.dot(tri, mid, preferred_element_type=F32)
            + jnp.dot(tri, lo, preferred_element_type=F32))


def _gla(gq, gk, gv, glog, go, state0, gn, nc, bb):
    bsz, t, _ = gq.shape
    c = math.gcd(t, GLA_CHUNK)
    sub = min(GLA_SUB, c)
    n_groups = t // (c * nc)
    tri = jnp.tril(jnp.ones((c, c), F32)).astype(BF16)
    qk_spec = pl.BlockSpec((bb, c * nc, 2 * GLA_DK), lambda b, p, g: (b, g, p))
    v_spec = pl.BlockSpec((bb, c * nc, 2 * GLA_DV), lambda b, p, g: (b, g, p))
    st_spec = pl.BlockSpec((bb, 2, GLA_DK, GLA_DV), lambda b, p, g: (b, p, 0, 0))
    return pl.pallas_call(
        functools.partial(_gla_kernel, c=c, sub=sub, nc=nc, bb=bb),
        grid=(bsz // bb, GLA_HEADS // 2, n_groups),
        in_specs=[qk_spec, qk_spec, v_spec, qk_spec, v_spec, st_spec,
                  pl.BlockSpec((1, GLA_DV), lambda b, p, g: (0, 0)),
                  pl.BlockSpec((c, c), lambda b, p, g: (0, 0))],
        out_specs=(v_spec, st_spec),
        out_shape=(jax.ShapeDtypeStruct((bsz, t, GLA_HEADS * GLA_DV), BF16),
                   jax.ShapeDtypeStruct(state0.shape, F32)),
        compiler_params=_cparams(("parallel", "parallel", "arbitrary")), name="gla",
    )(gq, gk, gv, glog, go, state0, gn, tri)


def _mix_out_mem_kernel(mla_ref, gla_ref, h_ref, woa_ref, wob_ref, gm_ref, wq_ref, mk_ref, mv_ref, wo_ref,
                        o_ref, *, groups, r, interleaved):
    h1 = (h_ref[...] + jnp.dot(mla_ref[...], woa_ref[...], preferred_element_type=F32)
          + jnp.dot(gla_ref[...], wob_ref[...], preferred_element_type=F32))
    xn = _rms(h1, gm_ref[...]).astype(BF16)
    scale = MEM_HEAD_DIM ** -0.5
    if interleaved:
        q = jnp.dot(xn, wq_ref[...], preferred_element_type=F32)
        hr = MEM_HEADS * r
        s_parts = []
        for gi in range(groups):
            qg = jnp.concatenate([q[gi * r:(gi + 1) * r, hh * MEM_HEAD_DIM:(hh + 1) * MEM_HEAD_DIM]
                                  for hh in range(MEM_HEADS)], axis=0)
            s_parts.append(_bdot_nt(qg, mk_ref[gi]))
        s = (jnp.concatenate(s_parts, axis=0) if groups > 1 else s_parts[0]) * scale
        row = lax.broadcasted_iota(jnp.int32, s.shape, 0)
        col = lax.broadcasted_iota(jnp.int32, s.shape, 1)
        own = (col & (MEM_HEADS - 1)) == ((row >> (r.bit_length() - 1)) & (MEM_HEADS - 1))
        s = jnp.where(own, s, NEG)
        s = s - jnp.max(s, axis=-1, keepdims=True)
        p = jnp.exp(s)
        p = (p / jnp.sum(p, axis=-1, keepdims=True)).astype(BF16)
        outs = []
        for gi in range(groups):
            og = _bdot(p[gi * hr:(gi + 1) * hr], mv_ref[gi])
            outs.append(jnp.concatenate([og[hh * r:(hh + 1) * r] for hh in range(MEM_HEADS)], axis=1))
        o = jnp.concatenate(outs, axis=0) if groups > 1 else outs[0]
    else:
        q = jnp.dot(xn, wq_ref[...], preferred_element_type=F32).astype(BF16)
        heads = []
        for hh in range(MEM_HEADS):
            ls = slice(hh * MEM_HEAD_DIM, (hh + 1) * MEM_HEAD_DIM)
            s = _bdot_nt(q[:, ls], mk_ref[0, :, ls]) * scale
            s = s - jnp.max(s, axis=-1, keepdims=True)
            p = jnp.exp(s)
            p = p / jnp.sum(p, axis=-1, keepdims=True)
            heads.append(_bdot(p, mv_ref[0, :, ls]))
        o = jnp.concatenate(heads, axis=1)
    o_ref[...] = h1 + jnp.dot(o.astype(BF16), wo_ref[...], preferred_element_type=F32)


def _mix_out_mem(mla_o, gla_o, h, woa, wob, gm, wq, mk, mv, wo, groups, r):
    t = h.shape[0]
    tm = groups * r
    row = lambda w: pl.BlockSpec((tm, w), lambda i: (i, 0))
    interleaved = mk.shape[-1] == MEM_HEAD_DIM
    assert not interleaved or (r & (r - 1) == 0 and MEM_HEADS & (MEM_HEADS - 1) == 0)
    assert interleaved or groups == 1
    if interleaved:
        kv_spec = pl.BlockSpec((groups, MEM_TOKENS * MEM_HEADS, MEM_HEAD_DIM), lambda i: (i, 0, 0))
    else:
        kv_spec = pl.BlockSpec((1, MEM_TOKENS, MEM_WIDTH), lambda i: (0, 0, 0))
    return pl.pallas_call(
        functools.partial(_mix_out_mem_kernel, groups=groups, r=r, interleaved=interleaved),
        grid=(t // tm,),
        in_specs=[row(512), row(512), row(D_MODEL), _full(woa.shape), _full(wob.shape), _full(gm.shape),
                  _full(wq.shape), kv_spec, kv_spec, _full(wo.shape)],
        out_specs=row(D_MODEL),
        out_shape=jax.ShapeDtypeStruct((t, D_MODEL), F32),
        compiler_params=_cparams(("parallel",)), name="mix_out_mem",
    )(mla_o, gla_o, h, woa, wob, gm, wq, mk, mv, wo)


MOE_TB = 512
MOE_CHUNK = 16
MOE_TM = 256
MOE_CPT = MOE_TM // MOE_CHUNK
MOE_ROW_BLOCK = 512


def _moe_local_rows(tb):
    worst = TOP_K * tb + N_EXPERTS * (MOE_CHUNK - 1)
    return -(-worst // MOE_ROW_BLOCK) * MOE_ROW_BLOCK


def _route_kernel(hp_ref, hs_ref, g_ref, wrt_ref, brc_ref, u_ref, ltri_ref, xl_ref, pos_ref, gate_ref, cnt_ref,
                  *, rl, nbp):
    tb = hp_ref.shape[0]
    h = jnp.where(pl.program_id(0) < nbp, hp_ref[...], hs_ref[...])
    xn = _rms(h, g_ref[...])
    x_hi = xn.astype(BF16)
    x_lo = (xn - x_hi.astype(F32)).astype(BF16)
    w = wrt_ref[...]
    w_hi = w.astype(BF16)
    w_lo = (w - w_hi.astype(F32)).astype(BF16)
    logits = _bdot_nt(w_hi, x_hi) + _bdot_nt(w_lo, x_hi) + _bdot_nt(w_hi, x_lo) + brc_ref[...]
    e_iota = lax.broadcasted_iota(jnp.int32, (N_EXPERTS, tb), 0)
    work = logits
    sel = jnp.zeros((N_EXPERTS, tb), jnp.bool_)
    top = None
    for _ in range(TOP_K):
        m = jnp.max(work, axis=0, keepdims=True)
        if top is None:
            top = m
        idx = jnp.min(jnp.where(work == m, e_iota, N_EXPERTS), axis=0, keepdims=True)
        pick = e_iota == idx
        sel = jnp.logical_or(sel, pick)
        work = jnp.where(pick, -jnp.inf, work)
    ex = jnp.where(sel, jnp.exp(logits - top), 0.0)
    gates = ex / jnp.sum(ex, axis=0, keepdims=True)
    self32 = jnp.where(sel, 1.0, 0.0)
    prefix = jnp.dot(self32.astype(BF16), u_ref[...], preferred_element_type=F32)
    cnt = jnp.sum(self32, axis=1, keepdims=True)
    cnt_ref[0] = jnp.broadcast_to(cnt, (N_EXPERTS, 128))
    padded = jnp.floor((cnt + (MOE_CHUNK - 1)) * (1.0 / MOE_CHUNK)) * MOE_CHUNK
    off = jnp.dot(ltri_ref[...], jnp.broadcast_to(padded, (N_EXPERTS, 128)).astype(BF16),
                  preferred_element_type=F32)[:, 0:1]
    pos = off + prefix
    pending = sel
    pos_rows, gate_rows = [], []
    for _ in range(TOP_K):
        emin = jnp.min(jnp.where(pending, e_iota, N_EXPERTS), axis=0, keepdims=True)
        pick = e_iota == emin
        pos_rows.append(jnp.sum(jnp.where(pick, pos, 0.0), axis=0, keepdims=True))
        gate_rows.append(jnp.sum(jnp.where(pick, gates, 0.0), axis=0, keepdims=True))
        pending = jnp.logical_and(pending, jnp.logical_not(pick))
    zeros4 = jnp.zeros((8 - TOP_K, tb), F32)
    pos8 = jnp.concatenate(pos_rows + [zeros4 - 1.0], axis=0)
    pos_ref[0] = pos8.astype(jnp.int32)
    gate_ref[0] = jnp.concatenate(gate_rows + [zeros4], axis=0)
    pk = [p.astype(jnp.int32) for p in pos_rows]
    for rb in range(rl // MOE_ROW_BLOCK):
        r_iota = rb * MOE_ROW_BLOCK + lax.broadcasted_iota(jnp.int32, (MOE_ROW_BLOCK, tb), 0)
        hit = r_iota == pk[0]
        for k in range(1, TOP_K):
            hit = jnp.logical_or(hit, r_iota == pk[k])
        onehot = jnp.where(hit, 1.0, 0.0).astype(BF16)
        xl_ref[0, rb * MOE_ROW_BLOCK:(rb + 1) * MOE_ROW_BLOCK, :] = jnp.dot(
            onehot, x_hi, preferred_element_type=F32).astype(BF16)


def _moe_route(hp, hs, g, wrt, brc, tb, rl):
    nbp = hp.shape[0] // tb
    nb = nbp + hs.shape[0] // tb
    u = jnp.triu(jnp.ones((tb, tb), F32), 1).astype(BF16)
    ltri = jnp.tril(jnp.ones((N_EXPERTS, N_EXPERTS), F32), -1).astype(BF16)
    c2 = lambda a: pl.BlockSpec(a.shape, lambda i: (0, 0))
    return pl.pallas_call(
        functools.partial(_route_kernel, rl=rl, nbp=nbp),
        grid=(nb,),
        in_specs=[pl.BlockSpec((tb, D_MODEL), lambda i: (jnp.minimum(i, nbp - 1), 0)),
                  pl.BlockSpec((tb, D_MODEL), lambda i: (jnp.maximum(i - nbp, 0), 0)),
                  c2(g), c2(wrt), c2(brc), c2(u), c2(ltri)],
        out_specs=(pl.BlockSpec((1, rl, D_MODEL), lambda i: (i, 0, 0)),
                   pl.BlockSpec((1, 8, tb), lambda i: (i, 0, 0)),
                   pl.BlockSpec((1, 8, tb), lambda i: (i, 0, 0)),
                   pl.BlockSpec((1, N_EXPERTS, 128), lambda i: (i, 0, 0))),
        out_shape=(jax.ShapeDtypeStruct((nb, rl, D_MODEL), BF16),
                   jax.ShapeDtypeStruct((nb, 8, tb), jnp.int32),
                   jax.ShapeDtypeStruct((nb, 8, tb), F32),
                   jax.ShapeDtypeStruct((nb, N_EXPERTS, 128), F32)),
        compiler_params=_cparams(("parallel",)), name="moe_route",
    )(hp, hs, g, wrt, brc, u, ltri)


def _moe_tables(cnt, rl, n_tiles):
    nb = cnt.shape[0]
    cpl = rl // MOE_CHUNK
    nch = (cnt + (MOE_CHUNK - 1)) // MOE_CHUNK
    loc_off = jnp.cumsum(nch, axis=1) - nch
    seg_end = jnp.cumsum(nch, axis=0)
    seg_start = seg_end - nch
    tot = seg_end[-1]
    totp = (tot + (MOE_CPT - 1)) // MOE_CPT * MOE_CPT
    e_end = jnp.cumsum(totp)
    e_start = e_end - totp
    n_used = e_end[-1] // MOE_CPT
    tiles = jnp.arange(n_tiles, dtype=jnp.int32)
    first = tiles * MOE_CPT
    e_of = jnp.minimum(jnp.sum(first[:, None] >= e_end[None, :], axis=1), N_EXPERTS - 1).astype(jnp.int32)
    o = (first - e_start[e_of])[:, None] + jnp.arange(MOE_CPT, dtype=jnp.int32)[None, :]
    valid = jnp.logical_and(o < tot[e_of][:, None], (first < e_end[-1])[:, None])
    se = seg_end.T[e_of]
    b_of = jnp.minimum(jnp.sum(o[:, :, None] >= se[:, None, :], axis=2), nb - 1)
    pick = b_of[:, :, None] == jnp.arange(nb, dtype=jnp.int32)[None, None, :]
    run_loc = (loc_off - seg_start).T[e_of]
    loc = jnp.sum(jnp.where(pick, run_loc[:, None, :], 0), axis=2) + o
    real = (b_of * cpl + loc).reshape(-1)
    valid = valid.reshape(-1)
    slots = jnp.arange(n_tiles * MOE_CPT, dtype=jnp.int32)
    spare = nb * cpl + (slots % (2 * MOE_CPT))
    src = jnp.where(valid, real, 0).astype(jnp.int32)
    dst = jnp.where(valid, real, spare).astype(jnp.int32)
    te = e_of[jnp.minimum(tiles, n_used - 1)]
    return te, src, dst, n_used.astype(jnp.int32).reshape(1)


def _experts_kernel(te_ref, src_ref, dst_ref, nu_ref, xl_hbm, wup_ref, bup_ref, wdn_ref, bdn_ref, yinit_hbm,
                    yl_hbm, xbuf, ybuf, wub, wdb, isem, osem):
    del yinit_hbm
    i = pl.program_id(0)
    nu = nu_ref[0]
    slot = lax.rem(i, 2)

    def in_copy(tile, sl, c):
        return pltpu.make_async_copy(xl_hbm.at[src_ref[tile * MOE_CPT + c]],
                                     xbuf.at[sl, pl.ds(c * MOE_CHUNK, MOE_CHUNK), :], isem.at[sl])

    def out_copy(tile, sl, c):
        return pltpu.make_async_copy(ybuf.at[sl, pl.ds(c * MOE_CHUNK, MOE_CHUNK), :],
                                     yl_hbm.at[dst_ref[tile * MOE_CPT + c]], osem.at[sl])

    @pl.when(i == 0)
    def _():
        for c in range(MOE_CPT):
            in_copy(0, 0, c).start()

    @pl.when(i < nu)
    def _():
        @pl.when(i + 1 < nu)
        def _():
            for c in range(MOE_CPT):
                in_copy(i + 1, 1 - slot, c).start()

        for c in range(MOE_CPT):
            in_copy(i, slot, c).wait()

        @pl.when(i >= 2)
        def _():
            for c in range(MOE_CPT):
                out_copy(i - 2, slot, c).wait()

        @pl.when(jnp.logical_or(i == 0, te_ref[i] != te_ref[jnp.maximum(i - 1, 0)]))
        def _():
            wub[...] = wup_ref[0].astype(BF16)
            wdb[...] = wdn_ref[0].astype(BF16)

        x = xbuf[slot]
        hu = jnp.dot(x, wub[...], preferred_element_type=F32) + bup_ref[0]
        gate = jnp.minimum(hu[:, 0:D_FF], SWIGLU_LIMIT)
        up = jnp.clip(hu[:, D_FF:2 * D_FF], -SWIGLU_LIMIT, SWIGLU_LIMIT)
        act = (up + 1.0) * gate * jax.nn.sigmoid(SWIGLU_ALPHA * gate)
        y = jnp.dot(act.astype(BF16), wdb[...], preferred_element_type=F32) + bdn_ref[0]
        ybuf[slot] = y.astype(BF16)
        for c in range(MOE_CPT):
            out_copy(i, slot, c).start()

        @pl.when(i == nu - 1)
        def _():
            for c in range(MOE_CPT):
                out_copy(i, slot, c).wait()

            @pl.when(i >= 1)
            def _():
                for c in range(MOE_CPT):
                    out_copy(i - 1, 1 - slot, c).wait()


def _moe_experts(te, src, dst, n_used, xl, wup, bup, wdn, bdn, n_tiles):
    nb, rl, _ = xl.shape
    cpl = rl // MOE_CHUNK
    xl_chunks = xl.reshape(nb * cpl, MOE_CHUNK, D_MODEL)
    y_init = jnp.zeros(((nb + 1) * cpl, MOE_CHUNK, D_MODEL), BF16)
    wmap = lambda i, te, src, dst, nu: (te[i], 0, 0)
    grid_spec = pltpu.PrefetchScalarGridSpec(
        num_scalar_prefetch=4, grid=(n_tiles,),
        in_specs=[pl.BlockSpec(memory_space=pl.ANY),
                  pl.BlockSpec((1, D_MODEL, 2 * D_FF), wmap), pl.BlockSpec((1, 1, 2 * D_FF), wmap),
                  pl.BlockSpec((1, D_FF, D_MODEL), wmap), pl.BlockSpec((1, 1, D_MODEL), wmap),
                  pl.BlockSpec(memory_space=pl.ANY)],
        out_specs=pl.BlockSpec(memory_space=pl.ANY),
        scratch_shapes=[pltpu.VMEM((2, MOE_TM, D_MODEL), BF16), pltpu.VMEM((2, MOE_TM, D_MODEL), BF16),
                        pltpu.VMEM((D_MODEL, 2 * D_FF), BF16), pltpu.VMEM((D_FF, D_MODEL), BF16),
                        pltpu.SemaphoreType.DMA((2,)), pltpu.SemaphoreType.DMA((2,))])
    yl = pl.pallas_call(
        _experts_kernel, grid_spec=grid_spec,
        out_shape=jax.ShapeDtypeStruct(y_init.shape, BF16),
        input_output_aliases={9: 0},
        compiler_params=_cparams(("arbitrary",)), name="moe_experts",
    )(te, src, dst, n_used, xl_chunks, wup, bup, wdn, bdn, y_init)
    return yl.reshape(nb + 1, rl, D_MODEL)


def _combine_kernel(yl_ref, pos_ref, gate_ref, hp_ref, hs_ref, gf_ref, yp_ref, ys_ref, *, rl, nbp):
    tb = hp_ref.shape[0]
    i = pl.program_id(0)
    pos = pos_ref[0]
    gate = gate_ref[0]
    acc = jnp.where(i < nbp, hp_ref[...], hs_ref[...])
    for rb in range(rl // MOE_ROW_BLOCK):
        r_iota = rb * MOE_ROW_BLOCK + lax.broadcasted_iota(jnp.int32, (MOE_ROW_BLOCK, tb), 0)
        w = jnp.zeros((MOE_ROW_BLOCK, tb), F32)
        for k in range(TOP_K):
            w = jnp.where(r_iota == pos[k:k + 1], gate[k:k + 1], w)
        acc = acc + _bdot_tn(w, yl_ref[0, rb * MOE_ROW_BLOCK:(rb + 1) * MOE_ROW_BLOCK, :])
    y = _rms(acc, gf_ref[...])

    @pl.when(i < nbp)
    def _():
        yp_ref[...] = y

    @pl.when(i >= nbp)
    def _():
        ys_ref[...] = y


def _moe_combine(yl, pos, gate, hp, hs, gf, tb):
    rl = yl.shape[1]
    nbp = hp.shape[0] // tb
    nb = nbp + hs.shape[0] // tb
    p_map = lambda i: (jnp.minimum(i, nbp - 1), 0)
    s_map = lambda i: (jnp.maximum(i - nbp, 0), 0)
    return pl.pallas_call(
        functools.partial(_combine_kernel, rl=rl, nbp=nbp),
        grid=(nb,),
        in_specs=[pl.BlockSpec((1, rl, D_MODEL), lambda i: (i, 0, 0)),
                  pl.BlockSpec((1, 8, tb), lambda i: (i, 0, 0)), pl.BlockSpec((1, 8, tb), lambda i: (i, 0, 0)),
                  pl.BlockSpec((tb, D_MODEL), p_map), pl.BlockSpec((tb, D_MODEL), s_map),
                  pl.BlockSpec(gf.shape, lambda i: (0, 0))],
        out_specs=(pl.BlockSpec((tb, D_MODEL), p_map), pl.BlockSpec((tb, D_MODEL), s_map)),
        out_shape=(jax.ShapeDtypeStruct(hp.shape, F32), jax.ShapeDtypeStruct(hs.shape, F32)),
        compiler_params=_cparams(("arbitrary",)), name="moe_combine",
    )(yl, pos, gate, hp, hs, gf)


def _moe(hp, hs, g, wr, br, wup, bup, wdn, bdn, gf):
    tb = math.gcd(math.gcd(hp.shape[0], hs.shape[0]), MOE_TB)
    rl = _moe_local_rows(tb)
    nb = (hp.shape[0] + hs.shape[0]) // tb
    n_tiles = -(-(nb * (rl // MOE_CHUNK) + N_EXPERTS * (MOE_CPT - 1)) // MOE_CPT)
    xl, pos, gate, cnt = _moe_route(hp, hs, g, wr.T, br.reshape(N_EXPERTS, 1), tb, rl)
    te, src, dst, n_used = _moe_tables(cnt[:, :, 0].astype(jnp.int32), rl, n_tiles)
    yl = _moe_experts(te, src, dst, n_used, xl, wup, bup, wdn, bdn, n_tiles)
    return _moe_combine(yl, pos, gate, hp, hs, gf, tb)


def _rope_tables(pos):
    half = MLA_ROPE // 2
    inv = ROPE_THETA ** (-jnp.arange(half, dtype=F32) / half)
    ang = pos.astype(F32)[:, None] * inv[None, :]
    cos, sin = jnp.cos(ang), jnp.sin(ang)
    pad = jnp.zeros((pos.shape[0], 128 - MLA_ROPE), F32)
    return jnp.concatenate([cos, cos, pad], axis=1), jnp.concatenate([sin, sin, pad], axis=1)


def _rot_cols(w):
    half = w.shape[-1] // 2
    return jnp.concatenate([-w[..., half:], w[..., :half]], axis=-1)


def _pad_cols(w, width):
    return jnp.pad(w, ((0, 0), (0, width - w.shape[1])))


def _prep_mix_weights(norm_g, w_in, q_a_norm_g, w_uq, kv_a_norm_g, w_uk, gla_gate_w2, gla_gate_b):
    o = 0
    parts = {}
    for name, size in (("cq", MLA_Q_RANK), ("ckv", MLA_KV_RANK), ("kr", MLA_ROPE), ("gq", 256), ("gk", 256),
                       ("gv", 512), ("gg", GLA_GATE_RANK), ("go", 512)):
        parts[name] = w_in[:, o:o + size]
        o += size
    w1 = jnp.concatenate([parts["cq"], parts["ckv"], parts["gq"] * (GLA_DK ** -0.5), parts["gk"], parts["gv"],
                          parts["go"]], axis=1).astype(BF16)
    w2 = jnp.concatenate([_pad_cols(parts["kr"], 128), _pad_cols(_rot_cols(parts["kr"]), 128),
                          _pad_cols(parts["gg"], 128)], axis=1).astype(BF16)
    wuq = w_uq.reshape(MLA_Q_RANK, MLA_HEADS, MLA_NOPE + MLA_ROPE)
    wuq_n = wuq[:, :, :MLA_NOPE].reshape(MLA_Q_RANK, MLA_HEADS * MLA_NOPE).astype(BF16)
    wr = wuq[:, :, MLA_NOPE:]
    widen = lambda w: jnp.pad(w, ((0, 0), (0, 0), (0, 128 - MLA_ROPE))).reshape(MLA_Q_RANK, MLA_HEADS * 128)
    wuq_r = widen(wr).astype(BF16)
    wuq_rr = widen(_rot_cols(wr)).astype(BF16)
    wukt = jnp.transpose(w_uk, (1, 2, 0)).astype(BF16)
    wg2 = jnp.pad(gla_gate_w2, ((0, 128 - GLA_GATE_RANK), (0, 0))).astype(BF16)
    return (norm_g[None], w1, w2, q_a_norm_g[None], wuq_n, wuq_r, wuq_rr, kv_a_norm_g[None], wukt, wg2,
            gla_gate_b[None])


def _prep_wuv(w_uv):
    w = jnp.transpose(w_uv, (1, 0, 2))
    z = jnp.zeros_like(w)
    even = jnp.concatenate([w, z], axis=-1)
    odd = jnp.concatenate([z, w], axis=-1)
    sel = (jnp.arange(MLA_HEADS) % 2 == 0)[:, None, None]
    return jnp.where(sel, even, odd).astype(BF16)


def kernel(x_prompt, x_sample, mem_prompt, cache_ckv, cache_krope, state_gla, cache_mem_k, cache_mem_v,
           page_table, norm_mix_g, w_in, q_a_norm_g, w_uq, kv_a_norm_g, w_uk, w_uv, gla_gate_w2, gla_gate_b,
           gla_norm_g, w_out, norm_mem_g, mem_in_norm_g, w_mem_q, w_mem_k, w_mem_v, w_mem_o, norm_moe_g,
           w_router, b_router, w_moe_up, b_moe_up, w_moe_down, b_moe_down, norm_final_g):
    depth = w_in.shape[0]
    assert depth == 1 and x_prompt.shape[0] == 1
    l = 0
    b, t, _ = x_prompt.shape
    db, ts, _ = x_sample.shape
    n_pages = page_table.shape[1]
    past = n_pages * PAGE

    mixw = _prep_mix_weights(norm_mix_g[l], w_in[l], q_a_norm_g[l], w_uq[l], kv_a_norm_g[l], w_uk[l],
                             gla_gate_w2[l], gla_gate_b[l])
    wuv_pair = _prep_wuv(w_uv[l])
    woa = w_out[l][:MLA_HEADS * MLA_V].astype(BF16)
    wob = w_out[l][MLA_HEADS * MLA_V:].astype(BF16)
    gn = gla_norm_g[l][None]
    gm = norm_mem_g[l][None]
    wmq = w_mem_q[l].astype(BF16)
    wmo = w_mem_o[l].astype(BF16)
    wup = w_moe_up[l]
    wdn = w_moe_down[l]
    bup = b_moe_up[l][:, None, :]
    bdn = b_moe_down[l][:, None, :]
    moe_args = (norm_moe_g[l][None], w_router[l], b_router[l], wup, bup, wdn, bdn, norm_final_g[None])

    hp = x_prompt.reshape(t, D_MODEL)
    cs_p, sn_p = _rope_tables(jnp.arange(t))
    tm_p = min(512, t)
    q_p, ckv_p, kr_p, kcat_p, gq, gk, gv, gl, go = _mix_in(hp, cs_p, sn_p, mixw, tm_p)
    tq = min(256, t)
    tk = min(512, t)
    mla_p = _mla_prompt(q_p, kcat_p, wuv_pair, tq, tk)
    zero_state = jnp.zeros((1, GLA_HEADS, GLA_DK, GLA_DV), F32)
    nc_p = max(1, min(4, t // GLA_CHUNK))
    gla_p, st_p = _gla(gq[None], gk[None], gv[None], gl[None], go[None], zero_state, gn, nc_p, 1)
    mk_p, mv_p = _mem_kv(mem_prompt[0], mem_in_norm_g[l][None], w_mem_k[l].astype(BF16), w_mem_v[l].astype(BF16))
    hp = _mix_out_mem(mla_p, gla_p[0], hp, woa, wob, gm, wmq, mk_p[None], mv_p[None], wmo, 1, tm_p)

    n_s = db * ts
    hs = x_sample.reshape(n_s, D_MODEL)
    cs_s, sn_s = _rope_tables(past + jnp.arange(ts))
    cs_s = jnp.tile(cs_s, (db, 1))
    sn_s = jnp.tile(sn_s, (db, 1))
    tm_s = min(512, n_s)
    q_s, ckv_s, kr_s, kcat_s, gq, gk, gv, gl, go = _mix_in(hs, cs_s, sn_s, mixw, tm_s)
    n_pg = max(1, n_pages // 4)
    mla_s = _mla_sample(page_table, q_s.reshape(MLA_HEADS, db, ts, QK_WIDTH), kcat_s.reshape(db, ts, QK_WIDTH),
                        wuv_pair, cache_ckv[l], jnp.swapaxes(cache_krope[l], 1, 2), n_pg)
    r3 = lambda a: a.reshape(db, ts, a.shape[-1])
    gla_s, st_s = _gla(r3(gq), r3(gk), r3(gv), r3(gl), r3(go), state_gla[l], gn, 1, math.gcd(db, 8))
    groups = min(8, db)
    hs = _mix_out_mem(mla_s.reshape(n_s, -1), gla_s.reshape(n_s, -1), hs, woa, wob, gm, wmq,
                      cache_mem_k[l].reshape(db, MEM_TOKENS * MEM_HEADS, MEM_HEAD_DIM),
                      cache_mem_v[l].reshape(db, MEM_TOKENS * MEM_HEADS, MEM_HEAD_DIM), wmo, groups, ts)
    y_p, y_s = _moe(hp, hs, *moe_args)

    return (y_p.reshape(b, t, D_MODEL), y_s.reshape(db, ts, D_MODEL),
            ckv_p.reshape(1, b, t, MLA_KV_RANK), kr_p.reshape(1, b, t, MLA_ROPE),
            st_p.reshape(1, b, GLA_HEADS, GLA_DK, GLA_DV),
            mk_p.reshape(1, b, MEM_TOKENS, MEM_HEADS, MEM_HEAD_DIM),
            mv_p.reshape(1, b, MEM_TOKENS, MEM_HEADS, MEM_HEAD_DIM),
            ckv_s.reshape(1, db, ts, MLA_KV_RANK), kr_s.reshape(1, db, ts, MLA_ROPE),
            st_s.reshape(1, db, GLA_HEADS, GLA_DK, GLA_DV))
```

```python
import functools
import math

import jax
import jax.numpy as jnp
from jax import lax
from jax.experimental import pallas as pl
from jax.experimental.pallas import tpu as pltpu

F32 = jnp.float32
BF16 = jnp.bfloat16

D_MODEL = 1024
MLA_HEADS = 8
MLA_NOPE = 64
MLA_ROPE = 32
MLA_V = 64
MLA_Q_RANK = 384
MLA_KV_RANK = 256
MLA_SCALE = (MLA_NOPE + MLA_ROPE) ** -0.5
MLA_QSCALE = MLA_SCALE * math.log2(math.e)
ROPE_THETA = 10000.0
QK_WIDTH = 384
GLA_HEADS = 4
GLA_DK = 64
GLA_DV = 128
GLA_GATE_RANK = 16
GLA_GATE_NORMALIZER = 16.0
GLA_CHUNK = 64
GLA_SUB = 16
MEM_TOKENS = 256
MEM_HEADS = 4
MEM_HEAD_DIM = 128
MEM_WIDTH = MEM_HEADS * MEM_HEAD_DIM
N_EXPERTS = 32
TOP_K = 4
D_FF = 1024
SWIGLU_LIMIT = 7.0
SWIGLU_ALPHA = 1.702
EPS = 1e-6
PAGE = 128
NEG = -1e30
SAMPLE_SLOTS = 4
SAMPLE_AHEAD = 2

VMEM_LIMIT = 56 * 1024 * 1024


def _cparams(sem):
    return pltpu.CompilerParams(dimension_semantics=sem, vmem_limit_bytes=VMEM_LIMIT)


def _rms(x, g):
    var = jnp.mean(x * x, axis=-1, keepdims=True)
    return x * lax.rsqrt(var + EPS) * g


def _bdot(a, b):
    return jnp.dot(a.astype(BF16), b.astype(BF16), preferred_element_type=F32)


def _bdot_nt(a, b):
    return lax.dot_general(a.astype(BF16), b.astype(BF16), (((1,), (1,)), ((), ())),
                           preferred_element_type=F32)


def _bdot_tn(a, b):
    return lax.dot_general(a.astype(BF16), b.astype(BF16), (((0,), (0,)), ((), ())),
                           preferred_element_type=F32)


def _split_dot(a, b_exact):
    hi = a.astype(BF16)
    r1 = a - hi.astype(F32)
    mid = r1.astype(BF16)
    lo = (r1 - mid.astype(F32)).astype(BF16)
    return (jnp.dot(hi, b_exact, preferred_element_type=F32)
            + jnp.dot(mid, b_exact, preferred_element_type=F32)
            + jnp.dot(lo, b_exact, preferred_element_type=F32))


def _full(shape):
    n = len(shape)
    return pl.BlockSpec(shape, lambda *_: (0,) * n)


def _mem_kv_kernel(mem_ref, g_ref, wk_ref, wv_ref, k_ref, v_ref):
    mn = _rms(mem_ref[...], g_ref[...]).astype(BF16)
    k_ref[...] = jnp.dot(mn, wk_ref[...], preferred_element_type=F32)
    v_ref[...] = jnp.dot(mn, wv_ref[...], preferred_element_type=F32)


def _mem_kv(mem, g, wk, wv):
    m = mem.shape[0]
    return pl.pallas_call(
        _mem_kv_kernel,
        out_shape=(jax.ShapeDtypeStruct((m, MEM_WIDTH), F32), jax.ShapeDtypeStruct((m, MEM_WIDTH), F32)),
        name="mem_kv",
    )(mem, g, wk, wv)


_C_CQ, _C_CKV, _C_GQ, _C_GK, _C_GV, _C_GO, _C_END = 0, 384, 640, 896, 1152, 1664, 2176


def _log_sigmoid(x):
    return jnp.minimum(x, 0.0) - jnp.log1p(jnp.exp(-jnp.abs(x)))


def _mix_in_kernel(h_ref, cs_ref, sn_ref, g_ref, w1_ref, w2_ref, qag_ref, wuqn_ref, wuqr_ref, wuqrr_ref,
                   kvag_ref, wukt_ref, wg2_ref, bg_ref,
                   q_ref, ckv_ref, kr_ref, kcat_ref, gq_ref, gk_ref, gv_ref, glog_ref, go_ref):
    xb = _rms(h_ref[...], g_ref[...]).astype(BF16)
    z1 = jnp.dot(xb, w1_ref[...], preferred_element_type=F32)
    z2 = jnp.dot(xb, w2_ref[...], preferred_element_type=F32)
    cs = cs_ref[...]
    sn = sn_ref[...]
    ckv = _rms(z1[:, _C_CKV:_C_GQ], kvag_ref[...])
    ckv_ref[...] = ckv
    krp = z2[:, 0:128] * cs + z2[:, 128:256] * sn
    kr_ref[...] = krp[:, 0:MLA_ROPE]
    kcat_ref[:, 0:MLA_KV_RANK] = ckv.astype(BF16)
    kcat_ref[:, MLA_KV_RANK:QK_WIDTH] = krp.astype(BF16)
    cqn = _rms(z1[:, _C_CQ:_C_CKV], qag_ref[...]).astype(BF16)
    qn = jnp.dot(cqn, wuqn_ref[...], preferred_element_type=F32)
    csw = jnp.concatenate([cs] * MLA_HEADS, axis=1)
    snw = jnp.concatenate([sn] * MLA_HEADS, axis=1)
    qr = (jnp.dot(cqn, wuqr_ref[...], preferred_element_type=F32) * csw
          + jnp.dot(cqn, wuqrr_ref[...], preferred_element_type=F32) * snw)
    for h in range(MLA_HEADS):
        ql = _bdot(qn[:, h * MLA_NOPE:(h + 1) * MLA_NOPE], wukt_ref[h])
        q_ref[h, :, 0:MLA_KV_RANK] = (ql * MLA_QSCALE).astype(BF16)
        q_ref[h, :, MLA_KV_RANK:QK_WIDTH] = (qr[:, h * 128:(h + 1) * 128] * MLA_QSCALE).astype(BF16)
    gq_ref[...] = z1[:, _C_GQ:_C_GK]
    gk_ref[...] = z1[:, _C_GK:_C_GV]
    gv_ref[...] = z1[:, _C_GV:_C_GO]
    go_ref[...] = z1[:, _C_GO:_C_END]
    gate = _bdot(z2[:, 256:384], wg2_ref[...]) + bg_ref[...]
    glog_ref[...] = _log_sigmoid(gate) * (1.0 / GLA_GATE_NORMALIZER)


def _mix_in(h, cs, sn, wts, tm):
    t = h.shape[0]
    grid = (t // tm,)
    row = lambda w: pl.BlockSpec((tm, w), lambda i: (i, 0))
    in_specs = [row(D_MODEL), row(128), row(128)] + [_full(w.shape) for w in wts]
    out_shape = (
        jax.ShapeDtypeStruct((MLA_HEADS, t, QK_WIDTH), BF16),
        jax.ShapeDtypeStruct((t, MLA_KV_RANK), F32),
        jax.ShapeDtypeStruct((t, MLA_ROPE), F32),
        jax.ShapeDtypeStruct((t, QK_WIDTH), BF16),
        jax.ShapeDtypeStruct((t, 256), F32),
        jax.ShapeDtypeStruct((t, 256), F32),
        jax.ShapeDtypeStruct((t, 512), F32),
        jax.ShapeDtypeStruct((t, 256), F32),
        jax.ShapeDtypeStruct((t, 512), F32),
    )
    out_specs = (
        pl.BlockSpec((MLA_HEADS, tm, QK_WIDTH), lambda i: (0, i, 0)),
        row(MLA_KV_RANK), row(MLA_ROPE), row(QK_WIDTH), row(256), row(256), row(512), row(256), row(512),
    )
    return pl.pallas_call(
        _mix_in_kernel, grid=grid, in_specs=in_specs, out_specs=out_specs, out_shape=out_shape,
        compiler_params=_cparams(("parallel",)), name="mix_in",
    )(h, cs, sn, *wts)


def _lane_tile(x, width):
    return x if x.shape[1] == 1 else jnp.tile(x, (1, width // x.shape[1]))


def _softmax_step(s, m_ref, l_ref, acc_ref, v):
    m_old = m_ref[...]
    m_new = jnp.maximum(m_old, jnp.max(s, axis=-1, keepdims=True))
    alpha = jnp.exp2(m_old - m_new)
    p = jnp.exp2(s - _lane_tile(m_new, s.shape[1]))
    l_ref[...] = alpha * l_ref[...] + jnp.sum(p, axis=-1, keepdims=True)
    acc_ref[...] = (_lane_tile(alpha, acc_ref.shape[1]) * acc_ref[...]
                    + jnp.dot(p.astype(BF16), v, preferred_element_type=F32))
    m_ref[...] = m_new


def _mla_out(acc_ref, l_ref, wuv_ref, o_ref, rows):
    o = acc_ref[...] / _lane_tile(l_ref[...], acc_ref.shape[1])
    for p in range(MLA_HEADS // 2):
        a = o[(2 * p) * rows:(2 * p + 1) * rows].astype(BF16)
        b = o[(2 * p + 1) * rows:(2 * p + 2) * rows].astype(BF16)
        y = (jnp.dot(a, wuv_ref[2 * p], preferred_element_type=F32)
             + jnp.dot(b, wuv_ref[2 * p + 1], preferred_element_type=F32))
        o_ref[:, p * 128:(p + 1) * 128] = y.astype(o_ref.dtype)


def _mla_prompt_kernel(q_ref, k_ref, wuv_ref, o_ref, s_ref, m_ref, l_ref, acc_ref, *, tq, tk):
    i = pl.program_id(0)
    rows = MLA_HEADS * tq
    q = q_ref[...].reshape(rows, QK_WIDTH)
    m_ref[...] = jnp.full(m_ref.shape, NEG, F32)
    l_ref[...] = jnp.zeros(l_ref.shape, F32)
    acc_ref[...] = jnp.zeros(acc_ref.shape, F32)
    q0 = i * tq
    n = q0 // tk + 1

    def kblk(j):
        return k_ref[pl.ds(pl.multiple_of(j * tk, tk), tk), :]

    def scores(j, slot):
        s_ref[slot] = _bdot_nt(q, kblk(j))

    def update(j, slot, masked):
        s = s_ref[slot]
        if masked:
            qpos = q0 + (lax.broadcasted_iota(jnp.int32, (rows, tk), 0) & (tq - 1))
            kpos = j * tk + lax.broadcasted_iota(jnp.int32, (rows, tk), 1)
            s = jnp.where(kpos <= qpos, s, NEG)
        _softmax_step(s, m_ref, l_ref, acc_ref, kblk(j)[:, 0:MLA_KV_RANK])

    scores(0, 0)
    n_pair = (n - 1) // 2

    def pair(jj, c):
        j = 2 * jj
        scores(j + 1, 1)
        update(j, 0, False)
        scores(j + 2, 0)
        update(j + 1, 1, False)
        return c

    lax.fori_loop(0, n_pair, pair, 0)
    left = (n - 1) - 2 * n_pair

    @pl.when(left == 0)
    def _():
        update(n - 1, 0, True)

    @pl.when(left == 1)
    def _():
        scores(n - 1, 1)
        update(n - 2, 0, False)
        update(n - 1, 1, True)

    _mla_out(acc_ref, l_ref, wuv_ref, o_ref, tq)


def _mla_prompt(q, kcat, wuv_pair, tq, tk):
    t = kcat.shape[0]
    assert tk % tq == 0 and t % tk == 0
    rows = MLA_HEADS * tq
    return pl.pallas_call(
        functools.partial(_mla_prompt_kernel, tq=tq, tk=tk),
        grid=(t // tq,),
        in_specs=[pl.BlockSpec((MLA_HEADS, tq, QK_WIDTH), lambda i: (0, i, 0)),
                  pl.BlockSpec(kcat.shape, lambda i: (0, 0), pipeline_mode=pl.Buffered(1)),
                  pl.BlockSpec(wuv_pair.shape, lambda i: (0, 0, 0), pipeline_mode=pl.Buffered(1))],
        out_specs=pl.BlockSpec((tq, MLA_HEADS * MLA_V), lambda i: (i, 0)),
        out_shape=jax.ShapeDtypeStruct((t, MLA_HEADS * MLA_V), BF16),
        scratch_shapes=[pltpu.VMEM((2, rows, tk), F32),
                        pltpu.VMEM((rows, 128), F32), pltpu.VMEM((rows, 128), F32),
                        pltpu.VMEM((rows, MLA_KV_RANK), F32)],
        compiler_params=_cparams(("arbitrary",)), name="mla_prompt",
    )(q, kcat, wuv_pair)


def _mla_sample_kernel(pt_ref, q_ref, knew_ref, wuv_ref, ckv_hbm, krt_hbm, o_ref,
                       ckv_buf, krt_buf, sem_c, sem_r, m_ref, l_ref, acc_ref, *, n_pg, n_groups, t_new):
    b = pl.program_id(0)
    nb = pl.num_programs(0)
    rows = MLA_HEADS * t_new
    q = q_ref[...].astype(F32).reshape(rows, QK_WIDTH).astype(BF16)

    def page_copies(bb, g, lookup):
        slot = g % SAMPLE_SLOTS
        out = []
        for i in range(n_pg):
            page = pt_ref[bb, g * n_pg + i] if lookup else 0
            out.append(pltpu.make_async_copy(ckv_hbm.at[page], ckv_buf.at[slot, i], sem_c.at[slot]))
            out.append(pltpu.make_async_copy(krt_hbm.at[page], krt_buf.at[slot, i], sem_r.at[slot]))
        return out

    def start_group(g):
        if g < n_groups:
            for c in page_copies(b, g, True):
                c.start()
        else:
            @pl.when(b + 1 < nb)
            def _():
                for c in page_copies(b + 1, g - n_groups, True):
                    c.start()

    def keys(g):
        slot = g % SAMPLE_SLOTS
        kv = ckv_buf[slot].reshape(n_pg * PAGE, MLA_KV_RANK).astype(BF16)
        krt = jnp.concatenate([krt_buf[slot, i] for i in range(n_pg)], axis=1).astype(BF16)
        return kv, krt

    def scores(g):
        for c in page_copies(b, g, False):
            c.wait()
        kv, krt = keys(g)
        return (_bdot_nt(q[:, 0:MLA_KV_RANK], kv)
                + jnp.dot(q[:, MLA_KV_RANK:MLA_KV_RANK + MLA_ROPE], krt, preferred_element_type=F32))

    @pl.when(b == 0)
    def _():
        for g in range(SAMPLE_AHEAD):
            for c in page_copies(0, g, True):
                c.start()

    m_ref[...] = jnp.full(m_ref.shape, NEG, F32)
    l_ref[...] = jnp.zeros(l_ref.shape, F32)
    acc_ref[...] = jnp.zeros(acc_ref.shape, F32)
    s = scores(0)
    for g in range(n_groups):
        start_group(g + SAMPLE_AHEAD)
        s_next = scores(g + 1) if g + 1 < n_groups else None
        _softmax_step(s, m_ref, l_ref, acc_ref, keys(g)[0])
        s = s_next

    kn = knew_ref[0]
    s = _bdot_nt(q, kn)
    qpos = lax.broadcasted_iota(jnp.int32, (rows, t_new), 0) & (t_new - 1)
    kpos = lax.broadcasted_iota(jnp.int32, (rows, t_new), 1)
    s = jnp.where(kpos <= qpos, s, NEG)
    _softmax_step(s, m_ref, l_ref, acc_ref, kn[:, 0:MLA_KV_RANK])
    _mla_out(acc_ref, l_ref, wuv_ref, o_ref.at[0], t_new)


def _mla_sample(page_table, q, knew, wuv_pair, cache_ckv, cache_krt, n_pg):
    db, n_pages = page_table.shape
    n_groups = n_pages // n_pg
    assert n_pages % n_pg == 0 and n_groups % SAMPLE_SLOTS == 0 and SAMPLE_AHEAD <= n_groups
    t_new = knew.shape[1]
    rows = MLA_HEADS * t_new
    grid_spec = pltpu.PrefetchScalarGridSpec(
        num_scalar_prefetch=1, grid=(db,),
        in_specs=[pl.BlockSpec((MLA_HEADS, 1, t_new, QK_WIDTH), lambda b, pt: (0, b, 0, 0)),
                  pl.BlockSpec((1, t_new, QK_WIDTH), lambda b, pt: (b, 0, 0)),
                  pl.BlockSpec(wuv_pair.shape, lambda b, pt: (0, 0, 0)),
                  pl.BlockSpec(memory_space=pl.ANY), pl.BlockSpec(memory_space=pl.ANY)],
        out_specs=pl.BlockSpec((1, t_new, MLA_HEADS * MLA_V), lambda b, pt: (b, 0, 0)),
        scratch_shapes=[pltpu.VMEM((SAMPLE_SLOTS, n_pg, PAGE, MLA_KV_RANK), F32),
                        pltpu.VMEM((SAMPLE_SLOTS, n_pg, MLA_ROPE, PAGE), F32),
                        pltpu.SemaphoreType.DMA((SAMPLE_SLOTS,)), pltpu.SemaphoreType.DMA((SAMPLE_SLOTS,)),
                        pltpu.VMEM((rows, 1), F32), pltpu.VMEM((rows, 1), F32),
                        pltpu.VMEM((rows, MLA_KV_RANK), F32)])
    return pl.pallas_call(
        functools.partial(_mla_sample_kernel, n_pg=n_pg, n_groups=n_groups, t_new=t_new),
        grid_spec=grid_spec,
        out_shape=jax.ShapeDtypeStruct((db, t_new, MLA_HEADS * MLA_V), BF16),
        compiler_params=_cparams(("arbitrary",)), name="mla_sample",
    )(page_table, q, knew, wuv_pair, cache_ckv, cache_krt)


def _gla_kernel(gq_ref, gk_ref, gv_ref, gl_ref, go_ref, s0_ref, gn_ref, tri_ref, o_ref, st_ref, *, c, sub, nc, bb):
    g_idx = pl.program_id(2)

    @pl.when(g_idx == 0)
    def _():
        st_ref[...] = s0_ref[...]

    tri = tri_ref[...]
    nsub = c // sub
    dk, dv = GLA_DK, GLA_DV
    lane = lax.broadcasted_iota(jnp.int32, (sub, 2 * dk), 1)
    col16 = lax.broadcasted_iota(jnp.int32, (sub, sub), 1)
    row16 = lax.broadcasted_iota(jnp.int32, (sub, sub), 0)
    eye = (lax.broadcasted_iota(jnp.int32, (dk, dk), 0) == lax.broadcasted_iota(jnp.int32, (dk, dk), 1))

    def chunk(bx, ci):
        r0 = ci * c
        q2 = gq_ref[bx, pl.ds(r0, c), :]
        k2 = gk_ref[bx, pl.ds(r0, c), :]
        g2 = gl_ref[bx, pl.ds(r0, c), :]
        b2 = _split_dot_left(tri, g2)
        blast2 = b2[c - 1:c, :]
        qe2 = q2 * jnp.exp(b2)
        kd2 = k2 * jnp.exp(blast2 - b2)
        diag = [[None] * nsub for _ in range(2)]
        for i in range(nsub):
            qi = q2[i * sub:(i + 1) * sub]
            ki = k2[i * sub:(i + 1) * sub]
            bi = b2[i * sub:(i + 1) * sub]
            a0 = jnp.zeros((sub, sub), F32)
            a1 = jnp.zeros((sub, sub), F32)
            for s in range(sub):
                x = qi * ki[s:s + 1] * jnp.exp(jnp.minimum(bi - bi[s:s + 1], 0.0))
                c0 = jnp.sum(jnp.where(lane < dk, x, 0.0), axis=1, keepdims=True)
                c1 = jnp.sum(jnp.where(lane >= dk, x, 0.0), axis=1, keepdims=True)
                a0 = jnp.where(col16 == s, c0, a0)
                a1 = jnp.where(col16 == s, c1, a1)
            diag[0][i] = jnp.where(row16 >= col16, a0, 0.0)
            diag[1][i] = jnp.where(row16 >= col16, a1, 0.0)
        for hh in range(2):
            ls = slice(hh * dk, (hh + 1) * dk)
            v = gv_ref[bx, pl.ds(r0, c), hh * dv:(hh + 1) * dv]
            vb = v.astype(BF16)
            st = st_ref[bx, hh]
            inter = _bdot(qe2[:, ls], st)
            b = b2[:, ls]
            outs = []
            for i in range(nsub):
                oi = _bdot(diag[hh][i], vb[i * sub:(i + 1) * sub])
                if i > 0:
                    ref = b[i * sub - 1:i * sub]
                    qi = q2[i * sub:(i + 1) * sub, ls] * jnp.exp(b[i * sub:(i + 1) * sub] - ref)
                    kj = k2[0:i * sub, ls] * jnp.exp(ref - b[0:i * sub])
                    oi = oi + _bdot(_bdot_nt(qi, kj), vb[0:i * sub])
                outs.append(oi)
            o = inter + (jnp.concatenate(outs, axis=0) if nsub > 1 else outs[0])
            a_row = jnp.exp(blast2[:, ls])
            a_col = jnp.sum(jnp.where(eye, a_row, 0.0), axis=1, keepdims=True)
            st_ref[bx, hh] = a_col * st + _bdot_tn(kd2[:, ls], vb)
            on = _rms(o, gn_ref[...])
            gate = go_ref[bx, pl.ds(r0, c), hh * dv:(hh + 1) * dv]
            o_ref[bx, pl.ds(r0, c), hh * dv:(hh + 1) * dv] = (on * gate * jax.nn.sigmoid(gate)).astype(o_ref.dtype)

    for bx in range(bb):
        for ci in range(nc):
            chunk(bx, ci)


def _split_dot_left(tri, x):
    hi = x.astype(BF16)
    r1 = x - hi.astype(F32)
    mid = r1.astype(BF16)
    lo = (r1 - mid.astype(F32)).astype(BF16)
    return (jnp.dot(tri, hi, preferred_element_type=F32)
            + jnp.dot(tri, mid, preferred_element_type=F32)
            + jnp.dot(tri, lo, preferred_element_type=F32))


def _gla(gq, gk, gv, glog, go, state0, gn, nc, bb):
    bsz, t, _ = gq.shape
    c = math.gcd(t, GLA_CHUNK)
    sub = min(GLA_SUB, c)
    n_groups = t // (c * nc)
    tri = jnp.tril(jnp.ones((c, c), F32)).astype(BF16)
    qk_spec = pl.BlockSpec((bb, c * nc, 2 * GLA_DK), lambda b, p, g: (b, g, p))
    v_spec = pl.BlockSpec((bb, c * nc, 2 * GLA_DV), lambda b, p, g: (b, g, p))
    st_spec = pl.BlockSpec((bb, 2, GLA_DK, GLA_DV), lambda b, p, g: (b, p, 0, 0))
    return pl.pallas_call(
        functools.partial(_gla_kernel, c=c, sub=sub, nc=nc, bb=bb),
        grid=(bsz // bb, GLA_HEADS // 2, n_groups),
        in_specs=[qk_spec, qk_spec, v_spec, qk_spec, v_spec, st_spec,
                  pl.BlockSpec((1, GLA_DV), lambda b, p, g: (0, 0)),
                  pl.BlockSpec((c, c), lambda b, p, g: (0, 0))],
        out_specs=(v_spec, st_spec),
        out_shape=(jax.ShapeDtypeStruct((bsz, t, GLA_HEADS * GLA_DV), BF16),
                   jax.ShapeDtypeStruct(state0.shape, F32)),
        compiler_params=_cparams(("parallel", "parallel", "arbitrary")), name="gla",
    )(gq, gk, gv, glog, go, state0, gn, tri)


def _mix_out_mem_kernel(mla_ref, gla_ref, h_ref, woa_ref, wob_ref, gm_ref, wq_ref, mk_ref, mv_ref, wo_ref,
                        o_ref, *, groups, r, interleaved):
    h1 = (h_ref[...] + jnp.dot(mla_ref[...], woa_ref[...], preferred_element_type=F32)
          + jnp.dot(gla_ref[...], wob_ref[...], preferred_element_type=F32))
    xn = _rms(h1, gm_ref[...]).astype(BF16)
    scale = MEM_HEAD_DIM ** -0.5
    if interleaved:
        q = jnp.dot(xn, wq_ref[...], preferred_element_type=F32)
        hr = MEM_HEADS * r
        s_parts = []
        for gi in range(groups):
            qg = jnp.concatenate([q[gi * r:(gi + 1) * r, hh * MEM_HEAD_DIM:(hh + 1) * MEM_HEAD_DIM]
                                  for hh in range(MEM_HEADS)], axis=0)
            s_parts.append(_bdot_nt(qg, mk_ref[gi]))
        s = (jnp.concatenate(s_parts, axis=0) if groups > 1 else s_parts[0]) * scale
        row = lax.broadcasted_iota(jnp.int32, s.shape, 0)
        col = lax.broadcasted_iota(jnp.int32, s.shape, 1)
        own = (col & (MEM_HEADS - 1)) == ((row >> (r.bit_length() - 1)) & (MEM_HEADS - 1))
        s = jnp.where(own, s, NEG)
        s = s - jnp.max(s, axis=-1, keepdims=True)
        p = jnp.exp(s)
        p = (p / jnp.sum(p, axis=-1, keepdims=True)).astype(BF16)
        outs = []
        for gi in range(groups):
            og = _bdot(p[gi * hr:(gi + 1) * hr], mv_ref[gi])
            outs.append(jnp.concatenate([og[hh * r:(hh + 1) * r] for hh in range(MEM_HEADS)], axis=1))
        o = jnp.concatenate(outs, axis=0) if groups > 1 else outs[0]
    else:
        q = jnp.dot(xn, wq_ref[...], preferred_element_type=F32).astype(BF16)
        heads = []
        for hh in range(MEM_HEADS):
            ls = slice(hh * MEM_HEAD_DIM, (hh + 1) * MEM_HEAD_DIM)
            s = _bdot_nt(q[:, ls], mk_ref[0, :, ls]) * scale
            s = s - jnp.max(s, axis=-1, keepdims=True)
            p = jnp.exp(s)
            p = p / jnp.sum(p, axis=-1, keepdims=True)
            heads.append(_bdot(p, mv_ref[0, :, ls]))
        o = jnp.concatenate(heads, axis=1)
    o_ref[...] = h1 + jnp.dot(o.astype(BF16), wo_ref[...], preferred_element_type=F32)


def _mix_out_mem(mla_o, gla_o, h, woa, wob, gm, wq, mk, mv, wo, groups, r):
    t = h.shape[0]
    tm = groups * r
    row = lambda w: pl.BlockSpec((tm, w), lambda i: (i, 0))
    interleaved = mk.shape[-1] == MEM_HEAD_DIM
    assert not interleaved or (r & (r - 1) == 0 and MEM_HEADS & (MEM_HEADS - 1) == 0)
    assert interleaved or groups == 1
    if interleaved:
        kv_spec = pl.BlockSpec((groups, MEM_TOKENS * MEM_HEADS, MEM_HEAD_DIM), lambda i: (i, 0, 0))
    else:
        kv_spec = pl.BlockSpec((1, MEM_TOKENS, MEM_WIDTH), lambda i: (0, 0, 0))
    return pl.pallas_call(
        functools.partial(_mix_out_mem_kernel, groups=groups, r=r, interleaved=interleaved),
        grid=(t // tm,),
        in_specs=[row(512), row(512), row(D_MODEL), _full(woa.shape), _full(wob.shape), _full(gm.shape),
                  _full(wq.shape), kv_spec, kv_spec, _full(wo.shape)],
        out_specs=row(D_MODEL),
        out_shape=jax.ShapeDtypeStruct((t, D_MODEL), F32),
        compiler_params=_cparams(("parallel",)), name="mix_out_mem",
    )(mla_o, gla_o, h, woa, wob, gm, wq, mk, mv, wo)


MOE_TB = 512
MOE_CHUNK = 16
MOE_TM = 256
MOE_CPT = MOE_TM // MOE_CHUNK
MOE_ROW_BLOCK = 512


def _moe_local_rows(tb):
    worst = TOP_K * tb + N_EXPERTS * (MOE_CHUNK - 1)
    return -(-worst // MOE_ROW_BLOCK) * MOE_ROW_BLOCK


def _route_kernel(hp_ref, hs_ref, g_ref, wrt_ref, brc_ref, u_ref, ltri_ref, xl_ref, pos_ref, gate_ref, cnt_ref,
                  *, rl, nbp):
    tb = hp_ref.shape[0]
    h = jnp.where(pl.program_id(0) < nbp, hp_ref[...], hs_ref[...])
    xn = _rms(h, g_ref[...])
    x_hi = xn.astype(BF16)
    x_lo = (xn - x_hi.astype(F32)).astype(BF16)
    w = wrt_ref[...]
    w_hi = w.astype(BF16)
    w_lo = (w - w_hi.astype(F32)).astype(BF16)
    logits = _bdot_nt(w_hi, x_hi) + _bdot_nt(w_lo, x_hi) + _bdot_nt(w_hi, x_lo) + brc_ref[...]
    e_iota = lax.broadcasted_iota(jnp.int32, (N_EXPERTS, tb), 0)
    work = logits
    sel = jnp.zeros((N_EXPERTS, tb), jnp.bool_)
    top = None
    for _ in range(TOP_K):
        m = jnp.max(work, axis=0, keepdims=True)
        if top is None:
            top = m
        idx = jnp.min(jnp.where(work == m, e_iota, N_EXPERTS), axis=0, keepdims=True)
        pick = e_iota == idx
        sel = jnp.logical_or(sel, pick)
        work = jnp.where(pick, -jnp.inf, work)
    ex = jnp.where(sel, jnp.exp(logits - top), 0.0)
    gates = ex / jnp.sum(ex, axis=0, keepdims=True)
    self32 = jnp.where(sel, 1.0, 0.0)
    prefix = jnp.dot(self32.astype(BF16), u_ref[...], preferred_element_type=F32)
    cnt = jnp.sum(self32, axis=1, keepdims=True)
    cnt_ref[0] = jnp.broadcast_to(cnt, (N_EXPERTS, 128))
    padded = jnp.floor((cnt + (MOE_CHUNK - 1)) * (1.0 / MOE_CHUNK)) * MOE_CHUNK
    off = jnp.dot(ltri_ref[...], jnp.broadcast_to(padded, (N_EXPERTS, 128)).astype(BF16),
                  preferred_element_type=F32)[:, 0:1]
    pos = off + prefix
    pending = sel
    pos_rows, gate_rows = [], []
    for _ in range(TOP_K):
        emin = jnp.min(jnp.where(pending, e_iota, N_EXPERTS), axis=0, keepdims=True)
        pick = e_iota == emin
        pos_rows.append(jnp.sum(jnp.where(pick, pos, 0.0), axis=0, keepdims=True))
        gate_rows.append(jnp.sum(jnp.where(pick, gates, 0.0), axis=0, keepdims=True))
        pending = jnp.logical_and(pending, jnp.logical_not(pick))
    zeros4 = jnp.zeros((8 - TOP_K, tb), F32)
    pos8 = jnp.concatenate(pos_rows + [zeros4 - 1.0], axis=0)
    pos_ref[0] = pos8.astype(jnp.int32)
    gate_ref[0] = jnp.concatenate(gate_rows + [zeros4], axis=0)
    pk = [p.astype(jnp.int32) for p in pos_rows]
    for rb in range(rl // MOE_ROW_BLOCK):
        r_iota = rb * MOE_ROW_BLOCK + lax.broadcasted_iota(jnp.int32, (MOE_ROW_BLOCK, tb), 0)
        hit = r_iota == pk[0]
        for k in range(1, TOP_K):
            hit = jnp.logical_or(hit, r_iota == pk[k])
        onehot = jnp.where(hit, 1.0, 0.0).astype(BF16)
        xl_ref[0, rb * MOE_ROW_BLOCK:(rb + 1) * MOE_ROW_BLOCK, :] = jnp.dot(
            onehot, x_hi, preferred_element_type=F32).astype(BF16)


def _moe_route(hp, hs, g, wrt, brc, tb, rl):
    nbp = hp.shape[0] // tb
    nb = nbp + hs.shape[0] // tb
    u = jnp.triu(jnp.ones((tb, tb), F32), 1).astype(BF16)
    ltri = jnp.tril(jnp.ones((N_EXPERTS, N_EXPERTS), F32), -1).astype(BF16)
    c2 = lambda a: pl.BlockSpec(a.shape, lambda i: (0, 0))
    return pl.pallas_call(
        functools.partial(_route_kernel, rl=rl, nbp=nbp),
        grid=(nb,),
        in_specs=[pl.BlockSpec((tb, D_MODEL), lambda i: (jnp.minimum(i, nbp - 1), 0)),
                  pl.BlockSpec((tb, D_MODEL), lambda i: (jnp.maximum(i - nbp, 0), 0)),
                  c2(g), c2(wrt), c2(brc), c2(u), c2(ltri)],
        out_specs=(pl.BlockSpec((1, rl, D_MODEL), lambda i: (i, 0, 0)),
                   pl.BlockSpec((1, 8, tb), lambda i: (i, 0, 0)),
                   pl.BlockSpec((1, 8, tb), lambda i: (i, 0, 0)),
                   pl.BlockSpec((1, N_EXPERTS, 128), lambda i: (i, 0, 0))),
        out_shape=(jax.ShapeDtypeStruct((nb, rl, D_MODEL), BF16),
                   jax.ShapeDtypeStruct((nb, 8, tb), jnp.int32),
                   jax.ShapeDtypeStruct((nb, 8, tb), F32),
                   jax.ShapeDtypeStruct((nb, N_EXPERTS, 128), F32)),
        compiler_params=_cparams(("parallel",)), name="moe_route",
    )(hp, hs, g, wrt, brc, u, ltri)


def _moe_tables(cnt, rl, n_tiles):
    nb = cnt.shape[0]
    cpl = rl // MOE_CHUNK
    nch = (cnt + (MOE_CHUNK - 1)) // MOE_CHUNK
    loc_off = jnp.cumsum(nch, axis=1) - nch
    seg_end = jnp.cumsum(nch, axis=0)
    seg_start = seg_end - nch
    tot = seg_end[-1]
    totp = (tot + (MOE_CPT - 1)) // MOE_CPT * MOE_CPT
    e_end = jnp.cumsum(totp)
    e_start = e_end - totp
    n_used = e_end[-1] // MOE_CPT
    tiles = jnp.arange(n_tiles, dtype=jnp.int32)
    first = tiles * MOE_CPT
    e_of = jnp.minimum(jnp.sum(first[:, None] >= e_end[None, :], axis=1), N_EXPERTS - 1).astype(jnp.int32)
    o = (first - e_start[e_of])[:, None] + jnp.arange(MOE_CPT, dtype=jnp.int32)[None, :]
    valid = jnp.logical_and(o < tot[e_of][:, None], (first < e_end[-1])[:, None])
    se = seg_end.T[e_of]
    b_of = jnp.minimum(jnp.sum(o[:, :, None] >= se[:, None, :], axis=2), nb - 1)
    pick = b_of[:, :, None] == jnp.arange(nb, dtype=jnp.int32)[None, None, :]
    run_loc = (loc_off - seg_start).T[e_of]
    loc = jnp.sum(jnp.where(pick, run_loc[:, None, :], 0), axis=2) + o
    real = (b_of * cpl + loc).reshape(-1)
    valid = valid.reshape(-1)
    slots = jnp.arange(n_tiles * MOE_CPT, dtype=jnp.int32)
    spare = nb * cpl + (slots % (2 * MOE_CPT))
    src = jnp.where(valid, real, 0).astype(jnp.int32)
    dst = jnp.where(valid, real, spare).astype(jnp.int32)
    te = e_of[jnp.minimum(tiles, n_used - 1)]
    return te, src, dst, n_used.astype(jnp.int32).reshape(1)


def _experts_kernel(te_ref, src_ref, dst_ref, nu_ref, xl_hbm, wup_ref, bup_ref, wdn_ref, bdn_ref, yinit_hbm,
                    yl_hbm, xbuf, ybuf, wub, wdb, isem, osem):
    del yinit_hbm
    i = pl.program_id(0)
    nu = nu_ref[0]
    slot = lax.rem(i, 2)

    def in_copy(tile, sl, c):
        return pltpu.make_async_copy(xl_hbm.at[src_ref[tile * MOE_CPT + c]],
                                     xbuf.at[sl, pl.ds(c * MOE_CHUNK, MOE_CHUNK), :], isem.at[sl])

    def out_copy(tile, sl, c):
        return pltpu.make_async_copy(ybuf.at[sl, pl.ds(c * MOE_CHUNK, MOE_CHUNK), :],
                                     yl_hbm.at[dst_ref[tile * MOE_CPT + c]], osem.at[sl])

    @pl.when(i == 0)
    def _():
        for c in range(MOE_CPT):
            in_copy(0, 0, c).start()

    @pl.when(i < nu)
    def _():
        @pl.when(i + 1 < nu)
        def _():
            for c in range(MOE_CPT):
                in_copy(i + 1, 1 - slot, c).start()

        for c in range(MOE_CPT):
            in_copy(i, slot, c).wait()

        @pl.when(i >= 2)
        def _():
            for c in range(MOE_CPT):
                out_copy(i - 2, slot, c).wait()

        @pl.when(jnp.logical_or(i == 0, te_ref[i] != te_ref[jnp.maximum(i - 1, 0)]))
        def _():
            wub[...] = wup_ref[0].astype(BF16)
            wdb[...] = wdn_ref[0].astype(BF16)

        x = xbuf[slot]
        hu = jnp.dot(x, wub[...], preferred_element_type=F32) + bup_ref[0]
        gate = jnp.minimum(hu[:, 0:D_FF], SWIGLU_LIMIT)
        up = jnp.clip(hu[:, D_FF:2 * D_FF], -SWIGLU_LIMIT, SWIGLU_LIMIT)
        act = (up + 1.0) * gate * jax.nn.sigmoid(SWIGLU_ALPHA * gate)
        y = jnp.dot(act.astype(BF16), wdb[...], preferred_element_type=F32) + bdn_ref[0]
        ybuf[slot] = y.astype(BF16)
        for c in range(MOE_CPT):
            out_copy(i, slot, c).start()

        @pl.when(i == nu - 1)
        def _():
            for c in range(MOE_CPT):
                out_copy(i, slot, c).wait()

            @pl.when(i >= 1)
            def _():
                for c in range(MOE_CPT):
                    out_copy(i - 1, 1 - slot, c).wait()


def _moe_experts(te, src, dst, n_used, xl, wup, bup, wdn, bdn, n_tiles):
    nb, rl, _ = xl.shape
    cpl = rl // MOE_CHUNK
    xl_chunks = xl.reshape(nb * cpl, MOE_CHUNK, D_MODEL)
    y_init = jnp.zeros(((nb + 1) * cpl, MOE_CHUNK, D_MODEL), BF16)
    wmap = lambda i, te, src, dst, nu: (te[i], 0, 0)
    grid_spec = pltpu.PrefetchScalarGridSpec(
        num_scalar_prefetch=4, grid=(n_tiles,),
        in_specs=[pl.BlockSpec(memory_space=pl.ANY),
                  pl.BlockSpec((1, D_MODEL, 2 * D_FF), wmap), pl.BlockSpec((1, 1, 2 * D_FF), wmap),
                  pl.BlockSpec((1, D_FF, D_MODEL), wmap), pl.BlockSpec((1, 1, D_MODEL), wmap),
                  pl.BlockSpec(memory_space=pl.ANY)],
        out_specs=pl.BlockSpec(memory_space=pl.ANY),
        scratch_shapes=[pltpu.VMEM((2, MOE_TM, D_MODEL), BF16), pltpu.VMEM((2, MOE_TM, D_MODEL), BF16),
                        pltpu.VMEM((D_MODEL, 2 * D_FF), BF16), pltpu.VMEM((D_FF, D_MODEL), BF16),
                        pltpu.SemaphoreType.DMA((2,)), pltpu.SemaphoreType.DMA((2,))])
    yl = pl.pallas_call(
        _experts_kernel, grid_spec=grid_spec,
        out_shape=jax.ShapeDtypeStruct(y_init.shape, BF16),
        input_output_aliases={9: 0},
        compiler_params=_cparams(("arbitrary",)), name="moe_experts",
    )(te, src, dst, n_used, xl_chunks, wup, bup, wdn, bdn, y_init)
    return yl.reshape(nb + 1, rl, D_MODEL)


def _combine_kernel(yl_ref, pos_ref, gate_ref, hp_ref, hs_ref, gf_ref, yp_ref, ys_ref, *, rl, nbp):
    tb = hp_ref.shape[0]
    i = pl.program_id(0)
    pos = pos_ref[0]
    gate = gate_ref[0]
    acc = jnp.where(i < nbp, hp_ref[...], hs_ref[...])
    for rb in range(rl // MOE_ROW_BLOCK):
        r_iota = rb * MOE_ROW_BLOCK + lax.broadcasted_iota(jnp.int32, (MOE_ROW_BLOCK, tb), 0)
        w = jnp.zeros((MOE_ROW_BLOCK, tb), F32)
        for k in range(TOP_K):
            w = jnp.where(r_iota == pos[k:k + 1], gate[k:k + 1], w)
        acc = acc + _bdot_tn(w, yl_ref[0, rb * MOE_ROW_BLOCK:(rb + 1) * MOE_ROW_BLOCK, :])
    y = _rms(acc, gf_ref[...])

    @pl.when(i < nbp)
    def _():
        yp_ref[...] = y

    @pl.when(i >= nbp)
    def _():
        ys_ref[...] = y


def _moe_combine(yl, pos, gate, hp, hs, gf, tb):
    rl = yl.shape[1]
    nbp = hp.shape[0] // tb
    nb = nbp + hs.shape[0] // tb
    p_map = lambda i: (jnp.minimum(i, nbp - 1), 0)
    s_map = lambda i: (jnp.maximum(i - nbp, 0), 0)
    return pl.pallas_call(
        functools.partial(_combine_kernel, rl=rl, nbp=nbp),
        grid=(nb,),
        in_specs=[pl.BlockSpec((1, rl, D_MODEL), lambda i: (i, 0, 0)),
                  pl.BlockSpec((1, 8, tb), lambda i: (i, 0, 0)), pl.BlockSpec((1, 8, tb), lambda i: (i, 0, 0)),
                  pl.BlockSpec((tb, D_MODEL), p_map), pl.BlockSpec((tb, D_MODEL), s_map),
                  pl.BlockSpec(gf.shape, lambda i: (0, 0))],
        out_specs=(pl.BlockSpec((tb, D_MODEL), p_map), pl.BlockSpec((tb, D_MODEL), s_map)),
        out_shape=(jax.ShapeDtypeStruct(hp.shape, F32), jax.ShapeDtypeStruct(hs.shape, F32)),
        compiler_params=_cparams(("arbitrary",)), name="moe_combine",
    )(yl, pos, gate, hp, hs, gf)


def _moe(hp, hs, g, wr, br, wup, bup, wdn, bdn, gf):
    tb = math.gcd(math.gcd(hp.shape[0], hs.shape[0]), MOE_TB)
    rl = _moe_local_rows(tb)
    nb = (hp.shape[0] + hs.shape[0]) // tb
    n_tiles = -(-(nb * (rl // MOE_CHUNK) + N_EXPERTS * (MOE_CPT - 1)) // MOE_CPT)
    xl, pos, gate, cnt = _moe_route(hp, hs, g, wr.T, br.reshape(N_EXPERTS, 1), tb, rl)
    te, src, dst, n_used = _moe_tables(cnt[:, :, 0].astype(jnp.int32), rl, n_tiles)
    yl = _moe_experts(te, src, dst, n_used, xl, wup, bup, wdn, bdn, n_tiles)
    return _moe_combine(yl, pos, gate, hp, hs, gf, tb)


def _rope_tables(pos):
    half = MLA_ROPE // 2
    inv = ROPE_THETA ** (-jnp.arange(half, dtype=F32) / half)
    ang = pos.astype(F32)[:, None] * inv[None, :]
    cos, sin = jnp.cos(ang), jnp.sin(ang)
    pad = jnp.zeros((pos.shape[0], 128 - MLA_ROPE), F32)
    return jnp.concatenate([cos, cos, pad], axis=1), jnp.concatenate([sin, sin, pad], axis=1)


def _rot_cols(w):
    half = w.shape[-1] // 2
    return jnp.concatenate([-w[..., half:], w[..., :half]], axis=-1)


def _pad_cols(w, width):
    return jnp.pad(w, ((0, 0), (0, width - w.shape[1])))


def _prep_mix_weights(norm_g, w_in, q_a_norm_g, w_uq, kv_a_norm_g, w_uk, gla_gate_w2, gla_gate_b):
    o = 0
    parts = {}
    for name, size in (("cq", MLA_Q_RANK), ("ckv", MLA_KV_RANK), ("kr", MLA_ROPE), ("gq", 256), ("gk", 256),
                       ("gv", 512), ("gg", GLA_GATE_RANK), ("go", 512)):
        parts[name] = w_in[:, o:o + size]
        o += size
    w1 = jnp.concatenate([parts["cq"], parts["ckv"], parts["gq"] * (GLA_DK ** -0.5), parts["gk"], parts["gv"],
                          parts["go"]], axis=1).astype(BF16)
    w2 = jnp.concatenate([_pad_cols(parts["kr"], 128), _pad_cols(_rot_cols(parts["kr"]), 128),
                          _pad_cols(parts["gg"], 128)], axis=1).astype(BF16)
    wuq = w_uq.reshape(MLA_Q_RANK, MLA_HEADS, MLA_NOPE + MLA_ROPE)
    wuq_n = wuq[:, :, :MLA_NOPE].reshape(MLA_Q_RANK, MLA_HEADS * MLA_NOPE).astype(BF16)
    wr = wuq[:, :, MLA_NOPE:]
    widen = lambda w: jnp.pad(w, ((0, 0), (0, 0), (0, 128 - MLA_ROPE))).reshape(MLA_Q_RANK, MLA_HEADS * 128)
    wuq_r = widen(wr).astype(BF16)
    wuq_rr = widen(_rot_cols(wr)).astype(BF16)
    wukt = jnp.transpose(w_uk, (1, 2, 0)).astype(BF16)
    wg2 = jnp.pad(gla_gate_w2, ((0, 128 - GLA_GATE_RANK), (0, 0))).astype(BF16)
    return (norm_g[None], w1, w2, q_a_norm_g[None], wuq_n, wuq_r, wuq_rr, kv_a_norm_g[None], wukt, wg2,
            gla_gate_b[None])


def _prep_wuv(w_uv):
    w = jnp.transpose(w_uv, (1, 0, 2))
    z = jnp.zeros_like(w)
    even = jnp.concatenate([w, z], axis=-1)
    odd = jnp.concatenate([z, w], axis=-1)
    sel = (jnp.arange(MLA_HEADS) % 2 == 0)[:, None, None]
    return jnp.where(sel, even, odd).astype(BF16)


def kernel(x_prompt, x_sample, mem_prompt, cache_ckv, cache_krope, state_gla, cache_mem_k, cache_mem_v,
           page_table, norm_mix_g, w_in, q_a_norm_g, w_uq, kv_a_norm_g, w_uk, w_uv, gla_gate_w2, gla_gate_b,
           gla_norm_g, w_out, norm_mem_g, mem_in_norm_g, w_mem_q, w_mem_k, w_mem_v, w_mem_o, norm_moe_g,
           w_router, b_router, w_moe_up, b_moe_up, w_moe_down, b_moe_down, norm_final_g):
    depth = w_in.shape[0]
    assert depth == 1 and x_prompt.shape[0] == 1
    l = 0
    b, t, _ = x_prompt.shape
    db, ts, _ = x_sample.shape
    n_pages = page_table.shape[1]
    past = n_pages * PAGE

    mixw = _prep_mix_weights(norm_mix_g[l], w_in[l], q_a_norm_g[l], w_uq[l], kv_a_norm_g[l], w_uk[l],
                             gla_gate_w2[l], gla_gate_b[l])
    wuv_pair = _prep_wuv(w_uv[l])
    woa = w_out[l][:MLA_HEADS * MLA_V].astype(BF16)
    wob = w_out[l][MLA_HEADS * MLA_V:].astype(BF16)
    gn = gla_norm_g[l][None]
    gm = norm_mem_g[l][None]
    wmq = w_mem_q[l].astype(BF16)
    wmo = w_mem_o[l].astype(BF16)
    wup = w_moe_up[l]
    wdn = w_moe_down[l]
    bup = b_moe_up[l][:, None, :]
    bdn = b_moe_down[l][:, None, :]
    moe_args = (norm_moe_g[l][None], w_router[l], b_router[l], wup, bup, wdn, bdn, norm_final_g[None])

    hp = x_prompt.reshape(t, D_MODEL)
    cs_p, sn_p = _rope_tables(jnp.arange(t))
    tm_p = min(512, t)
    q_p, ckv_p, kr_p, kcat_p, gq, gk, gv, gl, go = _mix_in(hp, cs_p, sn_p, mixw, tm_p)
    tq = min(256, t)
    tk = min(512, t)
    mla_p = _mla_prompt(q_p, kcat_p, wuv_pair, tq, tk)
    zero_state = jnp.zeros((1, GLA_HEADS, GLA_DK, GLA_DV), F32)
    nc_p = max(1, min(4, t // GLA_CHUNK))
    gla_p, st_p = _gla(gq[None], gk[None], gv[None], gl[None], go[None], zero_state, gn, nc_p, 1)
    mk_p, mv_p = _mem_kv(mem_prompt[0], mem_in_norm_g[l][None], w_mem_k[l].astype(BF16), w_mem_v[l].astype(BF16))
    hp = _mix_out_mem(mla_p, gla_p[0], hp, woa, wob, gm, wmq, mk_p[None], mv_p[None], wmo, 1, tm_p)

    n_s = db * ts
    hs = x_sample.reshape(n_s, D_MODEL)
    cs_s, sn_s = _rope_tables(past + jnp.arange(ts))
    cs_s = jnp.tile(cs_s, (db, 1))
    sn_s = jnp.tile(sn_s, (db, 1))
    tm_s = min(512, n_s)
    q_s, ckv_s, kr_s, kcat_s, gq, gk, gv, gl, go = _mix_in(hs, cs_s, sn_s, mixw, tm_s)
    n_pg = max(1, n_pages // 4)
    mla_s = _mla_sample(page_table, q_s.reshape(MLA_HEADS, db, ts, QK_WIDTH), kcat_s.reshape(db, ts, QK_WIDTH),
                        wuv_pair, cache_ckv[l], jnp.swapaxes(cache_krope[l], 1, 2), n_pg)
    r3 = lambda a: a.reshape(db, ts, a.shape[-1])
    gla_s, st_s = _gla(r3(gq), r3(gk), r3(gv), r3(gl), r3(go), state_gla[l], gn, 1, math.gcd(db, 8))
    groups = min(8, db)
    hs = _mix_out_mem(mla_s.reshape(n_s, -1), gla_s.reshape(n_s, -1), hs, woa, wob, gm, wmq,
                      cache_mem_k[l].reshape(db, MEM_TOKENS * MEM_HEADS, MEM_HEAD_DIM),
                      cache_mem_v[l].reshape(db, MEM_TOKENS * MEM_HEADS, MEM_HEAD_DIM), wmo, groups, ts)
    y_p, y_s = _moe(hp, hs, *moe_args)

    return (y_p.reshape(b, t, D_MODEL), y_s.reshape(db, ts, D_MODEL),
            ckv_p.reshape(1, b, t, MLA_KV_RANK), kr_p.reshape(1, b, t, MLA_ROPE),
            st_p.reshape(1, b, GLA_HEADS, GLA_DK, GLA_DV),
            mk_p.reshape(1, b, MEM_TOKENS, MEM_HEADS, MEM_HEAD_DIM),
            mv_p.reshape(1, b, MEM_TOKENS, MEM_HEADS, MEM_HEAD_DIM),
            ckv_s.reshape(1, db, ts, MLA_KV_RANK), kr_s.reshape(1, db, ts, MLA_ROPE),
            st_s.reshape(1, db, GLA_HEADS, GLA_DK, GLA_DV))
```

```python
import functools
import math

import jax
import jax.numpy as jnp
from jax import lax
from jax.experimental import pallas as pl
from jax.experimental.pallas import tpu as pltpu

F32 = jnp.float32
BF16 = jnp.bfloat16

D_MODEL = 1024
MLA_HEADS = 8
MLA_NOPE = 64
MLA_ROPE = 32
MLA_V = 64
MLA_Q_RANK = 384
MLA_KV_RANK = 256
MLA_SCALE = (MLA_NOPE + MLA_ROPE) ** -0.5
MLA_QSCALE = MLA_SCALE * math.log2(math.e)
ROPE_THETA = 10000.0
QK_WIDTH = 384
GLA_HEADS = 4
GLA_DK = 64
GLA_DV = 128
GLA_GATE_RANK = 16
GLA_GATE_NORMALIZER = 16.0
GLA_CHUNK = 64
GLA_SUB = 16
MEM_TOKENS = 256
MEM_HEADS = 4
MEM_HEAD_DIM = 128
MEM_WIDTH = MEM_HEADS * MEM_HEAD_DIM
N_EXPERTS = 32
TOP_K = 4
D_FF = 1024
SWIGLU_LIMIT = 7.0
SWIGLU_ALPHA = 1.702
EPS = 1e-6
PAGE = 128
NEG = -1e30
SAMPLE_SLOTS = 4
SAMPLE_AHEAD = 2

VMEM_LIMIT = 56 * 1024 * 1024


def _cparams(sem):
    return pltpu.CompilerParams(dimension_semantics=sem, vmem_limit_bytes=VMEM_LIMIT)


def _rms(x, g):
    var = jnp.mean(x * x, axis=-1, keepdims=True)
    return x * lax.rsqrt(var + EPS) * g


def _bdot(a, b):
    return jnp.dot(a.astype(BF16), b.astype(BF16), preferred_element_type=F32)


def _bdot_nt(a, b):
    return lax.dot_general(a.astype(BF16), b.astype(BF16), (((1,), (1,)), ((), ())),
                           preferred_element_type=F32)


def _bdot_tn(a, b):
    return lax.dot_general(a.astype(BF16), b.astype(BF16), (((0,), (0,)), ((), ())),
                           preferred_element_type=F32)


def _split_dot(a, b_exact):
    hi = a.astype(BF16)
    r1 = a - hi.astype(F32)
    mid = r1.astype(BF16)
    lo = (r1 - mid.astype(F32)).astype(BF16)
    return (jnp.dot(hi, b_exact, preferred_element_type=F32)
            + jnp.dot(mid, b_exact, preferred_element_type=F32)
            + jnp.dot(lo, b_exact, preferred_element_type=F32))


def _full(shape):
    n = len(shape)
    return pl.BlockSpec(shape, lambda *_: (0,) * n)


def _mem_kv_kernel(mem_ref, g_ref, wk_ref, wv_ref, k_ref, v_ref):
    mn = _rms(mem_ref[...], g_ref[...]).astype(BF16)
    k_ref[...] = jnp.dot(mn, wk_ref[...], preferred_element_type=F32)
    v_ref[...] = jnp.dot(mn, wv_ref[...], preferred_element_type=F32)


def _mem_kv(mem, g, wk, wv):
    m = mem.shape[0]
    return pl.pallas_call(
        _mem_kv_kernel,
        out_shape=(jax.ShapeDtypeStruct((m, MEM_WIDTH), F32), jax.ShapeDtypeStruct((m, MEM_WIDTH), F32)),
        name="mem_kv",
    )(mem, g, wk, wv)


_C_CQ, _C_CKV, _C_GQ, _C_GK, _C_GV, _C_GO, _C_END = 0, 384, 640, 896, 1152, 1664, 2176


def _log_sigmoid(x):
    return jnp.minimum(x, 0.0) - jnp.log1p(jnp.exp(-jnp.abs(x)))


def _mix_in_kernel(h_ref, cs_ref, sn_ref, g_ref, w1_ref, w2_ref, qag_ref, wuqn_ref, wuqr_ref, wuqrr_ref,
                   kvag_ref, wukt_ref, wg2_ref, bg_ref,
                   q_ref, ckv_ref, kr_ref, kcat_ref, gq_ref, gk_ref, gv_ref, glog_ref, go_ref):
    xb = _rms(h_ref[...], g_ref[...]).astype(BF16)
    z1 = jnp.dot(xb, w1_ref[...], preferred_element_type=F32)
    z2 = jnp.dot(xb, w2_ref[...], preferred_element_type=F32)
    cs = cs_ref[...]
    sn = sn_ref[...]
    ckv = _rms(z1[:, _C_CKV:_C_GQ], kvag_ref[...])
    ckv_ref[...] = ckv
    krp = z2[:, 0:128] * cs + z2[:, 128:256] * sn
    kr_ref[...] = krp[:, 0:MLA_ROPE]
    kcat_ref[:, 0:MLA_KV_RANK] = ckv.astype(BF16)
    kcat_ref[:, MLA_KV_RANK:QK_WIDTH] = krp.astype(BF16)
    cqn = _rms(z1[:, _C_CQ:_C_CKV], qag_ref[...]).astype(BF16)
    qn = jnp.dot(cqn, wuqn_ref[...], preferred_element_type=F32)
    csw = jnp.concatenate([cs] * MLA_HEADS, axis=1)
    snw = jnp.concatenate([sn] * MLA_HEADS, axis=1)
    qr = (jnp.dot(cqn, wuqr_ref[...], preferred_element_type=F32) * csw
          + jnp.dot(cqn, wuqrr_ref[...], preferred_element_type=F32) * snw)
    for h in range(MLA_HEADS):
        ql = _bdot(qn[:, h * MLA_NOPE:(h + 1) * MLA_NOPE], wukt_ref[h])
        q_ref[h, :, 0:MLA_KV_RANK] = (ql * MLA_QSCALE).astype(BF16)
        q_ref[h, :, MLA_KV_RANK:QK_WIDTH] = (qr[:, h * 128:(h + 1) * 128] * MLA_QSCALE).astype(BF16)
    gq_ref[...] = z1[:, _C_GQ:_C_GK]
    gk_ref[...] = z1[:, _C_GK:_C_GV]
    gv_ref[...] = z1[:, _C_GV:_C_GO]
    go_ref[...] = z1[:, _C_GO:_C_END]
    gate = _bdot(z2[:, 256:384], wg2_ref[...]) + bg_ref[...]
    glog_ref[...] = _log_sigmoid(gate) * (1.0 / GLA_GATE_NORMALIZER)


def _mix_in(h, cs, sn, wts, tm):
    t = h.shape[0]
    grid = (t // tm,)
    row = lambda w: pl.BlockSpec((tm, w), lambda i: (i, 0))
    in_specs = [row(D_MODEL), row(128), row(128)] + [_full(w.shape) for w in wts]
    out_shape = (
        jax.ShapeDtypeStruct((MLA_HEADS, t, QK_WIDTH), BF16),
        jax.ShapeDtypeStruct((t, MLA_KV_RANK), F32),
        jax.ShapeDtypeStruct((t, MLA_ROPE), F32),
        jax.ShapeDtypeStruct((t, QK_WIDTH), BF16),
        jax.ShapeDtypeStruct((t, 256), F32),
        jax.ShapeDtypeStruct((t, 256), F32),
        jax.ShapeDtypeStruct((t, 512), F32),
        jax.ShapeDtypeStruct((t, 256), F32),
        jax.ShapeDtypeStruct((t, 512), F32),
    )
    out_specs = (
        pl.BlockSpec((MLA_HEADS, tm, QK_WIDTH), lambda i: (0, i, 0)),
        row(MLA_KV_RANK), row(MLA_ROPE), row(QK_WIDTH), row(256), row(256), row(512), row(256), row(512),
    )
    return pl.pallas_call(
        _mix_in_kernel, grid=grid, in_specs=in_specs, out_specs=out_specs, out_shape=out_shape,
        compiler_params=_cparams(("parallel",)), name="mix_in",
    )(h, cs, sn, *wts)


def _lane_tile(x, width):
    if x.shape[1] == 1:
        return x
    if width <= x.shape[1]:
        return x[:, 0:width]
    return jnp.tile(x, (1, width // x.shape[1]))


def _softmax_step(s, m_ref, l_ref, acc_ref, v):
    m_old = m_ref[...]
    m_new = jnp.maximum(m_old, jnp.max(s, axis=-1, keepdims=True))
    alpha = jnp.exp2(m_old - m_new)
    p = jnp.exp2(s - _lane_tile(m_new, s.shape[1]))
    l_ref[...] = alpha * l_ref[...] + jnp.sum(p, axis=-1, keepdims=True)
    acc_ref[...] = (_lane_tile(alpha, acc_ref.shape[1]) * acc_ref[...]
                    + jnp.dot(p.astype(BF16), v, preferred_element_type=F32))
    m_ref[...] = m_new


def _mla_out(acc_ref, l_ref, wuv_ref, o_ref, rows):
    o = acc_ref[...] / _lane_tile(l_ref[...], acc_ref.shape[1])
    for p in range(MLA_HEADS // 2):
        a = o[(2 * p) * rows:(2 * p + 1) * rows].astype(BF16)
        b = o[(2 * p + 1) * rows:(2 * p + 2) * rows].astype(BF16)
        y = (jnp.dot(a, wuv_ref[2 * p], preferred_element_type=F32)
             + jnp.dot(b, wuv_ref[2 * p + 1], preferred_element_type=F32))
        o_ref[:, p * 128:(p + 1) * 128] = y.astype(o_ref.dtype)


def _mla_prompt_kernel(q_ref, k_ref, wuv_ref, o_ref, s_ref, m_ref, l_ref, acc_ref, *, tq, tk):
    i = pl.program_id(0)
    rows = MLA_HEADS * tq
    q = q_ref[...].reshape(rows, QK_WIDTH)
    m_ref[...] = jnp.full(m_ref.shape, NEG, F32)
    l_ref[...] = jnp.zeros(l_ref.shape, F32)
    acc_ref[...] = jnp.zeros(acc_ref.shape, F32)
    q0 = i * tq
    n = q0 // tk + 1

    def kblk(j):
        return k_ref[pl.ds(pl.multiple_of(j * tk, tk), tk), :]

    def scores(j, slot):
        s_ref[slot] = _bdot_nt(q, kblk(j))

    def update(j, slot, masked):
        s = s_ref[slot]
        if masked:
            qpos = q0 + (lax.broadcasted_iota(jnp.int32, (rows, tk), 0) & (tq - 1))
            kpos = j * tk + lax.broadcasted_iota(jnp.int32, (rows, tk), 1)
            s = jnp.where(kpos <= qpos, s, NEG)
        _softmax_step(s, m_ref, l_ref, acc_ref, kblk(j)[:, 0:MLA_KV_RANK])

    scores(0, 0)
    n_pair = (n - 1) // 2

    def pair(jj, c):
        j = 2 * jj
        scores(j + 1, 1)
        update(j, 0, False)
        scores(j + 2, 0)
        update(j + 1, 1, False)
        return c

    lax.fori_loop(0, n_pair, pair, 0)
    left = (n - 1) - 2 * n_pair

    @pl.when(left == 0)
    def _():
        update(n - 1, 0, True)

    @pl.when(left == 1)
    def _():
        scores(n - 1, 1)
        update(n - 2, 0, False)
        update(n - 1, 1, True)

    _mla_out(acc_ref, l_ref, wuv_ref, o_ref, tq)


def _mla_prompt(q, kcat, wuv_pair, tq, tk):
    t = kcat.shape[0]
    assert tk % tq == 0 and t % tk == 0
    rows = MLA_HEADS * tq
    return pl.pallas_call(
        functools.partial(_mla_prompt_kernel, tq=tq, tk=tk),
        grid=(t // tq,),
        in_specs=[pl.BlockSpec((MLA_HEADS, tq, QK_WIDTH), lambda i: (0, i, 0)),
                  pl.BlockSpec(kcat.shape, lambda i: (0, 0), pipeline_mode=pl.Buffered(1)),
                  pl.BlockSpec(wuv_pair.shape, lambda i: (0, 0, 0), pipeline_mode=pl.Buffered(1))],
        out_specs=pl.BlockSpec((tq, MLA_HEADS * MLA_V), lambda i: (i, 0)),
        out_shape=jax.ShapeDtypeStruct((t, MLA_HEADS * MLA_V), BF16),
        scratch_shapes=[pltpu.VMEM((2, rows, tk), F32),
                        pltpu.VMEM((rows, 128), F32), pltpu.VMEM((rows, 128), F32),
                        pltpu.VMEM((rows, MLA_KV_RANK), F32)],
        compiler_params=_cparams(("arbitrary",)), name="mla_prompt",
    )(q, kcat, wuv_pair)


def _mla_sample_kernel(pt_ref, q_ref, knew_ref, wuv_ref, ckv_hbm, krt_hbm, o_ref,
                       ckv_buf, krt_buf, sem_c, sem_r, m_ref, l_ref, acc_ref, *, n_pg, n_groups, t_new):
    b = pl.program_id(0)
    nb = pl.num_programs(0)
    rows = MLA_HEADS * t_new
    q = q_ref[...].astype(F32).reshape(rows, QK_WIDTH).astype(BF16)

    def page_copies(bb, g, lookup):
        slot = g % SAMPLE_SLOTS
        out = []
        for i in range(n_pg):
            page = pt_ref[bb, g * n_pg + i] if lookup else 0
            out.append(pltpu.make_async_copy(ckv_hbm.at[page], ckv_buf.at[slot, i], sem_c.at[slot]))
            out.append(pltpu.make_async_copy(krt_hbm.at[page], krt_buf.at[slot, i], sem_r.at[slot]))
        return out

    def start_group(g):
        if g < n_groups:
            for c in page_copies(b, g, True):
                c.start()
        else:
            @pl.when(b + 1 < nb)
            def _():
                for c in page_copies(b + 1, g - n_groups, True):
                    c.start()

    def keys(g):
        slot = g % SAMPLE_SLOTS
        kv = ckv_buf[slot].reshape(n_pg * PAGE, MLA_KV_RANK).astype(BF16)
        krt = jnp.concatenate([krt_buf[slot, i] for i in range(n_pg)], axis=1).astype(BF16)
        return kv, krt

    def scores(g):
        for c in page_copies(b, g, False):
            c.wait()
        kv, krt = keys(g)
        return (_bdot_nt(q[:, 0:MLA_KV_RANK], kv)
                + jnp.dot(q[:, MLA_KV_RANK:MLA_KV_RANK + MLA_ROPE], krt, preferred_element_type=F32))

    @pl.when(b == 0)
    def _():
        for g in range(SAMPLE_AHEAD):
            for c in page_copies(0, g, True):
                c.start()

    m_ref[...] = jnp.full(m_ref.shape, NEG, F32)
    l_ref[...] = jnp.zeros(l_ref.shape, F32)
    acc_ref[...] = jnp.zeros(acc_ref.shape, F32)
    s = scores(0)
    for g in range(n_groups):
        start_group(g + SAMPLE_AHEAD)
        s_next = scores(g + 1) if g + 1 < n_groups else None
        _softmax_step(s, m_ref, l_ref, acc_ref, keys(g)[0])
        s = s_next

    kn = knew_ref[0]
    s = _bdot_nt(q, kn)
    qpos = lax.broadcasted_iota(jnp.int32, (rows, t_new), 0) & (t_new - 1)
    kpos = lax.broadcasted_iota(jnp.int32, (rows, t_new), 1)
    s = jnp.where(kpos <= qpos, s, NEG)
    _softmax_step(s, m_ref, l_ref, acc_ref, kn[:, 0:MLA_KV_RANK])
    _mla_out(acc_ref, l_ref, wuv_ref, o_ref.at[0], t_new)


def _mla_sample(page_table, q, knew, wuv_pair, cache_ckv, cache_krt, n_pg):
    db, n_pages = page_table.shape
    n_groups = n_pages // n_pg
    assert n_pages % n_pg == 0 and n_groups % SAMPLE_SLOTS == 0 and SAMPLE_AHEAD <= n_groups
    t_new = knew.shape[1]
    rows = MLA_HEADS * t_new
    grid_spec = pltpu.PrefetchScalarGridSpec(
        num_scalar_prefetch=1, grid=(db,),
        in_specs=[pl.BlockSpec((MLA_HEADS, 1, t_new, QK_WIDTH), lambda b, pt: (0, b, 0, 0)),
                  pl.BlockSpec((1, t_new, QK_WIDTH), lambda b, pt: (b, 0, 0)),
                  pl.BlockSpec(wuv_pair.shape, lambda b, pt: (0, 0, 0)),
                  pl.BlockSpec(memory_space=pl.ANY), pl.BlockSpec(memory_space=pl.ANY)],
        out_specs=pl.BlockSpec((1, t_new, MLA_HEADS * MLA_V), lambda b, pt: (b, 0, 0)),
        scratch_shapes=[pltpu.VMEM((SAMPLE_SLOTS, n_pg, PAGE, MLA_KV_RANK), F32),
                        pltpu.VMEM((SAMPLE_SLOTS, n_pg, MLA_ROPE, PAGE), F32),
                        pltpu.SemaphoreType.DMA((SAMPLE_SLOTS,)), pltpu.SemaphoreType.DMA((SAMPLE_SLOTS,)),
                        pltpu.VMEM((rows, 128), F32), pltpu.VMEM((rows, 128), F32),
                        pltpu.VMEM((rows, MLA_KV_RANK), F32)])
    return pl.pallas_call(
        functools.partial(_mla_sample_kernel, n_pg=n_pg, n_groups=n_groups, t_new=t_new),
        grid_spec=grid_spec,
        out_shape=jax.ShapeDtypeStruct((db, t_new, MLA_HEADS * MLA_V), BF16),
        compiler_params=_cparams(("arbitrary",)), name="mla_sample",
    )(page_table, q, knew, wuv_pair, cache_ckv, cache_krt)


def _gla_kernel(gq_ref, gk_ref, gv_ref, gl_ref, go_ref, s0_ref, gn_ref, tri_ref, o_ref, st_ref, *, c, sub, nc, bb):
    g_idx = pl.program_id(2)

    @pl.when(g_idx == 0)
    def _():
        st_ref[...] = s0_ref[...]

    tri = tri_ref[...]
    nsub = c // sub
    dk, dv = GLA_DK, GLA_DV
    lane = lax.broadcasted_iota(jnp.int32, (sub, 2 * dk), 1)
    col16 = lax.broadcasted_iota(jnp.int32, (sub, sub), 1)
    row16 = lax.broadcasted_iota(jnp.int32, (sub, sub), 0)
    eye = (lax.broadcasted_iota(jnp.int32, (dk, dk), 0) == lax.broadcasted_iota(jnp.int32, (dk, dk), 1))

    def chunk(bx, ci):
        r0 = ci * c
        q2 = gq_ref[bx, pl.ds(r0, c), :]
        k2 = gk_ref[bx, pl.ds(r0, c), :]
        g2 = gl_ref[bx, pl.ds(r0, c), :]
        b2 = _split_dot_left(tri, g2)
        blast2 = b2[c - 1:c, :]
        qe2 = q2 * jnp.exp(b2)
        kd2 = k2 * jnp.exp(blast2 - b2)
        diag = [[None] * nsub for _ in range(2)]
        for i in range(nsub):
            qi = q2[i * sub:(i + 1) * sub]
            ki = k2[i * sub:(i + 1) * sub]
            bi = b2[i * sub:(i + 1) * sub]
            a0 = jnp.zeros((sub, sub), F32)
            a1 = jnp.zeros((sub, sub), F32)
            for s in range(sub):
                x = qi * ki[s:s + 1] * jnp.exp(jnp.minimum(bi - bi[s:s + 1], 0.0))
                c0 = jnp.sum(jnp.where(lane < dk, x, 0.0), axis=1, keepdims=True)
                c1 = jnp.sum(jnp.where(lane >= dk, x, 0.0), axis=1, keepdims=True)
                a0 = jnp.where(col16 == s, c0, a0)
                a1 = jnp.where(col16 == s, c1, a1)
            diag[0][i] = jnp.where(row16 >= col16, a0, 0.0)
            diag[1][i] = jnp.where(row16 >= col16, a1, 0.0)
        for hh in range(2):
            ls = slice(hh * dk, (hh + 1) * dk)
            v = gv_ref[bx, pl.ds(r0, c), hh * dv:(hh + 1) * dv]
            vb = v.astype(BF16)
            st = st_ref[bx, hh]
            inter = _bdot(qe2[:, ls], st)
            b = b2[:, ls]
            outs = []
            for i in range(nsub):
                oi = _bdot(diag[hh][i], vb[i * sub:(i + 1) * sub])
                if i > 0:
                    ref = b[i * sub - 1:i * sub]
                    qi = q2[i * sub:(i + 1) * sub, ls] * jnp.exp(b[i * sub:(i + 1) * sub] - ref)
                    kj = k2[0:i * sub, ls] * jnp.exp(ref - b[0:i * sub])
                    oi = oi + _bdot(_bdot_nt(qi, kj), vb[0:i * sub])
                outs.append(oi)
            o = inter + (jnp.concatenate(outs, axis=0) if nsub > 1 else outs[0])
            a_row = jnp.exp(blast2[:, ls])
            a_col = jnp.sum(jnp.where(eye, a_row, 0.0), axis=1, keepdims=True)
            st_ref[bx, hh] = a_col * st + _bdot_tn(kd2[:, ls], vb)
            on = _rms(o, gn_ref[...])
            gate = go_ref[bx, pl.ds(r0, c), hh * dv:(hh + 1) * dv]
            o_ref[bx, pl.ds(r0, c), hh * dv:(hh + 1) * dv] = (on * gate * jax.nn.sigmoid(gate)).astype(o_ref.dtype)

    for bx in range(bb):
        for ci in range(nc):
            chunk(bx, ci)


def _split_dot_left(tri, x):
    hi = x.astype(BF16)
    r1 = x - hi.astype(F32)
    mid = r1.astype(BF16)
    lo = (r1 - mid.astype(F32)).astype(BF16)
    return (jnp.dot(tri, hi, preferred_element_type=F32)
            + jnp.dot(tri, mid, preferred_element_type=F32)
            + jnp.dot(tri, lo, preferred_element_type=F32))


def _gla(gq, gk, gv, glog, go, state0, gn, nc, bb):
    bsz, t, _ = gq.shape
    c = math.gcd(t, GLA_CHUNK)
    sub = min(GLA_SUB, c)
    n_groups = t // (c * nc)
    tri = jnp.tril(jnp.ones((c, c), F32)).astype(BF16)
    qk_spec = pl.BlockSpec((bb, c * nc, 2 * GLA_DK), lambda b, p, g: (b, g, p))
    v_spec = pl.BlockSpec((bb, c * nc, 2 * GLA_DV), lambda b, p, g: (b, g, p))
    st_spec = pl.BlockSpec((bb, 2, GLA_DK, GLA_DV), lambda b, p, g: (b, p, 0, 0))
    return pl.pallas_call(
        functools.partial(_gla_kernel, c=c, sub=sub, nc=nc, bb=bb),
        grid=(bsz // bb, GLA_HEADS // 2, n_groups),
        in_specs=[qk_spec, qk_spec, v_spec, qk_spec, v_spec, st_spec,
                  pl.BlockSpec((1, GLA_DV), lambda b, p, g: (0, 0)),
                  pl.BlockSpec((c, c), lambda b, p, g: (0, 0))],
        out_specs=(v_spec, st_spec),
        out_shape=(jax.ShapeDtypeStruct((bsz, t, GLA_HEADS * GLA_DV), BF16),
                   jax.ShapeDtypeStruct(state0.shape, F32)),
        compiler_params=_cparams(("parallel", "parallel", "arbitrary")), name="gla",
    )(gq, gk, gv, glog, go, state0, gn, tri)


def _mix_out_mem_kernel(mla_ref, gla_ref, h_ref, woa_ref, wob_ref, gm_ref, wq_ref, mk_ref, mv_ref, wo_ref,
                        o_ref, *, groups, r, interleaved):
    h1 = (h_ref[...] + jnp.dot(mla_ref[...], woa_ref[...], preferred_element_type=F32)
          + jnp.dot(gla_ref[...], wob_ref[...], preferred_element_type=F32))
    xn = _rms(h1, gm_ref[...]).astype(BF16)
    scale = MEM_HEAD_DIM ** -0.5
    if interleaved:
        q = jnp.dot(xn, wq_ref[...], preferred_element_type=F32)
        hr = MEM_HEADS * r
        s_parts = []
        for gi in range(groups):
            qg = jnp.concatenate([q[gi * r:(gi + 1) * r, hh * MEM_HEAD_DIM:(hh + 1) * MEM_HEAD_DIM]
                                  for hh in range(MEM_HEADS)], axis=0)
            s_parts.append(_bdot_nt(qg, mk_ref[gi]))
        s = (jnp.concatenate(s_parts, axis=0) if groups > 1 else s_parts[0]) * scale
        row = lax.broadcasted_iota(jnp.int32, s.shape, 0)
        col = lax.broadcasted_iota(jnp.int32, s.shape, 1)
        own = (col & (MEM_HEADS - 1)) == ((row >> (r.bit_length() - 1)) & (MEM_HEADS - 1))
        s = jnp.where(own, s, NEG)
        s = s - jnp.max(s, axis=-1, keepdims=True)
        p = jnp.exp(s)
        p = (p / jnp.sum(p, axis=-1, keepdims=True)).astype(BF16)
        outs = []
        for gi in range(groups):
            og = _bdot(p[gi * hr:(gi + 1) * hr], mv_ref[gi])
            outs.append(jnp.concatenate([og[hh * r:(hh + 1) * r] for hh in range(MEM_HEADS)], axis=1))
        o = jnp.concatenate(outs, axis=0) if groups > 1 else outs[0]
    else:
        q = jnp.dot(xn, wq_ref[...], preferred_element_type=F32).astype(BF16)
        heads = []
        for hh in range(MEM_HEADS):
            ls = slice(hh * MEM_HEAD_DIM, (hh + 1) * MEM_HEAD_DIM)
            s = _bdot_nt(q[:, ls], mk_ref[0, :, ls]) * scale
            s = s - jnp.max(s, axis=-1, keepdims=True)
            p = jnp.exp(s)
            p = p / jnp.sum(p, axis=-1, keepdims=True)
            heads.append(_bdot(p, mv_ref[0, :, ls]))
        o = jnp.concatenate(heads, axis=1)
    o_ref[...] = h1 + jnp.dot(o.astype(BF16), wo_ref[...], preferred_element_type=F32)


def _mix_out_mem(mla_o, gla_o, h, woa, wob, gm, wq, mk, mv, wo, groups, r):
    t = h.shape[0]
    tm = groups * r
    row = lambda w: pl.BlockSpec((tm, w), lambda i: (i, 0))
    interleaved = mk.shape[-1] == MEM_HEAD_DIM
    assert not interleaved or (r & (r - 1) == 0 and MEM_HEADS & (MEM_HEADS - 1) == 0)
    assert interleaved or groups == 1
    if interleaved:
        kv_spec = pl.BlockSpec((groups, MEM_TOKENS * MEM_HEADS, MEM_HEAD_DIM), lambda i: (i, 0, 0))
    else:
        kv_spec = pl.BlockSpec((1, MEM_TOKENS, MEM_WIDTH), lambda i: (0, 0, 0))
    return pl.pallas_call(
        functools.partial(_mix_out_mem_kernel, groups=groups, r=r, interleaved=interleaved),
        grid=(t // tm,),
        in_specs=[row(512), row(512), row(D_MODEL), _full(woa.shape), _full(wob.shape), _full(gm.shape),
                  _full(wq.shape), kv_spec, kv_spec, _full(wo.shape)],
        out_specs=row(D_MODEL),
        out_shape=jax.ShapeDtypeStruct((t, D_MODEL), F32),
        compiler_params=_cparams(("parallel",)), name="mix_out_mem",
    )(mla_o, gla_o, h, woa, wob, gm, wq, mk, mv, wo)


MOE_TB = 512
MOE_CHUNK = 16
MOE_TM = 256
MOE_CPT = MOE_TM // MOE_CHUNK
MOE_ROW_BLOCK = 512


def _moe_local_rows(tb):
    worst = TOP_K * tb + N_EXPERTS * (MOE_CHUNK - 1)
    return -(-worst // MOE_ROW_BLOCK) * MOE_ROW_BLOCK


def _route_kernel(hp_ref, hs_ref, g_ref, wrt_ref, brc_ref, u_ref, ltri_ref, xl_ref, pos_ref, gate_ref, cnt_ref,
                  *, rl, nbp):
    tb = hp_ref.shape[0]
    h = jnp.where(pl.program_id(0) < nbp, hp_ref[...], hs_ref[...])
    xn = _rms(h, g_ref[...])
    x_hi = xn.astype(BF16)
    x_lo = (xn - x_hi.astype(F32)).astype(BF16)
    w = wrt_ref[...]
    w_hi = w.astype(BF16)
    w_lo = (w - w_hi.astype(F32)).astype(BF16)
    logits = _bdot_nt(w_hi, x_hi) + _bdot_nt(w_lo, x_hi) + _bdot_nt(w_hi, x_lo) + brc_ref[...]
    e_iota = lax.broadcasted_iota(jnp.int32, (N_EXPERTS, tb), 0)
    work = logits
    sel = jnp.zeros((N_EXPERTS, tb), jnp.bool_)
    top = None
    for _ in range(TOP_K):
        m = jnp.max(work, axis=0, keepdims=True)
        if top is None:
            top = m
        idx = jnp.min(jnp.where(work == m, e_iota, N_EXPERTS), axis=0, keepdims=True)
        pick = e_iota == idx
        sel = jnp.logical_or(sel, pick)
        work = jnp.where(pick, -jnp.inf, work)
    ex = jnp.where(sel, jnp.exp(logits - top), 0.0)
    gates = ex / jnp.sum(ex, axis=0, keepdims=True)
    self32 = jnp.where(sel, 1.0, 0.0)
    prefix = jnp.dot(self32.astype(BF16), u_ref[...], preferred_element_type=F32)
    cnt = jnp.sum(self32, axis=1, keepdims=True)
    cnt_ref[0] = jnp.broadcast_to(cnt, (N_EXPERTS, 128))
    padded = jnp.floor((cnt + (MOE_CHUNK - 1)) * (1.0 / MOE_CHUNK)) * MOE_CHUNK
    off = jnp.dot(ltri_ref[...], jnp.broadcast_to(padded, (N_EXPERTS, 128)).astype(BF16),
                  preferred_element_type=F32)[:, 0:1]
    pos = off + prefix
    pending = sel
    pos_rows, gate_rows = [], []
    for _ in range(TOP_K):
        emin = jnp.min(jnp.where(pending, e_iota, N_EXPERTS), axis=0, keepdims=True)
        pick = e_iota == emin
        pos_rows.append(jnp.sum(jnp.where(pick, pos, 0.0), axis=0, keepdims=True))
        gate_rows.append(jnp.sum(jnp.where(pick, gates, 0.0), axis=0, keepdims=True))
        pending = jnp.logical_and(pending, jnp.logical_not(pick))
    zeros4 = jnp.zeros((8 - TOP_K, tb), F32)
    pos8 = jnp.concatenate(pos_rows + [zeros4 - 1.0], axis=0)
    pos_ref[0] = pos8.astype(jnp.int32)
    gate_ref[0] = jnp.concatenate(gate_rows + [zeros4], axis=0)
    pk = [p.astype(jnp.int32) for p in pos_rows]
    for rb in range(rl // MOE_ROW_BLOCK):
        r_iota = rb * MOE_ROW_BLOCK + lax.broadcasted_iota(jnp.int32, (MOE_ROW_BLOCK, tb), 0)
        hit = r_iota == pk[0]
        for k in range(1, TOP_K):
            hit = jnp.logical_or(hit, r_iota == pk[k])
        onehot = jnp.where(hit, 1.0, 0.0).astype(BF16)
        xl_ref[0, rb * MOE_ROW_BLOCK:(rb + 1) * MOE_ROW_BLOCK, :] = jnp.dot(
            onehot, x_hi, preferred_element_type=F32).astype(BF16)


def _moe_route(hp, hs, g, wrt, brc, tb, rl):
    nbp = hp.shape[0] // tb
    nb = nbp + hs.shape[0] // tb
    u = jnp.triu(jnp.ones((tb, tb), F32), 1).astype(BF16)
    ltri = jnp.tril(jnp.ones((N_EXPERTS, N_EXPERTS), F32), -1).astype(BF16)
    c2 = lambda a: pl.BlockSpec(a.shape, lambda i: (0, 0))
    return pl.pallas_call(
        functools.partial(_route_kernel, rl=rl, nbp=nbp),
        grid=(nb,),
        in_specs=[pl.BlockSpec((tb, D_MODEL), lambda i: (jnp.minimum(i, nbp - 1), 0)),
                  pl.BlockSpec((tb, D_MODEL), lambda i: (jnp.maximum(i - nbp, 0), 0)),
                  c2(g), c2(wrt), c2(brc), c2(u), c2(ltri)],
        out_specs=(pl.BlockSpec((1, rl, D_MODEL), lambda i: (i, 0, 0)),
                   pl.BlockSpec((1, 8, tb), lambda i: (i, 0, 0)),
                   pl.BlockSpec((1, 8, tb), lambda i: (i, 0, 0)),
                   pl.BlockSpec((1, N_EXPERTS, 128), lambda i: (i, 0, 0))),
        out_shape=(jax.ShapeDtypeStruct((nb, rl, D_MODEL), BF16),
                   jax.ShapeDtypeStruct((nb, 8, tb), jnp.int32),
                   jax.ShapeDtypeStruct((nb, 8, tb), F32),
                   jax.ShapeDtypeStruct((nb, N_EXPERTS, 128), F32)),
        compiler_params=_cparams(("parallel",)), name="moe_route",
    )(hp, hs, g, wrt, brc, u, ltri)


def _moe_tables(cnt, rl, n_tiles):
    nb = cnt.shape[0]
    cpl = rl // MOE_CHUNK
    nch = (cnt + (MOE_CHUNK - 1)) // MOE_CHUNK
    loc_off = jnp.cumsum(nch, axis=1) - nch
    seg_end = jnp.cumsum(nch, axis=0)
    seg_start = seg_end - nch
    tot = seg_end[-1]
    totp = (tot + (MOE_CPT - 1)) // MOE_CPT * MOE_CPT
    e_end = jnp.cumsum(totp)
    e_start = e_end - totp
    n_used = e_end[-1] // MOE_CPT
    tiles = jnp.arange(n_tiles, dtype=jnp.int32)
    first = tiles * MOE_CPT
    e_of = jnp.minimum(jnp.sum(first[:, None] >= e_end[None, :], axis=1), N_EXPERTS - 1).astype(jnp.int32)
    o = (first - e_start[e_of])[:, None] + jnp.arange(MOE_CPT, dtype=jnp.int32)[None, :]
    valid = jnp.logical_and(o < tot[e_of][:, None], (first < e_end[-1])[:, None])
    se = seg_end.T[e_of]
    b_of = jnp.minimum(jnp.sum(o[:, :, None] >= se[:, None, :], axis=2), nb - 1)
    pick = b_of[:, :, None] == jnp.arange(nb, dtype=jnp.int32)[None, None, :]
    run_loc = (loc_off - seg_start).T[e_of]
    loc = jnp.sum(jnp.where(pick, run_loc[:, None, :], 0), axis=2) + o
    real = (b_of * cpl + loc).reshape(-1)
    valid = valid.reshape(-1)
    slots = jnp.arange(n_tiles * MOE_CPT, dtype=jnp.int32)
    spare = nb * cpl + (slots % (2 * MOE_CPT))
    src = jnp.where(valid, real, 0).astype(jnp.int32)
    dst = jnp.where(valid, real, spare).astype(jnp.int32)
    te = e_of[jnp.minimum(tiles, n_used - 1)]
    return te, src, dst, n_used.astype(jnp.int32).reshape(1)


def _experts_kernel(te_ref, src_ref, dst_ref, nu_ref, xl_hbm, wup_ref, bup_ref, wdn_ref, bdn_ref, yinit_hbm,
                    yl_hbm, xbuf, ybuf, wub, wdb, isem, osem):
    del yinit_hbm
    i = pl.program_id(0)
    nu = nu_ref[0]
    slot = lax.rem(i, 2)

    def in_copy(tile, sl, c):
        return pltpu.make_async_copy(xl_hbm.at[src_ref[tile * MOE_CPT + c]],
                                     xbuf.at[sl, pl.ds(c * MOE_CHUNK, MOE_CHUNK), :], isem.at[sl])

    def out_copy(tile, sl, c):
        return pltpu.make_async_copy(ybuf.at[sl, pl.ds(c * MOE_CHUNK, MOE_CHUNK), :],
                                     yl_hbm.at[dst_ref[tile * MOE_CPT + c]], osem.at[sl])

    @pl.when(i == 0)
    def _():
        for c in range(MOE_CPT):
            in_copy(0, 0, c).start()

    @pl.when(i < nu)
    def _():
        @pl.when(i + 1 < nu)
        def _():
            for c in range(MOE_CPT):
                in_copy(i + 1, 1 - slot, c).start()

        for c in range(MOE_CPT):
            in_copy(i, slot, c).wait()

        @pl.when(i >= 2)
        def _():
            for c in range(MOE_CPT):
                out_copy(i - 2, slot, c).wait()

        @pl.when(jnp.logical_or(i == 0, te_ref[i] != te_ref[jnp.maximum(i - 1, 0)]))
        def _():
            wub[...] = wup_ref[0].astype(BF16)
            wdb[...] = wdn_ref[0].astype(BF16)

        x = xbuf[slot]
        hu = jnp.dot(x, wub[...], preferred_element_type=F32) + bup_ref[0]
        gate = jnp.minimum(hu[:, 0:D_FF], SWIGLU_LIMIT)
        up = jnp.clip(hu[:, D_FF:2 * D_FF], -SWIGLU_LIMIT, SWIGLU_LIMIT)
        act = (up + 1.0) * gate * jax.nn.sigmoid(SWIGLU_ALPHA * gate)
        y = jnp.dot(act.astype(BF16), wdb[...], preferred_element_type=F32) + bdn_ref[0]
        ybuf[slot] = y.astype(BF16)
        for c in range(MOE_CPT):
            out_copy(i, slot, c).start()

        @pl.when(i == nu - 1)
        def _():
            for c in range(MOE_CPT):
                out_copy(i, slot, c).wait()

            @pl.when(i >= 1)
            def _():
                for c in range(MOE_CPT):
                    out_copy(i - 1, 1 - slot, c).wait()


def _moe_experts(te, src, dst, n_used, xl, wup, bup, wdn, bdn, n_tiles):
    nb, rl, _ = xl.shape
    cpl = rl // MOE_CHUNK
    xl_chunks = xl.reshape(nb * cpl, MOE_CHUNK, D_MODEL)
    y_init = jnp.zeros(((nb + 1) * cpl, MOE_CHUNK, D_MODEL), BF16)
    wmap = lambda i, te, src, dst, nu: (te[i], 0, 0)
    grid_spec = pltpu.PrefetchScalarGridSpec(
        num_scalar_prefetch=4, grid=(n_tiles,),
        in_specs=[pl.BlockSpec(memory_space=pl.ANY),
                  pl.BlockSpec((1, D_MODEL, 2 * D_FF), wmap), pl.BlockSpec((1, 1, 2 * D_FF), wmap),
                  pl.BlockSpec((1, D_FF, D_MODEL), wmap), pl.BlockSpec((1, 1, D_MODEL), wmap),
                  pl.BlockSpec(memory_space=pl.ANY)],
        out_specs=pl.BlockSpec(memory_space=pl.ANY),
        scratch_shapes=[pltpu.VMEM((2, MOE_TM, D_MODEL), BF16), pltpu.VMEM((2, MOE_TM, D_MODEL), BF16),
                        pltpu.VMEM((D_MODEL, 2 * D_FF), BF16), pltpu.VMEM((D_FF, D_MODEL), BF16),
                        pltpu.SemaphoreType.DMA((2,)), pltpu.SemaphoreType.DMA((2,))])
    yl = pl.pallas_call(
        _experts_kernel, grid_spec=grid_spec,
        out_shape=jax.ShapeDtypeStruct(y_init.shape, BF16),
        input_output_aliases={9: 0},
        compiler_params=_cparams(("arbitrary",)), name="moe_experts",
    )(te, src, dst, n_used, xl_chunks, wup, bup, wdn, bdn, y_init)
    return yl.reshape(nb + 1, rl, D_MODEL)


def _combine_kernel(yl_ref, pos_ref, gate_ref, hp_ref, hs_ref, gf_ref, yp_ref, ys_ref, *, rl, nbp):
    tb = hp_ref.shape[0]
    i = pl.program_id(0)
    pos = pos_ref[0]
    gate = gate_ref[0]
    acc = jnp.where(i < nbp, hp_ref[...], hs_ref[...])
    for rb in range(rl // MOE_ROW_BLOCK):
        r_iota = rb * MOE_ROW_BLOCK + lax.broadcasted_iota(jnp.int32, (MOE_ROW_BLOCK, tb), 0)
        w = jnp.zeros((MOE_ROW_BLOCK, tb), F32)
        for k in range(TOP_K):
            w = jnp.where(r_iota == pos[k:k + 1], gate[k:k + 1], w)
        acc = acc + _bdot_tn(w, yl_ref[0, rb * MOE_ROW_BLOCK:(rb + 1) * MOE_ROW_BLOCK, :])
    y = _rms(acc, gf_ref[...])

    @pl.when(i < nbp)
    def _():
        yp_ref[...] = y

    @pl.when(i >= nbp)
    def _():
        ys_ref[...] = y


def _moe_combine(yl, pos, gate, hp, hs, gf, tb):
    rl = yl.shape[1]
    nbp = hp.shape[0] // tb
    nb = nbp + hs.shape[0] // tb
    p_map = lambda i: (jnp.minimum(i, nbp - 1), 0)
    s_map = lambda i: (jnp.maximum(i - nbp, 0), 0)
    return pl.pallas_call(
        functools.partial(_combine_kernel, rl=rl, nbp=nbp),
        grid=(nb,),
        in_specs=[pl.BlockSpec((1, rl, D_MODEL), lambda i: (i, 0, 0)),
                  pl.BlockSpec((1, 8, tb), lambda i: (i, 0, 0)), pl.BlockSpec((1, 8, tb), lambda i: (i, 0, 0)),
                  pl.BlockSpec((tb, D_MODEL), p_map), pl.BlockSpec((tb, D_MODEL), s_map),
                  pl.BlockSpec(gf.shape, lambda i: (0, 0))],
        out_specs=(pl.BlockSpec((tb, D_MODEL), p_map), pl.BlockSpec((tb, D_MODEL), s_map)),
        out_shape=(jax.ShapeDtypeStruct(hp.shape, F32), jax.ShapeDtypeStruct(hs.shape, F32)),
        compiler_params=_cparams(("arbitrary",)), name="moe_combine",
    )(yl, pos, gate, hp, hs, gf)


def _moe(hp, hs, g, wr, br, wup, bup, wdn, bdn, gf):
    tb = math.gcd(math.gcd(hp.shape[0], hs.shape[0]), MOE_TB)
    rl = _moe_local_rows(tb)
    nb = (hp.shape[0] + hs.shape[0]) // tb
    n_tiles = -(-(nb * (rl // MOE_CHUNK) + N_EXPERTS * (MOE_CPT - 1)) // MOE_CPT)
    xl, pos, gate, cnt = _moe_route(hp, hs, g, wr.T, br.reshape(N_EXPERTS, 1), tb, rl)
    te, src, dst, n_used = _moe_tables(cnt[:, :, 0].astype(jnp.int32), rl, n_tiles)
    yl = _moe_experts(te, src, dst, n_used, xl, wup, bup, wdn, bdn, n_tiles)
    return _moe_combine(yl, pos, gate, hp, hs, gf, tb)


def _rope_tables(pos):
    half = MLA_ROPE // 2
    inv = ROPE_THETA ** (-jnp.arange(half, dtype=F32) / half)
    ang = pos.astype(F32)[:, None] * inv[None, :]
    cos, sin = jnp.cos(ang), jnp.sin(ang)
    pad = jnp.zeros((pos.shape[0], 128 - MLA_ROPE), F32)
    return jnp.concatenate([cos, cos, pad], axis=1), jnp.concatenate([sin, sin, pad], axis=1)


def _rot_cols(w):
    half = w.shape[-1] // 2
    return jnp.concatenate([-w[..., half:], w[..., :half]], axis=-1)


def _pad_cols(w, width):
    return jnp.pad(w, ((0, 0), (0, width - w.shape[1])))


def _prep_mix_weights(norm_g, w_in, q_a_norm_g, w_uq, kv_a_norm_g, w_uk, gla_gate_w2, gla_gate_b):
    o = 0
    parts = {}
    for name, size in (("cq", MLA_Q_RANK), ("ckv", MLA_KV_RANK), ("kr", MLA_ROPE), ("gq", 256), ("gk", 256),
                       ("gv", 512), ("gg", GLA_GATE_RANK), ("go", 512)):
        parts[name] = w_in[:, o:o + size]
        o += size
    w1 = jnp.concatenate([parts["cq"], parts["ckv"], parts["gq"] * (GLA_DK ** -0.5), parts["gk"], parts["gv"],
                          parts["go"]], axis=1).astype(BF16)
    w2 = jnp.concatenate([_pad_cols(parts["kr"], 128), _pad_cols(_rot_cols(parts["kr"]), 128),
                          _pad_cols(parts["gg"], 128)], axis=1).astype(BF16)
    wuq = w_uq.reshape(MLA_Q_RANK, MLA_HEADS, MLA_NOPE + MLA_ROPE)
    wuq_n = wuq[:, :, :MLA_NOPE].reshape(MLA_Q_RANK, MLA_HEADS * MLA_NOPE).astype(BF16)
    wr = wuq[:, :, MLA_NOPE:]
    widen = lambda w: jnp.pad(w, ((0, 0), (0, 0), (0, 128 - MLA_ROPE))).reshape(MLA_Q_RANK, MLA_HEADS * 128)
    wuq_r = widen(wr).astype(BF16)
    wuq_rr = widen(_rot_cols(wr)).astype(BF16)
    wukt = jnp.transpose(w_uk, (1, 2, 0)).astype(BF16)
    wg2 = jnp.pad(gla_gate_w2, ((0, 128 - GLA_GATE_RANK), (0, 0))).astype(BF16)
    return (norm_g[None], w1, w2, q_a_norm_g[None], wuq_n, wuq_r, wuq_rr, kv_a_norm_g[None], wukt, wg2,
            gla_gate_b[None])


def _prep_wuv(w_uv):
    w = jnp.transpose(w_uv, (1, 0, 2))
    z = jnp.zeros_like(w)
    even = jnp.concatenate([w, z], axis=-1)
    odd = jnp.concatenate([z, w], axis=-1)
    sel = (jnp.arange(MLA_HEADS) % 2 == 0)[:, None, None]
    return jnp.where(sel, even, odd).astype(BF16)


def kernel(x_prompt, x_sample, mem_prompt, cache_ckv, cache_krope, state_gla, cache_mem_k, cache_mem_v,
           page_table, norm_mix_g, w_in, q_a_norm_g, w_uq, kv_a_norm_g, w_uk, w_uv, gla_gate_w2, gla_gate_b,
           gla_norm_g, w_out, norm_mem_g, mem_in_norm_g, w_mem_q, w_mem_k, w_mem_v, w_mem_o, norm_moe_g,
           w_router, b_router, w_moe_up, b_moe_up, w_moe_down, b_moe_down, norm_final_g):
    depth = w_in.shape[0]
    assert depth == 1 and x_prompt.shape[0] == 1
    l = 0
    b, t, _ = x_prompt.shape
    db, ts, _ = x_sample.shape
    n_pages = page_table.shape[1]
    past = n_pages * PAGE

    mixw = _prep_mix_weights(norm_mix_g[l], w_in[l], q_a_norm_g[l], w_uq[l], kv_a_norm_g[l], w_uk[l],
                             gla_gate_w2[l], gla_gate_b[l])
    wuv_pair = _prep_wuv(w_uv[l])
    woa = w_out[l][:MLA_HEADS * MLA_V].astype(BF16)
    wob = w_out[l][MLA_HEADS * MLA_V:].astype(BF16)
    gn = gla_norm_g[l][None]
    gm = norm_mem_g[l][None]
    wmq = w_mem_q[l].astype(BF16)
    wmo = w_mem_o[l].astype(BF16)
    wup = w_moe_up[l]
    wdn = w_moe_down[l]
    bup = b_moe_up[l][:, None, :]
    bdn = b_moe_down[l][:, None, :]
    moe_args = (norm_moe_g[l][None], w_router[l], b_router[l], wup, bup, wdn, bdn, norm_final_g[None])

    hp = x_prompt.reshape(t, D_MODEL)
    cs_p, sn_p = _rope_tables(jnp.arange(t))
    tm_p = min(512, t)
    q_p, ckv_p, kr_p, kcat_p, gq, gk, gv, gl, go = _mix_in(hp, cs_p, sn_p, mixw, tm_p)
    tq = min(256, t)
    tk = min(512, t)
    mla_p = _mla_prompt(q_p, kcat_p, wuv_pair, tq, tk)
    zero_state = jnp.zeros((1, GLA_HEADS, GLA_DK, GLA_DV), F32)
    nc_p = max(1, min(4, t // GLA_CHUNK))
    gla_p, st_p = _gla(gq[None], gk[None], gv[None], gl[None], go[None], zero_state, gn, nc_p, 1)
    mk_p, mv_p = _mem_kv(mem_prompt[0], mem_in_norm_g[l][None], w_mem_k[l].astype(BF16), w_mem_v[l].astype(BF16))
    hp = _mix_out_mem(mla_p, gla_p[0], hp, woa, wob, gm, wmq, mk_p[None], mv_p[None], wmo, 1, tm_p)

    n_s = db * ts
    hs = x_sample.reshape(n_s, D_MODEL)
    cs_s, sn_s = _rope_tables(past + jnp.arange(ts))
    cs_s = jnp.tile(cs_s, (db, 1))
    sn_s = jnp.tile(sn_s, (db, 1))
    tm_s = min(512, n_s)
    q_s, ckv_s, kr_s, kcat_s, gq, gk, gv, gl, go = _mix_in(hs, cs_s, sn_s, mixw, tm_s)
    n_pg = max(1, n_pages // 4)
    mla_s = _mla_sample(page_table, q_s.reshape(MLA_HEADS, db, ts, QK_WIDTH), kcat_s.reshape(db, ts, QK_WIDTH),
                        wuv_pair, cache_ckv[l], jnp.swapaxes(cache_krope[l], 1, 2), n_pg)
    r3 = lambda a: a.reshape(db, ts, a.shape[-1])
    gla_s, st_s = _gla(r3(gq), r3(gk), r3(gv), r3(gl), r3(go), state_gla[l], gn, 1, math.gcd(db, 8))
    groups = min(8, db)
    hs = _mix_out_mem(mla_s.reshape(n_s, -1), gla_s.reshape(n_s, -1), hs, woa, wob, gm, wmq,
                      cache_mem_k[l].reshape(db, MEM_TOKENS * MEM_HEADS, MEM_HEAD_DIM),
                      cache_mem_v[l].reshape(db, MEM_TOKENS * MEM_HEADS, MEM_HEAD_DIM), wmo, groups, ts)
    y_p, y_s = _moe(hp, hs, *moe_args)

    return (y_p.reshape(b, t, D_MODEL), y_s.reshape(db, ts, D_MODEL),
            ckv_p.reshape(1, b, t, MLA_KV_RANK), kr_p.reshape(1, b, t, MLA_ROPE),
            st_p.reshape(1, b, GLA_HEADS, GLA_DK, GLA_DV),
            mk_p.reshape(1, b, MEM_TOKENS, MEM_HEADS, MEM_HEAD_DIM),
            mv_p.reshape(1, b, MEM_TOKENS, MEM_HEADS, MEM_HEAD_DIM),
            ckv_s.reshape(1, db, ts, MLA_KV_RANK), kr_s.reshape(1, db, ts, MLA_ROPE),
            st_s.reshape(1, db, GLA_HEADS, GLA_DK, GLA_DV))
```
